```python
import math
import jax, jax.numpy as jnp
from jax import lax
import numpy as np

D_MODEL = 2048
BATCH = 4
SEQ = 4096
DEPTH = 1
DEC_BATCH = 32
DEC_SEQ = 4
PAST_LEN = 16384
PAGE_SIZE = 128

HEAD_DIM = 128
GDN_HEADS = D_MODEL // (2 * HEAD_DIM)
GDN_DK = HEAD_DIM
GDN_DV = HEAD_DIM
GDN_QK = GDN_HEADS * GDN_DK
GDN_CONV_DIM = 2 * GDN_QK + GDN_HEADS * GDN_DV
CONV_WIDTH = 4
GDN_CHUNK = 64
NSA_HEADS = D_MODEL // (2 * HEAD_DIM)
NSA_KV_HEADS = 2
NSA_GROUP = NSA_HEADS // NSA_KV_HEADS
N_BRANCH = 3
CMP_BLOCK = 64
SEL_BLOCK = CMP_BLOCK
SEL_TOPK = 16
WINDOW = 512
WIN_QBLOCK = 128
SEL_QBLOCK = 32
MIX_DIM = GDN_HEADS * GDN_DV + NSA_HEADS * HEAD_DIM
REL_BUCKETS = 32
REL_MAX_DIST = 8192
N_EXPERTS = 256
MOE_TOPK = 8
N_GROUPS = 8
TOPK_GROUPS = 4
EXPERT_FF = D_MODEL // 4
SHARED_FF = D_MODEL // 4
ROUTED_SCALE = 2.5
NORM_EPS = 1e-6
NEG_INF = -1e30
ATTN_SCALE = HEAD_DIM ** -0.5
IN_SIZES = (GDN_CONV_DIM, GDN_HEADS * GDN_DV, GDN_HEADS, GDN_HEADS, NSA_HEADS * HEAD_DIM,
            N_BRANCH * 2 * NSA_KV_HEADS * HEAD_DIM, N_BRANCH * NSA_HEADS)
IN_OFFSETS = tuple(int(v) for v in np.cumsum(IN_SIZES)[:-1])
IN_DIM = sum(IN_SIZES)

kernel_name = 'hybrid_gdn_nsa_moe_step'


def rms_norm(x, g):
    xf = x.astype(jnp.float32)
    y = xf * lax.rsqrt(jnp.mean(xf * xf, axis=-1, keepdims=True) + NORM_EPS)
    return (y * g.astype(jnp.float32)).astype(x.dtype)


def l2_normalize(x):
    return x * lax.rsqrt(jnp.sum(x * x, axis=-1, keepdims=True) + NORM_EPS)


def t5_bucket(dist):
    n = jnp.maximum(dist, 0)
    max_exact = REL_BUCKETS // 2
    nf = jnp.maximum(n, 1).astype(jnp.float32)
    large = max_exact + (jnp.log(nf / max_exact) / math.log(REL_MAX_DIST / max_exact)
                         * (REL_BUCKETS - max_exact)).astype(jnp.int32)
    return jnp.where(n < max_exact, n, jnp.minimum(large, REL_BUCKETS - 1))


def masked_probs(s, mask):
    s = jnp.where(mask, s.astype(jnp.float32), NEG_INF)
    return jax.nn.softmax(s, axis=-1) * mask


def short_conv(x, buf, w):
    L = x.shape[1]
    xp = jnp.concatenate([buf.astype(x.dtype), x], axis=1)
    y = sum(xp[:, j:j + L] * w[j] for j in range(CONV_WIDTH))
    return jax.nn.silu(y), xp[:, L:]


def gated_delta_chunked(q, k, v, g, beta, s0):
    B, H, L, dk = q.shape
    dv = v.shape[-1]
    C = math.gcd(L, GDN_CHUNK)
    n = L // C

    def chunks(t):
        return t.reshape(B, H, n, C, *t.shape[3:])

    q, k, v, g, beta = (chunks(t) for t in (q, k, v, g, beta))
    gc = jnp.cumsum(g, axis=-1)
    lower = jnp.tril(jnp.ones((C, C), bool))
    strict = jnp.tril(jnp.ones((C, C), bool), -1)
    decay = jnp.exp(jnp.where(lower, gc[..., :, None] - gc[..., None, :], NEG_INF))
    kb = k * beta[..., None]
    lmat = jnp.where(strict, jnp.einsum('bhncd,bhnjd->bhncj', kb, k) * decay, 0.0)
    rhs = jnp.concatenate([v * beta[..., None], kb * jnp.exp(gc)[..., None]], axis=-1)
    sol = lax.linalg.triangular_solve(lmat + jnp.eye(C, dtype=lmat.dtype), rhs,
                                      left_side=True, lower=True, unit_diagonal=True)
    u, w = sol[..., :dv], sol[..., dv:]
    qk = jnp.where(lower, jnp.einsum('bhncd,bhnjd->bhncj', q, k) * decay, 0.0)
    qg = q * jnp.exp(gc)[..., None]
    kg = k * jnp.exp(gc[..., -1:] - gc)[..., None]
    g_last = jnp.exp(gc[..., -1])

    def step(S, xs):
        u_i, w_i, qk_i, qg_i, kg_i, gl_i = xs
        v_new = u_i - jnp.einsum('bhcd,bhde->bhce', w_i, S)
        o = jnp.einsum('bhcd,bhde->bhce', qg_i, S) + jnp.einsum('bhcj,bhje->bhce', qk_i, v_new)
        S = S * gl_i[..., None, None] + jnp.einsum('bhcd,bhce->bhde', kg_i, v_new)
        return S, o

    xs = tuple(jnp.moveaxis(t, 2, 0) for t in (u, w, qk, qg, kg, g_last))
    S, o = lax.scan(step, s0, xs)
    return jnp.moveaxis(o, 0, 2).reshape(B, H, L, dv), S


def gdn_mixer(qkv, z, b_raw, a_raw, conv_buf, s0, conv_w, a_log, dt_bias, norm_w):
    B, L, _ = qkv.shape
    qkv_c, new_buf = short_conv(qkv, conv_buf, conv_w)
    qc, kc, vc = jnp.split(qkv_c, [GDN_QK, 2 * GDN_QK], axis=-1)

    def heads(t, d):
        return t.reshape(B, L, GDN_HEADS, d).transpose(0, 2, 1, 3).astype(jnp.float32)

    q = l2_normalize(heads(qc, GDN_DK)) * (GDN_DK ** -0.5)
    k = l2_normalize(heads(kc, GDN_DK))
    v = heads(vc, GDN_DV)
    beta = jax.nn.sigmoid(b_raw.astype(jnp.float32)).transpose(0, 2, 1)
    g = (-jnp.exp(a_log.astype(jnp.float32))
         * jax.nn.softplus(a_raw.astype(jnp.float32) + dt_bias.astype(jnp.float32))).transpose(0, 2, 1)
    o, s_new = gated_delta_chunked(q, k, v, g, beta, s0.astype(jnp.float32))
    o = o.transpose(0, 2, 1, 3)
    o = (o * lax.rsqrt(jnp.mean(o * o, axis=-1, keepdims=True) + NORM_EPS) * norm_w.astype(jnp.float32)
         * jax.nn.silu(z.reshape(B, L, GDN_HEADS, GDN_DV).astype(jnp.float32)))
    return o.reshape(B, L, GDN_HEADS * GDN_DV).astype(qkv.dtype), new_buf, s_new.astype(s0.dtype)


def compress_blocks(kv, pe, w1, b1, w2):
    B, Lk = kv.shape[:2]
    nb = Lk // CMP_BLOCK
    blk = kv[:, :nb * CMP_BLOCK].reshape(B, nb, CMP_BLOCK, 2, NSA_KV_HEADS, HEAD_DIM)
    blk = blk + pe[:, :, None, :]
    flat = blk.transpose(0, 1, 3, 4, 2, 5).reshape(B, nb, 2, NSA_KV_HEADS, CMP_BLOCK * HEAD_DIM)
    hid = jax.nn.silu(jnp.einsum('bnshf,sfe->bnshe', flat, w1) + b1[:, None, :])
    return jnp.einsum('bnshe,sed->bnshd', hid, w2)


def cmp_attend(q, qpos, kvc, rel_g):
    nb = kvc.shape[1]
    bend = jnp.arange(nb, dtype=jnp.int32) * CMP_BLOCK + (CMP_BLOCK - 1)
    dist = qpos[:, None] - bend[None, :]
    bias = rel_g[t5_bucket(dist)].transpose(2, 3, 0, 1)
    s = jnp.einsum('bhgqd,bnhd->bhgqn', q, kvc[:, :, 0]).astype(jnp.float32) * ATTN_SCALE + bias
    p = masked_probs(s, dist >= 0)
    o = jnp.einsum('bhgqn,bnhd->bhgqd', p, kvc[:, :, 1].astype(jnp.float32))
    return o, p


def select_blocks(p, qpos):
    score = jnp.sum(p, axis=2)
    B, Hkv, Q, nb = score.shape
    cur = qpos // SEL_BLOCK
    score = jnp.where(jnp.arange(nb)[None, :] < cur[:, None], score, -1.0)
    width = max(nb, SEL_TOPK - 1)
    score = jnp.pad(score, ((0, 0), (0, 0), (0, 0), (0, width - nb)), constant_values=-1.0)
    top_s, top_i = lax.top_k(score, SEL_TOPK - 1)
    cur_b = jnp.broadcast_to(cur[None, None, :, None], (B, Hkv, Q, 1)).astype(jnp.int32)
    idx = jnp.concatenate([cur_b, top_i.astype(jnp.int32)], axis=-1)
    valid = jnp.concatenate([jnp.ones((B, Hkv, Q, 1), bool), top_s >= 0], axis=-1)
    return idx, valid


def sel_attend(q, qpos, idx, valid, fetch, rel_g):
    B, Hkv, G, Q, dh = q.shape
    qc = math.gcd(Q, SEL_QBLOCK)
    nc = Q // qc
    qs = q.reshape(B, Hkv, G, nc, qc, dh).transpose(3, 0, 1, 2, 4, 5)
    ids = idx.reshape(B, Hkv, nc, qc, SEL_TOPK).transpose(2, 0, 1, 3, 4)
    vals = valid.reshape(B, Hkv, nc, qc, SEL_TOPK).transpose(2, 0, 1, 3, 4)
    ps = qpos.reshape(nc, qc)
    hidx = jnp.arange(Hkv)[None, :, None, None]
    offs = jnp.arange(SEL_BLOCK, dtype=jnp.int32)
    nkeys = SEL_TOPK * SEL_BLOCK

    def one(args):
        qb, ib, vb, pb = args
        kv = fetch(ib)
        kk = kv[..., 0, :].reshape(B, Hkv, qc, nkeys, dh)
        vv = kv[..., 1, :].reshape(B, Hkv, qc, nkeys, dh)
        kpos = (ib[..., None] * SEL_BLOCK + offs).reshape(B, Hkv, qc, nkeys)
        dist = pb[None, None, :, None] - kpos
        mask = jnp.broadcast_to(vb[..., None], (B, Hkv, qc, SEL_TOPK, SEL_BLOCK)).reshape(B, Hkv, qc, nkeys) & (dist >= 0)
        bias = rel_g[t5_bucket(dist), hidx].transpose(0, 1, 4, 2, 3)
        s = jnp.einsum('bhgqd,bhqkd->bhgqk', qb, kk).astype(jnp.float32) * ATTN_SCALE + bias
        p = masked_probs(s, mask[:, :, None])
        return jnp.einsum('bhgqk,bhqkd->bhgqd', p, vv.astype(jnp.float32))

    o = lax.map(one, (qs, ids, vals, ps))
    return o.transpose(1, 2, 3, 0, 4, 5).reshape(B, Hkv, G, Q, dh)


def win_attend(q, qpos, kv, kpos, rel_g):
    dist = qpos[:, None] - kpos[None, :]
    mask = (dist >= 0) & (dist < WINDOW) & (kpos[None, :] >= 0)
    bias = rel_g[t5_bucket(dist)].transpose(2, 3, 0, 1)
    s = jnp.einsum('bhgqd,bkhd->bhgqk', q, kv[:, :, 0]).astype(jnp.float32) * ATTN_SCALE + bias
    p = masked_probs(s, mask)
    return jnp.einsum('bhgqk,bkhd->bhgqd', p, kv[:, :, 1].astype(jnp.float32))


def win_attend_prompt(q, kv, rel_g):
    B, Hkv, G, L, dh = q.shape
    wq = math.gcd(L, WIN_QBLOCK)
    nq = L // wq
    kvp = jnp.pad(kv, ((0, 0), (WINDOW, 0), (0, 0), (0, 0), (0, 0)))
    qs = q.reshape(B, Hkv, G, nq, wq, dh).transpose(3, 0, 1, 2, 4, 5)

    def one(args):
        qb, i = args
        start = i * wq
        band = lax.dynamic_slice_in_dim(kvp, start, WINDOW + wq, axis=1)
        qpos = start + jnp.arange(wq, dtype=jnp.int32)
        kpos = start - WINDOW + jnp.arange(WINDOW + wq, dtype=jnp.int32)
        return win_attend(qb, qpos, band, kpos, rel_g)

    o = lax.map(one, (qs, jnp.arange(nq, dtype=jnp.int32)))
    return o.transpose(1, 2, 3, 0, 4, 5).reshape(B, Hkv, G, L, dh)


def nsa_combine(o_cmp, o_sel, o_win, gate_raw):
    B, L = gate_raw.shape[:2]
    gt = jax.nn.sigmoid(gate_raw.astype(jnp.float32)).reshape(B, L, N_BRANCH, NSA_KV_HEADS, NSA_GROUP)
    gt = gt.transpose(2, 0, 3, 4, 1)[..., None]
    o = gt[0] * o_cmp + gt[1] * o_sel + gt[2] * o_win
    return o.transpose(0, 3, 1, 2, 4).reshape(B, L, NSA_HEADS * HEAD_DIM)


def make_nsa_prompt(rel_g, cmp_params):
    def attend(q, kv_cmp, kv_sel, kv_win):
        B, L = kv_cmp.shape[:2]
        qpos = jnp.arange(L, dtype=jnp.int32)
        o_cmp, p = cmp_attend(q, qpos, compress_blocks(kv_cmp, *cmp_params), rel_g)
        idx, valid = select_blocks(p, qpos)
        nblk = -(-L // SEL_BLOCK)
        store = jnp.pad(kv_sel, ((0, 0), (0, nblk * SEL_BLOCK - L), (0, 0), (0, 0), (0, 0)))
        bidx = jnp.arange(B)[:, None, None, None, None]
        hidx = jnp.arange(NSA_KV_HEADS)[None, :, None, None, None]
        offs = jnp.arange(SEL_BLOCK, dtype=jnp.int32)

        def fetch(ib):
            rows = jnp.clip(ib, 0, nblk - 1)[..., None] * SEL_BLOCK + offs
            return store[bidx, rows, :, hidx]

        o_sel = sel_attend(q, qpos, idx, valid, fetch, rel_g)
        o_win = win_attend_prompt(q, kv_win, rel_g)
        return o_cmp, o_sel, o_win, (kv_cmp, kv_sel, kv_win[:, L - min(WINDOW, L):])
    return attend


def make_nsa_sample(rel_g, cmp_params, pool_cmp, pool_sel, win_buf, page_table):
    def attend(q, kv_cmp, kv_sel, kv_win):
        Bd, L = kv_cmp.shape[:2]
        n_pages = page_table.shape[1]
        past = n_pages * PAGE_SIZE
        qpos = past + jnp.arange(L, dtype=jnp.int32)
        past_cmp = pool_cmp[page_table].reshape(Bd, past, 2, NSA_KV_HEADS, HEAD_DIM).astype(kv_cmp.dtype)
        kvc = jnp.concatenate([compress_blocks(past_cmp, *cmp_params),
                               compress_blocks(kv_cmp, *cmp_params)], axis=1)
        o_cmp, p = cmp_attend(q, qpos, kvc, rel_g)
        idx, valid = select_blocks(p, qpos)
        bpp = PAGE_SIZE // SEL_BLOCK
        n_past_blk = n_pages * bpp
        n_new_blk = -(-L // SEL_BLOCK)
        new_rows = jnp.pad(kv_sel, ((0, 0), (0, n_new_blk * SEL_BLOCK - L), (0, 0), (0, 0), (0, 0)))
        bidx = jnp.arange(Bd)[:, None, None, None, None]
        hidx = jnp.arange(NSA_KV_HEADS)[None, :, None, None, None]
        offs = jnp.arange(SEL_BLOCK, dtype=jnp.int32)

        def fetch(ib):
            ip = jnp.clip(ib, 0, n_past_blk - 1)
            phys = page_table[bidx[..., 0], ip // bpp][..., None]
            from_past = pool_sel[phys, (ip % bpp)[..., None] * SEL_BLOCK + offs, :, hidx]
            rows_new = jnp.clip(ib - n_past_blk, 0, n_new_blk - 1)[..., None] * SEL_BLOCK + offs
            from_new = new_rows[bidx, rows_new, :, hidx]
            return jnp.where((ib >= n_past_blk)[..., None, None, None], from_new, from_past.astype(from_new.dtype))

        o_sel = sel_attend(q, qpos, idx, valid, fetch, rel_g)
        wb = win_buf.shape[1]
        kw = jnp.concatenate([win_buf.astype(kv_win.dtype), kv_win], axis=1)
        kpos = past - wb + jnp.arange(wb + L, dtype=jnp.int32)
        o_win = win_attend(q, qpos, kw, kpos, rel_g)
        return o_cmp, o_sel, o_win, (kv_cmp, kv_sel, kw[:, L:])
    return attend


def route(h, w_router, router_bias):
    T = h.shape[0]
    s = jax.nn.sigmoid(jnp.einsum('td,ed->te', h, w_router).astype(jnp.float32))
    sb = s + router_bias.astype(jnp.float32)
    group_score = jnp.sum(lax.top_k(sb.reshape(T, N_GROUPS, N_EXPERTS // N_GROUPS), 2)[0], axis=-1)
    _, gidx = lax.top_k(group_score, TOPK_GROUPS)
    gmask = jnp.zeros((T, N_GROUPS), bool).at[jnp.arange(T)[:, None], gidx].set(True)
    emask = jnp.repeat(gmask, N_EXPERTS // N_GROUPS, axis=1)
    _, eidx = lax.top_k(jnp.where(emask, sb, NEG_INF), MOE_TOPK)
    w = jnp.take_along_axis(s, eidx, axis=1)
    return eidx, w / jnp.sum(w, axis=-1, keepdims=True) * ROUTED_SCALE


def routed_experts(h, eidx, ew, w_gu, w_down):
    T, D = h.shape
    A = T * MOE_TOPK
    per = max(A // N_EXPERTS, 1)
    mb = min(128, max(8, 1 << (per.bit_length() - 1)))
    n_blk = -(-(A + N_EXPERTS * (mb - 1)) // mb)
    flat_e = eidx.reshape(-1)
    order = jnp.argsort(flat_e)
    se = flat_e[order]
    counts = jnp.bincount(flat_e, length=N_EXPERTS)
    pcounts = (counts + mb - 1) // mb * mb
    pends = jnp.cumsum(pcounts)
    starts = jnp.cumsum(counts) - counts
    dest = (pends - pcounts)[se] + jnp.arange(A) - starts[se]
    rows_tok = jnp.full((n_blk * mb,), T, jnp.int32).at[dest].set((order // MOE_TOPK).astype(jnp.int32))
    rows_w = jnp.zeros((n_blk * mb,), jnp.float32).at[dest].set(ew.reshape(-1)[order])
    blk_e = jnp.minimum(jnp.searchsorted(pends, jnp.arange(n_blk) * mb, side='right'), N_EXPERTS - 1)
    hp = jnp.concatenate([h, jnp.zeros((1, D), h.dtype)], axis=0)

    def step(acc, xs):
        tok, wt, e = xs
        g, u = jnp.split(hp[tok] @ w_gu[e], 2, axis=-1)
        yb = (jax.nn.silu(g) * u) @ w_down[e]
        return acc.at[tok].add(yb.astype(jnp.float32) * wt[:, None]), None

    acc, _ = lax.scan(step, jnp.zeros((T + 1, D), jnp.float32),
                      (rows_tok.reshape(n_blk, mb), rows_w.reshape(n_blk, mb), blk_e))
    return acc[:T].astype(h.dtype)


def moe(h, lw):
    eidx, ew = route(h, lw['w_router'], lw['router_bias'])
    y = routed_experts(h, eidx, ew, lw['w_exp_gu'], lw['w_exp_down'])
    gs, us = jnp.split(h @ lw['w_sh_gu'], 2, axis=-1)
    return y + (jax.nn.silu(gs) * us) @ lw['w_sh_down']


def decoder_layer(x, c, lw, nsa_attend, conv_buf, s0):
    B, L, D = x.shape
    mod = jnp.einsum('bd,de->be', jax.nn.silu(c), lw['w_ada']) + lw['b_ada']
    sh1, sc1, gt1, sh2, sc2, gt2 = [m[:, None, :] for m in jnp.split(mod, 6, axis=-1)]
    h = rms_norm(x, lw['ln_mix_pre']) * (1 + sc1) + sh1
    qkv, z, b_raw, a_raw, nsa_q, nsa_kv, nsa_g = jnp.split(h @ lw['w_in'], IN_OFFSETS, axis=-1)
    o_gdn, new_conv, new_s = gdn_mixer(qkv, z, b_raw, a_raw, conv_buf, s0, lw['conv_w'],
                                       lw['gdn_a_log'], lw['gdn_dt_bias'], lw['gdn_norm'])
    q = nsa_q.reshape(B, L, NSA_KV_HEADS, NSA_GROUP, HEAD_DIM).transpose(0, 2, 3, 1, 4)
    kv = nsa_kv.reshape(B, L, N_BRANCH, 2, NSA_KV_HEADS, HEAD_DIM)
    o_cmp, o_sel, o_win, nsa_state = nsa_attend(q, kv[:, :, 0], kv[:, :, 1], kv[:, :, 2])
    o_nsa = nsa_combine(o_cmp, o_sel, o_win, nsa_g)
    mix = jnp.concatenate([o_gdn, o_nsa.astype(x.dtype)], axis=-1) @ lw['w_out']
    x = x + gt1 * rms_norm(mix, lw['ln_mix_post'])
    h2 = rms_norm(x, lw['ln_ffn_pre']) * (1 + sc2) + sh2
    f = moe(h2.reshape(B * L, D), lw).reshape(B, L, D)
    x = x + gt2 * rms_norm(f, lw['ln_ffn_post'])
    return x, (nsa_state[0], nsa_state[1], nsa_state[2], new_s, new_conv)


def stack_layers(states, i):
    return jnp.stack([s[i] for s in states])


def setup_inputs(seed: int = 0) -> dict:
    key = jax.random.key(seed)
    ks = jax.random.split(key, 40)

    def nrm(i, shape, scale):
        return jax.random.normal(ks[i], shape, jnp.float32) * scale

    n_pages = PAST_LEN // PAGE_SIZE
    n_used = DEC_BATCH * n_pages
    n_pool = n_used + n_used // 4
    win_buf = min(WINDOW, PAST_LEN)
    page_table = jax.random.permutation(ks[0], n_pool)[:n_used].reshape(DEC_BATCH, n_pages).astype(jnp.int32)
    dt = jnp.exp(jax.random.uniform(ks[1], (DEPTH, GDN_HEADS), jnp.float32, math.log(1e-3), math.log(1e-1)))
    kv_page_shape = (DEPTH, n_pool, PAGE_SIZE, 2, NSA_KV_HEADS, HEAD_DIM)
    return {
        'x_prompt': nrm(2, (BATCH, SEQ, D_MODEL), 1.0),
        'x_sample': nrm(3, (DEC_BATCH, DEC_SEQ, D_MODEL), 1.0),
        'cache_cmp_kv': nrm(4, kv_page_shape, 1.0),
        'cache_sel_kv': nrm(5, kv_page_shape, 1.0),
        'cache_win_kv': nrm(6, (DEPTH, DEC_BATCH, win_buf, 2, NSA_KV_HEADS, HEAD_DIM), 1.0),
        'state_gdn': nrm(7, (DEPTH, DEC_BATCH, GDN_HEADS, GDN_DK, GDN_DV), 0.05),
        'state_conv': nrm(8, (DEPTH, DEC_BATCH, CONV_WIDTH - 1, GDN_CONV_DIM), 1.0),
        'page_table': page_table,
        'c_prompt': nrm(9, (BATCH, D_MODEL), 1.0),
        'c_sample': nrm(10, (DEC_BATCH, D_MODEL), 1.0),
        'rel_bias': nrm(11, (REL_BUCKETS, NSA_HEADS), 0.5),
        'w_ada': nrm(12, (DEPTH, D_MODEL, 6 * D_MODEL), 0.5 * D_MODEL ** -0.5),
        'b_ada': nrm(13, (DEPTH, 6 * D_MODEL), 0.02),
        'ln_mix_pre': 1.0 + nrm(14, (DEPTH, D_MODEL), 0.05),
        'ln_mix_post': 1.0 + nrm(15, (DEPTH, D_MODEL), 0.05),
        'ln_ffn_pre': 1.0 + nrm(16, (DEPTH, D_MODEL), 0.05),
        'ln_ffn_post': 1.0 + nrm(17, (DEPTH, D_MODEL), 0.05),
        'w_in': nrm(18, (DEPTH, D_MODEL, IN_DIM), D_MODEL ** -0.5),
        'w_out': nrm(19, (DEPTH, MIX_DIM, D_MODEL), MIX_DIM ** -0.5),
        'conv_w': nrm(20, (DEPTH, CONV_WIDTH, GDN_CONV_DIM), CONV_WIDTH ** -0.5),
        'gdn_a_log': jnp.log(jax.random.uniform(ks[21], (DEPTH, GDN_HEADS), jnp.float32, 1.0, 16.0)),
        'gdn_dt_bias': dt + jnp.log(-jnp.expm1(-dt)),
        'gdn_norm': 1.0 + nrm(22, (DEPTH, GDN_DV), 0.05),
        'cmp_pe': nrm(23, (DEPTH, CMP_BLOCK, 2, HEAD_DIM), 0.1),
        'cmp_w1': nrm(24, (DEPTH, 2, CMP_BLOCK * HEAD_DIM, HEAD_DIM), (CMP_BLOCK * HEAD_DIM) ** -0.5),
        'cmp_b1': nrm(25, (DEPTH, 2, HEAD_DIM), 0.02),
        'cmp_w2': nrm(26, (DEPTH, 2, HEAD_DIM, HEAD_DIM), HEAD_DIM ** -0.5),
        'w_router': nrm(27, (DEPTH, N_EXPERTS, D_MODEL), D_MODEL ** -0.5),
        'router_bias': nrm(28, (DEPTH, N_EXPERTS), 0.01),
        'w_exp_gu': nrm(29, (DEPTH, N_EXPERTS, D_MODEL, 2 * EXPERT_FF), D_MODEL ** -0.5),
        'w_exp_down': nrm(30, (DEPTH, N_EXPERTS, EXPERT_FF, D_MODEL), EXPERT_FF ** -0.5),
        'w_sh_gu': nrm(31, (DEPTH, D_MODEL, 2 * SHARED_FF), D_MODEL ** -0.5),
        'w_sh_down': nrm(32, (DEPTH, SHARED_FF, D_MODEL), SHARED_FF ** -0.5),
    }


def reference(x_prompt, x_sample, cache_cmp_kv, cache_sel_kv, cache_win_kv, state_gdn, state_conv, page_table,
              c_prompt, c_sample, rel_bias, w_ada, b_ada, ln_mix_pre, ln_mix_post, ln_ffn_pre, ln_ffn_post,
              w_in, w_out, conv_w, gdn_a_log, gdn_dt_bias, gdn_norm, cmp_pe, cmp_w1, cmp_b1, cmp_w2,
              w_router, router_bias, w_exp_gu, w_exp_down, w_sh_gu, w_sh_down):
    rel_g = rel_bias.astype(jnp.float32).reshape(REL_BUCKETS, NSA_KV_HEADS, NSA_GROUP)
    B = x_prompt.shape[0]
    y_p, y_s = x_prompt, x_sample
    st_p, st_s = [], []
    for l in range(DEPTH):
        lw = {'w_ada': w_ada[l], 'b_ada': b_ada[l], 'ln_mix_pre': ln_mix_pre[l], 'ln_mix_post': ln_mix_post[l],
              'ln_ffn_pre': ln_ffn_pre[l], 'ln_ffn_post': ln_ffn_post[l], 'w_in': w_in[l], 'w_out': w_out[l],
              'conv_w': conv_w[l], 'gdn_a_log': gdn_a_log[l], 'gdn_dt_bias': gdn_dt_bias[l], 'gdn_norm': gdn_norm[l],
              'w_router': w_router[l], 'router_bias': router_bias[l], 'w_exp_gu': w_exp_gu[l],
              'w_exp_down': w_exp_down[l], 'w_sh_gu': w_sh_gu[l], 'w_sh_down': w_sh_down[l]}
        cmp_params = (cmp_pe[l], cmp_w1[l], cmp_b1[l], cmp_w2[l])
        conv0 = jnp.zeros((B, CONV_WIDTH - 1, GDN_CONV_DIM), state_conv.dtype)
        s00 = jnp.zeros((B, GDN_HEADS, GDN_DK, GDN_DV), state_gdn.dtype)
        y_p, sp = decoder_layer(y_p, c_prompt, lw, make_nsa_prompt(rel_g, cmp_params), conv0, s00)
        nsa_s = make_nsa_sample(rel_g, cmp_params, cache_cmp_kv[l], cache_sel_kv[l], cache_win_kv[l], page_table)
        y_s, ss = decoder_layer(y_s, c_sample, lw, nsa_s, state_conv[l], state_gdn[l])
        st_p.append(sp)
        st_s.append(ss)
    cmp_kv_prompt = stack_layers(st_p, 0)
    cmp_kv_sample = stack_layers(st_s, 0)
    sel_kv_prompt = stack_layers(st_p, 1)
    sel_kv_sample = stack_layers(st_s, 1)
    win_kv_prompt = stack_layers(st_p, 2)
    win_kv_sample = stack_layers(st_s, 2)
    gdn_state_prompt = stack_layers(st_p, 3)
    gdn_state_sample = stack_layers(st_s, 3)
    conv_state_prompt = stack_layers(st_p, 4)
    conv_state_sample = stack_layers(st_s, 4)
    return (y_p, y_s, cmp_kv_prompt, cmp_kv_sample, sel_kv_prompt, sel_kv_sample, win_kv_prompt, win_kv_sample,
            gdn_state_prompt, gdn_state_sample, conv_state_prompt, conv_state_sample)
```

```python
import functools
import math

import jax
import jax.numpy as jnp
from jax import lax
from jax.experimental import pallas as pl
from jax.experimental.pallas import tpu as pltpu

F32, BF16, I32 = jnp.float32, jnp.bfloat16, jnp.int32

D_MODEL = 2048
HEAD_DIM = 128
LANES = 128
SUBLANES = 8
ROW_SLABS = D_MODEL // LANES
GDN_HEADS = 8
GDN_QK = GDN_HEADS * HEAD_DIM
GDN_CONV_DIM = 3 * GDN_QK
CONV_WIDTH = 4
GDN_CHUNK = 64
NSA_HEADS = 8
NSA_KV_HEADS = 2
NSA_GROUP = NSA_HEADS // NSA_KV_HEADS
N_BRANCH = 3
CMP_BLOCK = 64
SEL_TOPK = 16
WINDOW = 512
REL_BUCKETS = 32
REL_MAX_DIST = 8192
MOE_TOPK = 8
N_GROUPS = 8
TOPK_GROUPS = 4
ROUTED_SCALE = 2.5
NORM_EPS = 1e-6
NEG_INF = -1e30
ATTN_SCALE = HEAD_DIM ** -0.5
KV_W = 2 * NSA_KV_HEADS * HEAD_DIM

P_QKV = 0
P_Z = P_QKV + GDN_CONV_DIM
P_Q = P_Z + GDN_QK
P_KV = P_Q + NSA_HEADS * HEAD_DIM
P_SMALL = P_KV + N_BRANCH * KV_W
P_DIM = 7168
SM_BETA, SM_DECAY, SM_GATE = 0, GDN_HEADS, 2 * GDN_HEADS
IN_SIZES = (GDN_CONV_DIM, GDN_QK, GDN_HEADS, GDN_HEADS, NSA_HEADS * HEAD_DIM, N_BRANCH * KV_W, N_BRANCH * NSA_HEADS)

VMEM_LIMIT = 56 * 1024 * 1024
ROW_CHUNK = 32
MOE_MB = 256


def _cparams(*sem):
    return pltpu.CompilerParams(dimension_semantics=sem, vmem_limit_bytes=VMEM_LIMIT)


def _silu(x):
    return x * jax.nn.sigmoid(x)


def _row_chunks(n_rows, body):
    def step(i, carry):
        body(pl.multiple_of(i * ROW_CHUNK, ROW_CHUNK))
        return carry
    lax.fori_loop(0, n_rows // ROW_CHUNK, step, 0)


def _rms(x):
    return x * lax.rsqrt(jnp.mean(x * x, axis=-1, keepdims=True) + NORM_EPS)


def _mod_rows(ref, r0, per_row):
    return ref[pl.ds(r0, ROW_CHUNK), :] if per_row else ref[...]


def _mod_operand(mod, tm, rows_per_batch):
    if rows_per_batch % tm == 0:
        per = rows_per_batch // tm
        return (mod[:, None, :], pl.BlockSpec((None, 1, D_MODEL), lambda i, *_: (i // per, 0, 0)), False)
    return (jnp.repeat(mod, rows_per_batch, axis=0), pl.BlockSpec((tm, D_MODEL), lambda i, *_: (i, 0)), True)


def _ada_kernel(c_ref, w_ref, b_ref, o_ref):
    a = _silu(c_ref[...]).astype(BF16)
    o_ref[...] = jnp.dot(a, w_ref[...].astype(BF16), preferred_element_type=F32) + b_ref[...]


def _ada(c, w_ada, b_ada):
    n = c.shape[0]
    npad = -(-n // SUBLANES) * SUBLANES
    cp = jnp.pad(c, ((0, npad - n), (0, 0)))
    tn = 512
    out = pl.pallas_call(
        _ada_kernel,
        grid=(w_ada.shape[1] // tn,),
        in_specs=[pl.BlockSpec((npad, D_MODEL), lambda j: (0, 0)),
                  pl.BlockSpec((D_MODEL, tn), lambda j: (0, j)),
                  pl.BlockSpec((1, tn), lambda j: (0, j))],
        out_specs=pl.BlockSpec((npad, tn), lambda j: (0, j)),
        out_shape=jax.ShapeDtypeStruct((npad, w_ada.shape[1]), F32),
        compiler_params=_cparams("parallel"),
        name="ada",
    )(cp, w_ada, b_ada[None, :])
    return out[:n]


def _inproj_kernel(x_ref, g_ref, sc_ref, sh_ref, w_ref, o_ref, h_scr, *, tm, per_row):
    @pl.when(pl.program_id(1) == 0)
    def _():
        def body(r0):
            y = _rms(x_ref[pl.ds(r0, ROW_CHUNK), :]) * g_ref[...]
            h = y * (1.0 + _mod_rows(sc_ref, r0, per_row)) + _mod_rows(sh_ref, r0, per_row)
            h_scr[pl.ds(r0, ROW_CHUNK), :] = h.astype(BF16)
        _row_chunks(tm, body)
    o_ref[...] = jnp.dot(h_scr[...], w_ref[...], preferred_element_type=F32)


def _inproj(x, ln_g, sc, sh, w_in_p, rows_per_batch, tm):
    rows = x.shape[0]
    tn = 1024
    sc_a, sc_spec, per_row = _mod_operand(sc, tm, rows_per_batch)
    sh_a, sh_spec, _ = _mod_operand(sh, tm, rows_per_batch)
    return pl.pallas_call(
        functools.partial(_inproj_kernel, tm=tm, per_row=per_row),
        grid=(rows // tm, P_DIM // tn),
        in_specs=[pl.BlockSpec((tm, D_MODEL), lambda i, j: (i, 0)),
                  pl.BlockSpec((1, D_MODEL), lambda i, j: (0, 0)),
                  sc_spec, sh_spec,
                  pl.BlockSpec((D_MODEL, tn), lambda i, j: (0, j))],
        out_specs=pl.BlockSpec((tm, tn), lambda i, j: (i, j)),
        out_shape=jax.ShapeDtypeStruct((rows, P_DIM), F32),
        scratch_shapes=[pltpu.VMEM((tm, D_MODEL), BF16)],
        compiler_params=_cparams("parallel", "arbitrary"),
        name="inproj",
    )(x, ln_g[None, :], sc_a, sh_a, w_in_p)


def _pack_w_in(w_in):
    parts = jnp.split(w_in, list(np_cumsum(IN_SIZES)[:-1]), axis=1)
    qkv, z, b_raw, a_raw, nsa_q, nsa_kv, nsa_g = parts
    small = jnp.concatenate([b_raw, a_raw, nsa_g], axis=1)
    w = jnp.concatenate([qkv, z, nsa_q, nsa_kv, small], axis=1)
    return jnp.pad(w, ((0, 0), (0, P_DIM - w.shape[1]))).astype(BF16)


def np_cumsum(sizes):
    out, acc = [], 0
    for s in sizes:
        acc += s
        out.append(acc)
    return out


def _outproj_kernel(og_ref, oc_ref, os_ref, ow_ref, sm_ref, x_ref, w_ref, g_ref, gt_ref, o_ref,
                    mix_in, mix_out, *, tm, per_row):
    def build(r0):
        rows = pl.ds(r0, ROW_CHUNK)
        mix_in[rows, :GDN_QK] = og_ref[rows, :].astype(BF16)
        gates = jax.nn.sigmoid(sm_ref[rows, :])
        for hd in range(NSA_HEADS):
            cols = slice(hd * HEAD_DIM, (hd + 1) * HEAD_DIM)
            acc = None
            for br, ref in enumerate((oc_ref, os_ref, ow_ref)):
                c = SM_GATE + br * NSA_HEADS + hd
                term = gates[:, c:c + 1] * ref[rows, cols]
                acc = term if acc is None else acc + term
            mix_in[rows, GDN_QK + hd * HEAD_DIM:GDN_QK + (hd + 1) * HEAD_DIM] = acc.astype(BF16)
    _row_chunks(tm, build)
    mix_out[...] = jnp.dot(mix_in[...], w_ref[...], preferred_element_type=F32)

    def finish(r0):
        rows = pl.ds(r0, ROW_CHUNK)
        y = _rms(mix_out[rows, :]) * g_ref[...]
        o_ref[rows, :] = x_ref[rows, :] + _mod_rows(gt_ref, r0, per_row) * y
    _row_chunks(tm, finish)


def _outproj(o_gdn, o_cmp, o_sel, o_win, proj, x, w_out_b, ln_g, gt, rows_per_batch, tm):
    rows = x.shape[0]
    gt_a, gt_spec, per_row = _mod_operand(gt, tm, rows_per_batch)
    head_spec = pl.BlockSpec((tm, GDN_QK), lambda i: (i, 0))
    row_spec = pl.BlockSpec((tm, D_MODEL), lambda i: (i, 0))
    return pl.pallas_call(
        functools.partial(_outproj_kernel, tm=tm, per_row=per_row),
        grid=(rows // tm,),
        in_specs=[head_spec, head_spec, head_spec, head_spec,
                  pl.BlockSpec((tm, LANES), lambda i: (i, P_SMALL // LANES)),
                  row_spec,
                  pl.BlockSpec((D_MODEL, D_MODEL), lambda i: (0, 0)),
                  pl.BlockSpec((1, D_MODEL), lambda i: (0, 0)),
                  gt_spec],
        out_specs=row_spec,
        out_shape=jax.ShapeDtypeStruct((rows, D_MODEL), F32),
        scratch_shapes=[pltpu.VMEM((tm, D_MODEL), BF16), pltpu.VMEM((tm, D_MODEL), F32)],
        compiler_params=_cparams("parallel"),
        name="outproj",
    )(o_gdn, o_cmp, o_sel, o_win, proj, x, w_out_b, ln_g[None, :], gt_a)


def _route_kernel(x_ref, g_ref, sc_ref, sh_ref, wr_ref, rb_ref, h_ref, ei_ref, ew_ref, h_scr,
                  *, tm, per_row, n_exp):
    def body(r0):
        rows = pl.ds(r0, ROW_CHUNK)
        y = _rms(x_ref[rows, :]) * g_ref[...]
        h = y * (1.0 + _mod_rows(sc_ref, r0, per_row)) + _mod_rows(sh_ref, r0, per_row)
        h_scr[rows, :] = h
        for s in range(ROW_SLABS):
            h_ref[pl.ds(r0 * ROW_SLABS + s, ROW_CHUNK, stride=ROW_SLABS), :] = h[:, s * LANES:(s + 1) * LANES]
    _row_chunks(tm, body)

    logits = lax.dot_general(wr_ref[...], h_scr[...], (((1,), (1,)), ((), ())),
                             precision=lax.Precision.HIGHEST, preferred_element_type=F32)
    s = jax.nn.sigmoid(logits)
    sb = s + rb_ref[...]
    gsz = n_exp // N_GROUPS
    sb3 = sb.reshape(N_GROUPS, gsz, tm)
    m1 = jnp.max(sb3, axis=1)
    n_top = jnp.sum((sb3 == m1[:, None, :]).astype(F32), axis=1)
    m2 = jnp.max(jnp.where(sb3 < m1[:, None, :], sb3, -jnp.inf), axis=1)
    gscore = m1 + jnp.where(n_top >= 2.0, m1, m2)
    gid = lax.broadcasted_iota(I32, (N_GROUPS, tm), 0)
    rank = jnp.zeros((N_GROUPS, tm), F32)
    for g in range(N_GROUPS):
        row = gscore[g:g + 1, :]
        ahead = (row > gscore) | ((row == gscore) & (g < gid))
        rank = rank + ahead.astype(F32)
    gsel = rank < float(TOPK_GROUPS)
    emask = jnp.broadcast_to(gsel[:, None, :], (N_GROUPS, gsz, tm)).reshape(n_exp, tm)
    v = jnp.where(emask, sb, NEG_INF)
    eid = lax.broadcasted_iota(I32, (n_exp, tm), 0)
    idxs, wts = [], []
    for _ in range(MOE_TOPK):
        m = jnp.max(v, axis=0, keepdims=True)
        idx = jnp.min(jnp.where(v == m, eid, n_exp), axis=0, keepdims=True)
        hit = eid == idx
        wts.append(jnp.sum(jnp.where(hit, s, 0.0), axis=0, keepdims=True))
        idxs.append(idx)
        v = jnp.where(hit, -jnp.inf, v)
    w = jnp.concatenate(wts, axis=0)
    ei_ref[...] = jnp.concatenate(idxs, axis=0)
    ew_ref[...] = w / jnp.sum(w, axis=0, keepdims=True) * ROUTED_SCALE


def _route(x1, ln_g, sc, sh, w_router, router_bias, rows_per_batch, tm):
    rows = x1.shape[0]
    n_exp = w_router.shape[0]
    sc_a, sc_spec, per_row = _mod_operand(sc, tm, rows_per_batch)
    sh_a, sh_spec, _ = _mod_operand(sh, tm, rows_per_batch)
    return pl.pallas_call(
        functools.partial(_route_kernel, tm=tm, per_row=per_row, n_exp=n_exp),
        grid=(rows // tm,),
        in_specs=[pl.BlockSpec((tm, D_MODEL), lambda i: (i, 0)),
                  pl.BlockSpec((1, D_MODEL), lambda i: (0, 0)),
                  sc_spec, sh_spec,
                  pl.BlockSpec((n_exp, D_MODEL), lambda i: (0, 0)),
                  pl.BlockSpec((n_exp, 1), lambda i: (0, 0))],
        out_specs=[pl.BlockSpec((tm * ROW_SLABS, LANES), lambda i: (i, 0)),
                   pl.BlockSpec((MOE_TOPK, tm), lambda i: (0, i)),
                   pl.BlockSpec((MOE_TOPK, tm), lambda i: (0, i))],
        out_shape=[jax.ShapeDtypeStruct((rows * ROW_SLABS, LANES), F32),
                   jax.ShapeDtypeStruct((MOE_TOPK, rows), I32),
                   jax.ShapeDtypeStruct((MOE_TOPK, rows), F32)],
        scratch_shapes=[pltpu.VMEM((tm, D_MODEL), F32)],
        compiler_params=_cparams("parallel"),
        name="route",
    )(x1, ln_g[None, :], sc_a, sh_a, w_router, router_bias[:, None])


def _expert_kernel(blk_e_ref, nused_ref, src_ref, nxt_ref, dst_ref, rw_ref, wgu_ref, wd_ref, h_hbm,
                   y_hbm, xbuf, xmat, ybuf, gsem, ssem, *, mb, ff, dump0):
    i = pl.program_id(0)
    nused = nused_ref[0]
    slot = lax.rem(i, 2)
    slab_rows = mb * ROW_SLABS

    def gather(idx_ref, to_slot, r):
        tok = idx_ref[0, 0, r]
        return pltpu.make_async_copy(
            h_hbm.at[pl.ds(pl.multiple_of(tok * ROW_SLABS, ROW_SLABS), ROW_SLABS), :],
            xbuf.at[pl.ds(pl.multiple_of(to_slot * slab_rows + r * ROW_SLABS, ROW_SLABS), ROW_SLABS), :],
            gsem.at[to_slot])

    def scatter(r):
        row = dst_ref[0, 0, r]
        return pltpu.make_async_copy(
            ybuf.at[pl.ds(r * ROW_SLABS, ROW_SLABS), :],
            y_hbm.at[pl.ds(pl.multiple_of(row * ROW_SLABS, ROW_SLABS), ROW_SLABS), :],
            ssem.at[0])

    @pl.when(i == 0)
    def _():
        for r in range(mb):
            gather(src_ref, 0, r).start()
        ybuf[...] = jnp.zeros_like(ybuf)
        fill = pltpu.make_async_copy(ybuf, y_hbm.at[pl.ds(dump0 * ROW_SLABS, slab_rows), :], ssem.at[0])
        fill.start()
        fill.wait()

    @pl.when(i < nused)
    def _():
        @pl.when(i + 1 < nused)
        def _():
            for r in range(mb):
                gather(nxt_ref, 1 - slot, r).start()

        for r in range(mb):
            gather(src_ref, slot, r).wait()
        base = pl.multiple_of(slot * slab_rows, slab_rows)
        for s in range(ROW_SLABS):
            xmat[:, s * LANES:(s + 1) * LANES] = xbuf[pl.ds(base + s, mb, stride=ROW_SLABS), :].astype(BF16)
        gu = jnp.dot(xmat[...], wgu_ref[...].astype(BF16), preferred_element_type=F32)
        hid = (_silu(gu[:, :ff]) * gu[:, ff:]).astype(BF16)
        y = jnp.dot(hid, wd_ref[...].astype(BF16), preferred_element_type=F32) * rw_ref[...]

        @pl.when(i > 0)
        def _():
            for r in range(mb):
                scatter(r).wait()
        for s in range(ROW_SLABS):
            ybuf[pl.ds(s, mb, stride=ROW_SLABS), :] = y[:, s * LANES:(s + 1) * LANES]
        for r in range(mb):
            scatter(r).start()

        @pl.when(i == nused - 1)
        def _():
            for r in range(mb):
                scatter(r).wait()


def _dispatch_plan(eidx, ew, n_exp, mb):
    t_all = eidx.shape[0]
    n_asg = t_all * MOE_TOPK
    n_blk = -(-(n_asg + n_exp * (mb - 1)) // mb)
    n_slot = n_blk * mb
    plane_rows = t_all
    flat_e = eidx.reshape(-1)
    order = jnp.argsort(flat_e, stable=True).astype(I32)
    se = flat_e[order]
    counts = jnp.bincount(flat_e, length=n_exp).astype(I32)
    pcounts = (counts + mb - 1) // mb * mb
    pends = jnp.cumsum(pcounts)
    starts = jnp.cumsum(counts) - counts
    dest = (pends - pcounts)[se] + jnp.arange(n_asg, dtype=I32) - starts[se]
    slot_id = jnp.arange(n_slot, dtype=I32)
    dump = MOE_TOPK * plane_rows + slot_id % mb
    real = (order % MOE_TOPK) * plane_rows + order // MOE_TOPK
    rows_src = jnp.zeros((n_slot,), I32).at[dest].set(order // MOE_TOPK)
    rows_dst = dump.at[dest].set(real)
    rows_w = jnp.zeros((n_slot,), F32).at[dest].set(ew.reshape(-1)[order])
    blk_e = jnp.minimum(jnp.searchsorted(pends, jnp.arange(n_blk, dtype=I32) * mb, side='right'),
                        n_exp - 1).astype(I32)
    nused = (pends[-1] // mb).astype(I32).reshape(1)
    return n_blk, plane_rows, rows_src, rows_dst, rows_w, blk_e, nused


def _experts(h_slabs, eidx, ew, w_gu, w_down):
    n_exp, _, ff2 = w_gu.shape
    ff = ff2 // 2
    mb = MOE_MB
    n_blk, plane_rows, rows_src, rows_dst, rows_w, blk_e, nused = _dispatch_plan(eidx, ew, n_exp, mb)
    idx_spec = lambda f: pl.BlockSpec((1, 1, mb), f, memory_space=pltpu.SMEM)
    y = pl.pallas_call(
        functools.partial(_expert_kernel, mb=mb, ff=ff, dump0=MOE_TOPK * plane_rows),
        grid_spec=pltpu.PrefetchScalarGridSpec(
            num_scalar_prefetch=2,
            grid=(n_blk,),
            in_specs=[idx_spec(lambda i, be, nu: (i, 0, 0)),
                      idx_spec(lambda i, be, nu: (jnp.minimum(i + 1, n_blk - 1), 0, 0)),
                      idx_spec(lambda i, be, nu: (i, 0, 0)),
                      pl.BlockSpec((mb, 1), lambda i, be, nu: (i, 0)),
                      pl.BlockSpec((None, D_MODEL, ff2), lambda i, be, nu: (be[i], 0, 0)),
                      pl.BlockSpec((None, ff, D_MODEL), lambda i, be, nu: (be[i], 0, 0)),
                      pl.BlockSpec(memory_space=pl.ANY)],
            out_specs=pl.BlockSpec(memory_space=pl.ANY),
            scratch_shapes=[pltpu.VMEM((2 * mb * ROW_SLABS, LANES), F32),
                            pltpu.VMEM((mb, D_MODEL), BF16),
                            pltpu.VMEM((mb * ROW_SLABS, LANES), F32),
                            pltpu.SemaphoreType.DMA((2,)),
                            pltpu.SemaphoreType.DMA((1,))]),
        out_shape=jax.ShapeDtypeStruct(((MOE_TOPK * plane_rows + mb) * ROW_SLABS, LANES), F32),
        compiler_params=_cparams("arbitrary"),
        name="experts",
    )(blk_e, nused, rows_src.reshape(n_blk, 1, mb), rows_src.reshape(n_blk, 1, mb),
      rows_dst.reshape(n_blk, 1, mb), rows_w[:, None], w_gu, w_down, h_slabs)
    return y


def _combine_kernel(*refs, tm, per_row, ff):
    y_refs = refs[:MOE_TOPK]
    h_ref, wgu_ref, wd_ref, x_ref, g_ref, gt_ref, o_ref, fsum, hmat, f_scr = refs[MOE_TOPK:]

    def add_planes(r0):
        rows = pl.ds(r0 * ROW_SLABS, ROW_CHUNK * ROW_SLABS)
        acc = y_refs[0][rows, :]
        for y_ref in y_refs[1:]:
            acc = acc + y_ref[rows, :]
        fsum[rows, :] = acc
    _row_chunks(tm, add_planes)
    for s in range(ROW_SLABS):
        hmat[:, s * LANES:(s + 1) * LANES] = h_ref[pl.ds(s, tm, stride=ROW_SLABS), :].astype(BF16)
    gu = jnp.dot(hmat[...], wgu_ref[...], preferred_element_type=F32)
    hid = (_silu(gu[:, :ff]) * gu[:, ff:]).astype(BF16)
    f_scr[...] = jnp.dot(hid, wd_ref[...], preferred_element_type=F32)
    for s in range(ROW_SLABS):
        f_scr[:, s * LANES:(s + 1) * LANES] += fsum[pl.ds(s, tm, stride=ROW_SLABS), :]

    def finish(r0):
        rows = pl.ds(r0, ROW_CHUNK)
        y = _rms(f_scr[rows, :]) * g_ref[...]
        o_ref[rows, :] = x_ref[rows, :] + _mod_rows(gt_ref, r0, per_row) * y
    _row_chunks(tm, finish)


def _combine(y_planes, h_slabs, row0, w_sh_gu_b, w_sh_down_b, x1, ln_g, gt, rows_per_batch, tm):
    rows = x1.shape[0]
    ff = w_sh_down_b.shape[0]
    t0 = row0 // tm
    plane_tiles = h_slabs.shape[0] // (tm * ROW_SLABS)
    gt_a, gt_spec, per_row = _mod_operand(gt, tm, rows_per_batch)
    row_spec = pl.BlockSpec((tm, D_MODEL), lambda i: (i, 0))
    plane_specs = [pl.BlockSpec((tm * ROW_SLABS, LANES), lambda i, k=k: (k * plane_tiles + t0 + i, 0))
                   for k in range(MOE_TOPK)]
    return pl.pallas_call(
        functools.partial(_combine_kernel, tm=tm, per_row=per_row, ff=ff),
        grid=(rows // tm,),
        in_specs=plane_specs + [
                  pl.BlockSpec((tm * ROW_SLABS, LANES), lambda i: (t0 + i, 0)),
                  pl.BlockSpec((D_MODEL, 2 * ff), lambda i: (0, 0)),
                  pl.BlockSpec((ff, D_MODEL), lambda i: (0, 0)),
                  row_spec,
                  pl.BlockSpec((1, D_MODEL), lambda i: (0, 0)),
                  gt_spec],
        out_specs=row_spec,
        out_shape=jax.ShapeDtypeStruct((rows, D_MODEL), F32),
        scratch_shapes=[pltpu.VMEM((tm * ROW_SLABS, LANES), F32),
                        pltpu.VMEM((tm, D_MODEL), BF16),
                        pltpu.VMEM((tm, D_MODEL), F32)],
        compiler_params=_cparams("parallel"),
        name="combine",
    )(*([y_planes] * MOE_TOPK), h_slabs, w_sh_gu_b, w_sh_down_b, x1, ln_g[None, :], gt_a)


def _moe(x1_p, x1_s, mod_p, mod_s, rpb_p, rpb_s, ln_pre, ln_post, w_router, router_bias, w_gu, w_down,
         w_sh_gu_b, w_sh_down_b):
    tm_s = x1_s.shape[0]
    h_p, ei_p, ew_p = _route(x1_p, ln_pre, mod_p[0], mod_p[1], w_router, router_bias, rpb_p, 256)
    h_s, ei_s, ew_s = _route(x1_s, ln_pre, mod_s[0], mod_s[1], w_router, router_bias, rpb_s, tm_s)
    h_all = jnp.concatenate([h_p, h_s], axis=0)
    eidx = jnp.concatenate([ei_p, ei_s], axis=1).T
    ew = jnp.concatenate([ew_p, ew_s], axis=1).T
    y4 = _experts(h_all, eidx, ew, w_gu, w_down)
    out_p = _combine(y4, h_all, 0, w_sh_gu_b, w_sh_down_b, x1_p, ln_post, mod_p[2], rpb_p, 128)
    out_s = _combine(y4, h_all, x1_p.shape[0], w_sh_gu_b, w_sh_down_b, x1_s, ln_post, mod_s[2], rpb_s, tm_s)
    return out_p, out_s


SEL_BLOCK = CMP_BLOCK
WIN_QBLOCK = 128
SEL_QBLOCK = 32
PAGE_SIZE = 128


def l2_normalize(x):
    return x * lax.rsqrt(jnp.sum(x * x, axis=-1, keepdims=True) + NORM_EPS)


def t5_bucket(dist):
    n = jnp.maximum(dist, 0)
    max_exact = REL_BUCKETS // 2
    nf = jnp.maximum(n, 1).astype(jnp.float32)
    large = max_exact + (jnp.log(nf / max_exact) / math.log(REL_MAX_DIST / max_exact)
                         * (REL_BUCKETS - max_exact)).astype(jnp.int32)
    return jnp.where(n < max_exact, n, jnp.minimum(large, REL_BUCKETS - 1))


def masked_probs(s, mask):
    s = jnp.where(mask, s.astype(jnp.float32), NEG_INF)
    return jax.nn.softmax(s, axis=-1) * mask


def short_conv(x, buf, w):
    L = x.shape[1]
    xp = jnp.concatenate([buf.astype(x.dtype), x], axis=1)
    y = sum(xp[:, j:j + L] * w[j] for j in range(CONV_WIDTH))
    return jax.nn.silu(y), xp[:, L:]


def gated_delta_chunked(q, k, v, g, beta, s0):
    B, H, L, dk = q.shape
    dv = v.shape[-1]
    C = math.gcd(L, GDN_CHUNK)
    n = L // C

    def chunks(t):
        return t.reshape(B, H, n, C, *t.shape[3:])

    q, k, v, g, beta = (chunks(t) for t in (q, k, v, g, beta))
    gc = jnp.cumsum(g, axis=-1)
    lower = jnp.tril(jnp.ones((C, C), bool))
    strict = jnp.tril(jnp.ones((C, C), bool), -1)
    decay = jnp.exp(jnp.where(lower, gc[..., :, None] - gc[..., None, :], NEG_INF))
    kb = k * beta[..., None]
    lmat = jnp.where(strict, jnp.einsum('bhncd,bhnjd->bhncj', kb, k) * decay, 0.0)
    rhs = jnp.concatenate([v * beta[..., None], kb * jnp.exp(gc)[..., None]], axis=-1)
    sol = lax.linalg.triangular_solve(lmat + jnp.eye(C, dtype=lmat.dtype), rhs,
                                      left_side=True, lower=True, unit_diagonal=True)
    u, w = sol[..., :dv], sol[..., dv:]
    qk = jnp.where(lower, jnp.einsum('bhncd,bhnjd->bhncj', q, k) * decay, 0.0)
    qg = q * jnp.exp(gc)[..., None]
    kg = k * jnp.exp(gc[..., -1:] - gc)[..., None]
    g_last = jnp.exp(gc[..., -1])

    def step(S, xs):
        u_i, w_i, qk_i, qg_i, kg_i, gl_i = xs
        v_new = u_i - jnp.einsum('bhcd,bhde->bhce', w_i, S)
        o = jnp.einsum('bhcd,bhde->bhce', qg_i, S) + jnp.einsum('bhcj,bhje->bhce', qk_i, v_new)
        S = S * gl_i[..., None, None] + jnp.einsum('bhcd,bhce->bhde', kg_i, v_new)
        return S, o

    xs = tuple(jnp.moveaxis(t, 2, 0) for t in (u, w, qk, qg, kg, g_last))
    S, o = lax.scan(step, s0, xs)
    return jnp.moveaxis(o, 0, 2).reshape(B, H, L, dv), S


def gdn_mixer(qkv, z, b_raw, a_raw, conv_buf, s0, conv_w, a_log, dt_bias, norm_w):
    B, L, _ = qkv.shape
    qkv_c, new_buf = short_conv(qkv, conv_buf, conv_w)
    qc, kc, vc = jnp.split(qkv_c, [GDN_QK, 2 * GDN_QK], axis=-1)

    def heads(t, d):
        return t.reshape(B, L, GDN_HEADS, d).transpose(0, 2, 1, 3).astype(jnp.float32)

    q = l2_normalize(heads(qc, HEAD_DIM)) * (HEAD_DIM ** -0.5)
    k = l2_normalize(heads(kc, HEAD_DIM))
    v = heads(vc, HEAD_DIM)
    beta = jax.nn.sigmoid(b_raw.astype(jnp.float32)).transpose(0, 2, 1)
    g = (-jnp.exp(a_log.astype(jnp.float32))
         * jax.nn.softplus(a_raw.astype(jnp.float32) + dt_bias.astype(jnp.float32))).transpose(0, 2, 1)
    o, s_new = gated_delta_chunked(q, k, v, g, beta, s0.astype(jnp.float32))
    o = o.transpose(0, 2, 1, 3)
    o = (o * lax.rsqrt(jnp.mean(o * o, axis=-1, keepdims=True) + NORM_EPS) * norm_w.astype(jnp.float32)
         * jax.nn.silu(z.reshape(B, L, GDN_HEADS, HEAD_DIM).astype(jnp.float32)))
    return o.reshape(B, L, GDN_HEADS * HEAD_DIM).astype(qkv.dtype), new_buf, s_new.astype(s0.dtype)


def compress_blocks(kv, pe, w1, b1, w2):
    B, Lk = kv.shape[:2]
    nb = Lk // CMP_BLOCK
    blk = kv[:, :nb * CMP_BLOCK].reshape(B, nb, CMP_BLOCK, 2, NSA_KV_HEADS, HEAD_DIM)
    blk = blk + pe[:, :, None, :]
    flat = blk.transpose(0, 1, 3, 4, 2, 5).reshape(B, nb, 2, NSA_KV_HEADS, CMP_BLOCK * HEAD_DIM)
    hid = jax.nn.silu(jnp.einsum('bnshf,sfe->bnshe', flat, w1) + b1[:, None, :])
    return jnp.einsum('bnshe,sed->bnshd', hid, w2)


def cmp_attend(q, qpos, kvc, rel_g):
    nb = kvc.shape[1]
    bend = jnp.arange(nb, dtype=jnp.int32) * CMP_BLOCK + (CMP_BLOCK - 1)
    dist = qpos[:, None] - bend[None, :]
    bias = rel_g[t5_bucket(dist)].transpose(2, 3, 0, 1)
    s = jnp.einsum('bhgqd,bnhd->bhgqn', q, kvc[:, :, 0]).astype(jnp.float32) * ATTN_SCALE + bias
    p = masked_probs(s, dist >= 0)
    o = jnp.einsum('bhgqn,bnhd->bhgqd', p, kvc[:, :, 1].astype(jnp.float32))
    return o, p


def select_blocks(p, qpos):
    score = jnp.sum(p, axis=2)
    B, Hkv, Q, nb = score.shape
    cur = qpos // SEL_BLOCK
    score = jnp.where(jnp.arange(nb)[None, :] < cur[:, None], score, -1.0)
    width = max(nb, SEL_TOPK - 1)
    score = jnp.pad(score, ((0, 0), (0, 0), (0, 0), (0, width - nb)), constant_values=-1.0)
    top_s, top_i = lax.top_k(score, SEL_TOPK - 1)
    cur_b = jnp.broadcast_to(cur[None, None, :, None], (B, Hkv, Q, 1)).astype(jnp.int32)
    idx = jnp.concatenate([cur_b, top_i.astype(jnp.int32)], axis=-1)
    valid = jnp.concatenate([jnp.ones((B, Hkv, Q, 1), bool), top_s >= 0], axis=-1)
    return idx, valid


def sel_attend(q, qpos, idx, valid, fetch, rel_g):
    B, Hkv, G, Q, dh = q.shape
    qc = math.gcd(Q, SEL_QBLOCK)
    nc = Q // qc
    qs = q.reshape(B, Hkv, G, nc, qc, dh).transpose(3, 0, 1, 2, 4, 5)
    ids = idx.reshape(B, Hkv, nc, qc, SEL_TOPK).transpose(2, 0, 1, 3, 4)
    vals = valid.reshape(B, Hkv, nc, qc, SEL_TOPK).transpose(2, 0, 1, 3, 4)
    ps = qpos.reshape(nc, qc)
    hidx = jnp.arange(Hkv)[None, :, None, None]
    offs = jnp.arange(SEL_BLOCK, dtype=jnp.int32)
    nkeys = SEL_TOPK * SEL_BLOCK

    def one(args):
        qb, ib, vb, pb = args
        kv = fetch(ib)
        kk = kv[..., 0, :].reshape(B, Hkv, qc, nkeys, dh)
        vv = kv[..., 1, :].reshape(B, Hkv, qc, nkeys, dh)
        kpos = (ib[..., None] * SEL_BLOCK + offs).reshape(B, Hkv, qc, nkeys)
        dist = pb[None, None, :, None] - kpos
        mask = jnp.broadcast_to(vb[..., None], (B, Hkv, qc, SEL_TOPK, SEL_BLOCK)).reshape(B, Hkv, qc, nkeys) & (dist >= 0)
        bias = rel_g[t5_bucket(dist), hidx].transpose(0, 1, 4, 2, 3)
        s = jnp.einsum('bhgqd,bhqkd->bhgqk', qb, kk).astype(jnp.float32) * ATTN_SCALE + bias
        p = masked_probs(s, mask[:, :, None])
        return jnp.einsum('bhgqk,bhqkd->bhgqd', p, vv.astype(jnp.float32))

    o = lax.map(one, (qs, ids, vals, ps))
    return o.transpose(1, 2, 3, 0, 4, 5).reshape(B, Hkv, G, Q, dh)


def win_attend(q, qpos, kv, kpos, rel_g):
    dist = qpos[:, None] - kpos[None, :]
    mask = (dist >= 0) & (dist < WINDOW) & (kpos[None, :] >= 0)
    bias = rel_g[t5_bucket(dist)].transpose(2, 3, 0, 1)
    s = jnp.einsum('bhgqd,bkhd->bhgqk', q, kv[:, :, 0]).astype(jnp.float32) * ATTN_SCALE + bias
    p = masked_probs(s, mask)
    return jnp.einsum('bhgqk,bkhd->bhgqd', p, kv[:, :, 1].astype(jnp.float32))


def win_attend_prompt(q, kv, rel_g):
    B, Hkv, G, L, dh = q.shape
    wq = math.gcd(L, WIN_QBLOCK)
    nq = L // wq
    kvp = jnp.pad(kv, ((0, 0), (WINDOW, 0), (0, 0), (0, 0), (0, 0)))
    qs = q.reshape(B, Hkv, G, nq, wq, dh).transpose(3, 0, 1, 2, 4, 5)

    def one(args):
        qb, i = args
        start = i * wq
        band = lax.dynamic_slice_in_dim(kvp, start, WINDOW + wq, axis=1)
        qpos = start + jnp.arange(wq, dtype=jnp.int32)
        kpos = start - WINDOW + jnp.arange(WINDOW + wq, dtype=jnp.int32)
        return win_attend(qb, qpos, band, kpos, rel_g)

    o = lax.map(one, (qs, jnp.arange(nq, dtype=jnp.int32)))
    return o.transpose(1, 2, 3, 0, 4, 5).reshape(B, Hkv, G, L, dh)


def make_nsa_prompt(rel_g, cmp_params):
    def attend(q, kv_cmp, kv_sel, kv_win):
        B, L = kv_cmp.shape[:2]
        qpos = jnp.arange(L, dtype=jnp.int32)
        o_cmp, p = cmp_attend(q, qpos, compress_blocks(kv_cmp, *cmp_params), rel_g)
        idx, valid = select_blocks(p, qpos)
        nblk = -(-L // SEL_BLOCK)
        store = jnp.pad(kv_sel, ((0, 0), (0, nblk * SEL_BLOCK - L), (0, 0), (0, 0), (0, 0)))
        bidx = jnp.arange(B)[:, None, None, None, None]
        hidx = jnp.arange(NSA_KV_HEADS)[None, :, None, None, None]
        offs = jnp.arange(SEL_BLOCK, dtype=jnp.int32)

        def fetch(ib):
            rows = jnp.clip(ib, 0, nblk - 1)[..., None] * SEL_BLOCK + offs
            return store[bidx, rows, :, hidx]

        o_sel = sel_attend(q, qpos, idx, valid, fetch, rel_g)
        o_win = win_attend_prompt(q, kv_win, rel_g)
        return o_cmp, o_sel, o_win, (kv_cmp, kv_sel, kv_win[:, L - min(WINDOW, L):])
    return attend


def make_nsa_sample(rel_g, cmp_params, pool_cmp, pool_sel, win_buf, page_table):
    def attend(q, kv_cmp, kv_sel, kv_win):
        Bd, L = kv_cmp.shape[:2]
        n_pages = page_table.shape[1]
        past = n_pages * PAGE_SIZE
        qpos = past + jnp.arange(L, dtype=jnp.int32)
        past_cmp = pool_cmp[page_table].reshape(Bd, past, 2, NSA_KV_HEADS, HEAD_DIM).astype(kv_cmp.dtype)
        kvc = jnp.concatenate([compress_blocks(past_cmp, *cmp_params),
                               compress_blocks(kv_cmp, *cmp_params)], axis=1)
        o_cmp, p = cmp_attend(q, qpos, kvc, rel_g)
        idx, valid = select_blocks(p, qpos)
        bpp = PAGE_SIZE // SEL_BLOCK
        n_past_blk = n_pages * bpp
        n_new_blk = -(-L // SEL_BLOCK)
        new_rows = jnp.pad(kv_sel, ((0, 0), (0, n_new_blk * SEL_BLOCK - L), (0, 0), (0, 0), (0, 0)))
        bidx = jnp.arange(Bd)[:, None, None, None, None]
        hidx = jnp.arange(NSA_KV_HEADS)[None, :, None, None, None]
        offs = jnp.arange(SEL_BLOCK, dtype=jnp.int32)

        def fetch(ib):
            ip = jnp.clip(ib, 0, n_past_blk - 1)
            phys = page_table[bidx[..., 0], ip // bpp][..., None]
            from_past = pool_sel[phys, (ip % bpp)[..., None] * SEL_BLOCK + offs, :, hidx]
            rows_new = jnp.clip(ib - n_past_blk, 0, n_new_blk - 1)[..., None] * SEL_BLOCK + offs
            from_new = new_rows[bidx, rows_new, :, hidx]
            return jnp.where((ib >= n_past_blk)[..., None, None, None], from_new, from_past.astype(from_new.dtype))

        o_sel = sel_attend(q, qpos, idx, valid, fetch, rel_g)
        wb = win_buf.shape[1]
        kw = jnp.concatenate([win_buf.astype(kv_win.dtype), kv_win], axis=1)
        kpos = past - wb + jnp.arange(wb + L, dtype=jnp.int32)
        o_win = win_attend(q, qpos, kw, kpos, rel_g)
        return o_cmp, o_sel, o_win, (kv_cmp, kv_sel, kw[:, L:])
    return attend


def _heads_to_rows(o):
    b, hkv, g, l, dh = o.shape
    return o.transpose(0, 3, 1, 2, 4).reshape(b * l, hkv * g * dh)


def _jax_mixers(proj, b, l, nsa_attend, conv_buf, s0, conv_w, a_log, dt_bias, norm_w):
    p3 = proj.reshape(b, l, P_DIM)
    qkv = p3[..., P_QKV:P_Z]
    z = p3[..., P_Z:P_Q]
    nsa_q = p3[..., P_Q:P_KV]
    nsa_kv = p3[..., P_KV:P_SMALL]
    small = p3[..., P_SMALL:P_SMALL + LANES]
    b_raw = small[..., SM_BETA:SM_BETA + GDN_HEADS]
    a_raw = small[..., SM_DECAY:SM_DECAY + GDN_HEADS]
    o_gdn, new_conv, new_s = gdn_mixer(qkv, z, b_raw, a_raw, conv_buf, s0, conv_w, a_log, dt_bias, norm_w)
    q = nsa_q.reshape(b, l, NSA_KV_HEADS, NSA_GROUP, HEAD_DIM).transpose(0, 2, 3, 1, 4)
    kv = nsa_kv.reshape(b, l, N_BRANCH, 2, NSA_KV_HEADS, HEAD_DIM)
    o_cmp, o_sel, o_win, nsa_state = nsa_attend(q, kv[:, :, 0], kv[:, :, 1], kv[:, :, 2])
    return (o_gdn.reshape(b * l, GDN_QK), _heads_to_rows(o_cmp), _heads_to_rows(o_sel), _heads_to_rows(o_win),
            nsa_state, new_s, new_conv)


def kernel(x_prompt, x_sample, cache_cmp_kv, cache_sel_kv, cache_win_kv, state_gdn, state_conv, page_table,
           c_prompt, c_sample, rel_bias, w_ada, b_ada, ln_mix_pre, ln_mix_post, ln_ffn_pre, ln_ffn_post,
           w_in, w_out, conv_w, gdn_a_log, gdn_dt_bias, gdn_norm, cmp_pe, cmp_w1, cmp_b1, cmp_w2,
           w_router, router_bias, w_exp_gu, w_exp_down, w_sh_gu, w_sh_down):
    bp, lp, _ = x_prompt.shape
    bs, ls, _ = x_sample.shape
    xp = x_prompt.reshape(bp * lp, D_MODEL)
    xs = x_sample.reshape(bs * ls, D_MODEL)
    mod = _ada(jnp.concatenate([c_prompt, c_sample], axis=0), w_ada[0], b_ada[0])
    sh1, sc1, gt1, sh2, sc2, gt2 = jnp.split(mod, 6, axis=1)
    w_in_p = _pack_w_in(w_in[0])
    proj_p = _inproj(xp, ln_mix_pre[0], sc1[:bp], sh1[:bp], w_in_p, lp, 512)
    proj_s = _inproj(xs, ln_mix_pre[0], sc1[bp:], sh1[bp:], w_in_p, ls, bs * ls)
    rel_g = rel_bias.astype(F32).reshape(REL_BUCKETS, NSA_KV_HEADS, NSA_GROUP)
    cmp_params = (cmp_pe[0], cmp_w1[0], cmp_b1[0], cmp_w2[0])
    gdn_params = (conv_w[0], gdn_a_log[0], gdn_dt_bias[0], gdn_norm[0])
    conv0 = jnp.zeros((bp, CONV_WIDTH - 1, GDN_CONV_DIM), state_conv.dtype)
    s00 = jnp.zeros((bp, GDN_HEADS, HEAD_DIM, HEAD_DIM), state_gdn.dtype)
    mix_p = _jax_mixers(proj_p, bp, lp, make_nsa_prompt(rel_g, cmp_params), conv0, s00, *gdn_params)
    nsa_s = make_nsa_sample(rel_g, cmp_params, cache_cmp_kv[0], cache_sel_kv[0], cache_win_kv[0], page_table)
    mix_s = _jax_mixers(proj_s, bs, ls, nsa_s, state_conv[0], state_gdn[0], *gdn_params)
    w_out_b = w_out[0].astype(BF16)
    x1_p = _outproj(*mix_p[:4], proj_p, xp, w_out_b, ln_mix_post[0], gt1[:bp], lp, 256)
    x1_s = _outproj(*mix_s[:4], proj_s, xs, w_out_b, ln_mix_post[0], gt1[bp:], ls, bs * ls)
    y_p, y_s = _moe(x1_p, x1_s, (sc2[:bp], sh2[:bp], gt2[:bp]), (sc2[bp:], sh2[bp:], gt2[bp:]), lp, ls,
                    ln_ffn_pre[0], ln_ffn_post[0], w_router[0], router_bias[0], w_exp_gu[0], w_exp_down[0],
                    w_sh_gu[0].astype(BF16), w_sh_down[0].astype(BF16))
    st_p = (*mix_p[4], mix_p[5], mix_p[6])
    st_s = (*mix_s[4], mix_s[5], mix_s[6])
    outs = [y_p.reshape(x_prompt.shape), y_s.reshape(x_sample.shape)]
    for a, b in zip(st_p, st_s):
        outs += [a[None], b[None]]
    return tuple(outs)
```

```python
import functools
import math

import jax
import jax.numpy as jnp
from jax import lax
from jax.experimental import pallas as pl
from jax.experimental.pallas import tpu as pltpu

F32, BF16, I32 = jnp.float32, jnp.bfloat16, jnp.int32

D_MODEL = 2048
HEAD_DIM = 128
LANES = 128
SUBLANES = 8
ROW_SLABS = D_MODEL // LANES
GDN_HEADS = 8
GDN_QK = GDN_HEADS * HEAD_DIM
GDN_CONV_DIM = 3 * GDN_QK
CONV_WIDTH = 4
GDN_CHUNK = 64
NSA_HEADS = 8
NSA_KV_HEADS = 2
NSA_GROUP = NSA_HEADS // NSA_KV_HEADS
N_BRANCH = 3
CMP_BLOCK = 64
CMP_SHIFT = CMP_BLOCK.bit_length() - 1
SEL_TOPK = 16
WINDOW = 512
REL_BUCKETS = 32
REL_MAX_DIST = 8192
MOE_TOPK = 8
N_GROUPS = 8
TOPK_GROUPS = 4
ROUTED_SCALE = 2.5
NORM_EPS = 1e-6
NEG_INF = -1e30
ATTN_SCALE = HEAD_DIM ** -0.5
KV_W = 2 * NSA_KV_HEADS * HEAD_DIM

P_QKV = 0
P_Z = P_QKV + GDN_CONV_DIM
P_Q = P_Z + GDN_QK
P_KV = P_Q + NSA_HEADS * HEAD_DIM
P_SMALL = P_KV + N_BRANCH * KV_W
P_DIM = 7168
SM_BETA, SM_DECAY, SM_GATE = 0, GDN_HEADS, 2 * GDN_HEADS
IN_SIZES = (GDN_CONV_DIM, GDN_QK, GDN_HEADS, GDN_HEADS, NSA_HEADS * HEAD_DIM, N_BRANCH * KV_W, N_BRANCH * NSA_HEADS)

VMEM_LIMIT = 56 * 1024 * 1024
ROW_CHUNK = 32
MOE_MB = 256


def _cparams(*sem):
    return pltpu.CompilerParams(dimension_semantics=sem, vmem_limit_bytes=VMEM_LIMIT)


def _silu(x):
    return x * jax.nn.sigmoid(x)


def _row_chunks(n_rows, body):
    def step(i, carry):
        body(pl.multiple_of(i * ROW_CHUNK, ROW_CHUNK))
        return carry
    lax.fori_loop(0, n_rows // ROW_CHUNK, step, 0)


def _rms(x):
    return x * lax.rsqrt(jnp.mean(x * x, axis=-1, keepdims=True) + NORM_EPS)


def _mod_rows(ref, r0, per_row):
    return ref[pl.ds(r0, ROW_CHUNK), :] if per_row else ref[...]


def _mod_operand(mod, tm, rows_per_batch):
    if rows_per_batch % tm == 0:
        per = rows_per_batch // tm
        return (mod[:, None, :], pl.BlockSpec((None, 1, D_MODEL), lambda i, *_: (i // per, 0, 0)), False)
    return (jnp.repeat(mod, rows_per_batch, axis=0), pl.BlockSpec((tm, D_MODEL), lambda i, *_: (i, 0)), True)


def _ada_kernel(c_ref, w_ref, b_ref, o_ref):
    a = _silu(c_ref[...]).astype(BF16)
    o_ref[...] = jnp.dot(a, w_ref[...].astype(BF16), preferred_element_type=F32) + b_ref[...]


def _ada(c, w_ada, b_ada):
    n = c.shape[0]
    npad = -(-n // SUBLANES) * SUBLANES
    cp = jnp.pad(c, ((0, npad - n), (0, 0)))
    tn = 512
    out = pl.pallas_call(
        _ada_kernel,
        grid=(w_ada.shape[1] // tn,),
        in_specs=[pl.BlockSpec((npad, D_MODEL), lambda j: (0, 0)),
                  pl.BlockSpec((D_MODEL, tn), lambda j: (0, j)),
                  pl.BlockSpec((1, tn), lambda j: (0, j))],
        out_specs=pl.BlockSpec((npad, tn), lambda j: (0, j)),
        out_shape=jax.ShapeDtypeStruct((npad, w_ada.shape[1]), F32),
        compiler_params=_cparams("parallel"),
        name="ada",
    )(cp, w_ada, b_ada[None, :])
    return out[:n]


def _inproj_kernel(x_ref, g_ref, sc_ref, sh_ref, w_ref, o_ref, h_scr, *, tm, per_row):
    @pl.when(pl.program_id(1) == 0)
    def _():
        def body(r0):
            y = _rms(x_ref[pl.ds(r0, ROW_CHUNK), :]) * g_ref[...]
            h = y * (1.0 + _mod_rows(sc_ref, r0, per_row)) + _mod_rows(sh_ref, r0, per_row)
            h_scr[pl.ds(r0, ROW_CHUNK), :] = h.astype(BF16)
        _row_chunks(tm, body)
    o_ref[...] = jnp.dot(h_scr[...], w_ref[...], preferred_element_type=F32)


def _inproj(x, ln_g, sc, sh, w_in_p, rows_per_batch, tm):
    rows = x.shape[0]
    tn = 1024
    sc_a, sc_spec, per_row = _mod_operand(sc, tm, rows_per_batch)
    sh_a, sh_spec, _ = _mod_operand(sh, tm, rows_per_batch)
    return pl.pallas_call(
        functools.partial(_inproj_kernel, tm=tm, per_row=per_row),
        grid=(rows // tm, P_DIM // tn),
        in_specs=[pl.BlockSpec((tm, D_MODEL), lambda i, j: (i, 0)),
                  pl.BlockSpec((1, D_MODEL), lambda i, j: (0, 0)),
                  sc_spec, sh_spec,
                  pl.BlockSpec((D_MODEL, tn), lambda i, j: (0, j))],
        out_specs=pl.BlockSpec((tm, tn), lambda i, j: (i, j)),
        out_shape=jax.ShapeDtypeStruct((rows, P_DIM), F32),
        scratch_shapes=[pltpu.VMEM((tm, D_MODEL), BF16)],
        compiler_params=_cparams("parallel", "arbitrary"),
        name="inproj",
    )(x, ln_g[None, :], sc_a, sh_a, w_in_p)


def _pack_w_in(w_in):
    parts = jnp.split(w_in, list(np_cumsum(IN_SIZES)[:-1]), axis=1)
    qkv, z, b_raw, a_raw, nsa_q, nsa_kv, nsa_g = parts
    small = jnp.concatenate([b_raw, a_raw, nsa_g], axis=1)
    w = jnp.concatenate([qkv, z, nsa_q, nsa_kv, small], axis=1)
    return jnp.pad(w, ((0, 0), (0, P_DIM - w.shape[1]))).astype(BF16)


def np_cumsum(sizes):
    out, acc = [], 0
    for s in sizes:
        acc += s
        out.append(acc)
    return out


def _outproj_kernel(og_ref, oc_ref, os_ref, ow_ref, sm_ref, x_ref, w_ref, g_ref, gt_ref, o_ref,
                    mix_in, mix_out, *, tm, per_row):
    def build(r0):
        rows = pl.ds(r0, ROW_CHUNK)
        mix_in[rows, :GDN_QK] = og_ref[rows, :].astype(BF16)
        gates = jax.nn.sigmoid(sm_ref[rows, :])
        for hd in range(NSA_HEADS):
            cols = slice(hd * HEAD_DIM, (hd + 1) * HEAD_DIM)
            acc = None
            for br, ref in enumerate((oc_ref, os_ref, ow_ref)):
                c = SM_GATE + br * NSA_HEADS + hd
                term = gates[:, c:c + 1] * ref[rows, cols]
                acc = term if acc is None else acc + term
            mix_in[rows, GDN_QK + hd * HEAD_DIM:GDN_QK + (hd + 1) * HEAD_DIM] = acc.astype(BF16)
    _row_chunks(tm, build)
    mix_out[...] = jnp.dot(mix_in[...], w_ref[...], preferred_element_type=F32)

    def finish(r0):
        rows = pl.ds(r0, ROW_CHUNK)
        y = _rms(mix_out[rows, :]) * g_ref[...]
        o_ref[rows, :] = x_ref[rows, :] + _mod_rows(gt_ref, r0, per_row) * y
    _row_chunks(tm, finish)


def _outproj(o_gdn, o_cmp, o_sel, o_win, proj, x, w_out_b, ln_g, gt, rows_per_batch, tm):
    rows = x.shape[0]
    gt_a, gt_spec, per_row = _mod_operand(gt, tm, rows_per_batch)
    head_spec = pl.BlockSpec((tm, GDN_QK), lambda i: (i, 0))
    row_spec = pl.BlockSpec((tm, D_MODEL), lambda i: (i, 0))
    return pl.pallas_call(
        functools.partial(_outproj_kernel, tm=tm, per_row=per_row),
        grid=(rows // tm,),
        in_specs=[head_spec, head_spec, head_spec, head_spec,
                  pl.BlockSpec((tm, LANES), lambda i: (i, P_SMALL // LANES)),
                  row_spec,
                  pl.BlockSpec((D_MODEL, D_MODEL), lambda i: (0, 0)),
                  pl.BlockSpec((1, D_MODEL), lambda i: (0, 0)),
                  gt_spec],
        out_specs=row_spec,
        out_shape=jax.ShapeDtypeStruct((rows, D_MODEL), F32),
        scratch_shapes=[pltpu.VMEM((tm, D_MODEL), BF16), pltpu.VMEM((tm, D_MODEL), F32)],
        compiler_params=_cparams("parallel"),
        name="outproj",
    )(o_gdn, o_cmp, o_sel, o_win, proj, x, w_out_b, ln_g[None, :], gt_a)


def _route_kernel(x_ref, g_ref, sc_ref, sh_ref, wr_ref, rb_ref, h_ref, ei_ref, ew_ref, h_scr,
                  *, tm, per_row, n_exp):
    def body(r0):
        rows = pl.ds(r0, ROW_CHUNK)
        y = _rms(x_ref[rows, :]) * g_ref[...]
        h = y * (1.0 + _mod_rows(sc_ref, r0, per_row)) + _mod_rows(sh_ref, r0, per_row)
        h_scr[rows, :] = h
        for s in range(ROW_SLABS):
            h_ref[pl.ds(r0 * ROW_SLABS + s, ROW_CHUNK, stride=ROW_SLABS), :] = h[:, s * LANES:(s + 1) * LANES]
    _row_chunks(tm, body)

    logits = lax.dot_general(wr_ref[...], h_scr[...], (((1,), (1,)), ((), ())),
                             precision=lax.Precision.HIGHEST, preferred_element_type=F32)
    s = jax.nn.sigmoid(logits)
    sb = s + rb_ref[...]
    gsz = n_exp // N_GROUPS
    sb3 = sb.reshape(N_GROUPS, gsz, tm)
    m1 = jnp.max(sb3, axis=1)
    n_top = jnp.sum((sb3 == m1[:, None, :]).astype(F32), axis=1)
    m2 = jnp.max(jnp.where(sb3 < m1[:, None, :], sb3, -jnp.inf), axis=1)
    gscore = m1 + jnp.where(n_top >= 2.0, m1, m2)
    gid = lax.broadcasted_iota(I32, (N_GROUPS, tm), 0)
    rank = jnp.zeros((N_GROUPS, tm), F32)
    for g in range(N_GROUPS):
        row = gscore[g:g + 1, :]
        ahead = (row > gscore) | ((row == gscore) & (g < gid))
        rank = rank + ahead.astype(F32)
    gsel = rank < float(TOPK_GROUPS)
    emask = jnp.broadcast_to(gsel[:, None, :], (N_GROUPS, gsz, tm)).reshape(n_exp, tm)
    v = jnp.where(emask, sb, NEG_INF)
    eid = lax.broadcasted_iota(I32, (n_exp, tm), 0)
    idxs, wts = [], []
    for _ in range(MOE_TOPK):
        m = jnp.max(v, axis=0, keepdims=True)
        idx = jnp.min(jnp.where(v == m, eid, n_exp), axis=0, keepdims=True)
        hit = eid == idx
        wts.append(jnp.sum(jnp.where(hit, s, 0.0), axis=0, keepdims=True))
        idxs.append(idx)
        v = jnp.where(hit, -jnp.inf, v)
    w = jnp.concatenate(wts, axis=0)
    ei_ref[...] = jnp.concatenate(idxs, axis=0)
    ew_ref[...] = w / jnp.sum(w, axis=0, keepdims=True) * ROUTED_SCALE


def _route(x1, ln_g, sc, sh, w_router, router_bias, rows_per_batch, tm):
    rows = x1.shape[0]
    n_exp = w_router.shape[0]
    sc_a, sc_spec, per_row = _mod_operand(sc, tm, rows_per_batch)
    sh_a, sh_spec, _ = _mod_operand(sh, tm, rows_per_batch)
    return pl.pallas_call(
        functools.partial(_route_kernel, tm=tm, per_row=per_row, n_exp=n_exp),
        grid=(rows // tm,),
        in_specs=[pl.BlockSpec((tm, D_MODEL), lambda i: (i, 0)),
                  pl.BlockSpec((1, D_MODEL), lambda i: (0, 0)),
                  sc_spec, sh_spec,
                  pl.BlockSpec((n_exp, D_MODEL), lambda i: (0, 0)),
                  pl.BlockSpec((n_exp, 1), lambda i: (0, 0))],
        out_specs=[pl.BlockSpec((tm * ROW_SLABS, LANES), lambda i: (i, 0)),
                   pl.BlockSpec((MOE_TOPK, tm), lambda i: (0, i)),
                   pl.BlockSpec((MOE_TOPK, tm), lambda i: (0, i))],
        out_shape=[jax.ShapeDtypeStruct((rows * ROW_SLABS, LANES), F32),
                   jax.ShapeDtypeStruct((MOE_TOPK, rows), I32),
                   jax.ShapeDtypeStruct((MOE_TOPK, rows), F32)],
        scratch_shapes=[pltpu.VMEM((tm, D_MODEL), F32)],
        compiler_params=_cparams("parallel"),
        name="route",
    )(x1, ln_g[None, :], sc_a, sh_a, w_router, router_bias[:, None])


def _expert_kernel(blk_e_ref, nused_ref, src_ref, nxt_ref, dst_ref, rw_ref, wgu_ref, wd_ref, h_hbm,
                   y_hbm, xbuf, xmat, ybuf, gsem, ssem, *, mb, ff, dump0):
    i = pl.program_id(0)
    nused = nused_ref[0]
    slot = lax.rem(i, 2)
    slab_rows = mb * ROW_SLABS

    def gather(idx_ref, to_slot, r):
        tok = idx_ref[0, 0, r]
        return pltpu.make_async_copy(
            h_hbm.at[pl.ds(pl.multiple_of(tok * ROW_SLABS, ROW_SLABS), ROW_SLABS), :],
            xbuf.at[pl.ds(pl.multiple_of(to_slot * slab_rows + r * ROW_SLABS, ROW_SLABS), ROW_SLABS), :],
            gsem.at[to_slot])

    def scatter(r):
        row = dst_ref[0, 0, r]
        return pltpu.make_async_copy(
            ybuf.at[pl.ds(r * ROW_SLABS, ROW_SLABS), :],
            y_hbm.at[pl.ds(pl.multiple_of(row * ROW_SLABS, ROW_SLABS), ROW_SLABS), :],
            ssem.at[0])

    @pl.when(i == 0)
    def _():
        for r in range(mb):
            gather(src_ref, 0, r).start()
        ybuf[...] = jnp.zeros_like(ybuf)
        fill = pltpu.make_async_copy(ybuf, y_hbm.at[pl.ds(dump0 * ROW_SLABS, slab_rows), :], ssem.at[0])
        fill.start()
        fill.wait()

    @pl.when(i < nused)
    def _():
        @pl.when(i + 1 < nused)
        def _():
            for r in range(mb):
                gather(nxt_ref, 1 - slot, r).start()

        for r in range(mb):
            gather(src_ref, slot, r).wait()
        base = pl.multiple_of(slot * slab_rows, slab_rows)
        for s in range(ROW_SLABS):
            xmat[:, s * LANES:(s + 1) * LANES] = xbuf[pl.ds(base + s, mb, stride=ROW_SLABS), :].astype(BF16)
        gu = jnp.dot(xmat[...], wgu_ref[...].astype(BF16), preferred_element_type=F32)
        hid = (_silu(gu[:, :ff]) * gu[:, ff:]).astype(BF16)
        y = jnp.dot(hid, wd_ref[...].astype(BF16), preferred_element_type=F32) * rw_ref[...]

        @pl.when(i > 0)
        def _():
            for r in range(mb):
                scatter(r).wait()
        for s in range(ROW_SLABS):
            ybuf[pl.ds(s, mb, stride=ROW_SLABS), :] = y[:, s * LANES:(s + 1) * LANES]
        for r in range(mb):
            scatter(r).start()

        @pl.when(i == nused - 1)
        def _():
            for r in range(mb):
                scatter(r).wait()


def _dispatch_plan(eidx, ew, n_exp, mb):
    t_all = eidx.shape[0]
    n_asg = t_all * MOE_TOPK
    n_blk = -(-(n_asg + n_exp * (mb - 1)) // mb)
    n_slot = n_blk * mb
    plane_rows = t_all
    flat_e = eidx.reshape(-1)
    order = jnp.argsort(flat_e, stable=True).astype(I32)
    se = flat_e[order]
    counts = jnp.bincount(flat_e, length=n_exp).astype(I32)
    pcounts = (counts + mb - 1) // mb * mb
    pends = jnp.cumsum(pcounts)
    starts = jnp.cumsum(counts) - counts
    dest = (pends - pcounts)[se] + jnp.arange(n_asg, dtype=I32) - starts[se]
    slot_id = jnp.arange(n_slot, dtype=I32)
    dump = MOE_TOPK * plane_rows + slot_id % mb
    real = (order % MOE_TOPK) * plane_rows + order // MOE_TOPK
    rows_src = jnp.zeros((n_slot,), I32).at[dest].set(order // MOE_TOPK)
    rows_dst = dump.at[dest].set(real)
    rows_w = jnp.zeros((n_slot,), F32).at[dest].set(ew.reshape(-1)[order])
    blk_e = jnp.minimum(jnp.searchsorted(pends, jnp.arange(n_blk, dtype=I32) * mb, side='right'),
                        n_exp - 1).astype(I32)
    nused = (pends[-1] // mb).astype(I32).reshape(1)
    return n_blk, plane_rows, rows_src, rows_dst, rows_w, blk_e, nused


def _experts(h_slabs, eidx, ew, w_gu, w_down):
    n_exp, _, ff2 = w_gu.shape
    ff = ff2 // 2
    mb = MOE_MB
    n_blk, plane_rows, rows_src, rows_dst, rows_w, blk_e, nused = _dispatch_plan(eidx, ew, n_exp, mb)
    idx_spec = lambda f: pl.BlockSpec((1, 1, mb), f, memory_space=pltpu.SMEM)
    y = pl.pallas_call(
        functools.partial(_expert_kernel, mb=mb, ff=ff, dump0=MOE_TOPK * plane_rows),
        grid_spec=pltpu.PrefetchScalarGridSpec(
            num_scalar_prefetch=2,
            grid=(n_blk,),
            in_specs=[idx_spec(lambda i, be, nu: (i, 0, 0)),
                      idx_spec(lambda i, be, nu: (jnp.minimum(i + 1, n_blk - 1), 0, 0)),
                      idx_spec(lambda i, be, nu: (i, 0, 0)),
                      pl.BlockSpec((mb, 1), lambda i, be, nu: (i, 0)),
                      pl.BlockSpec((None, D_MODEL, ff2), lambda i, be, nu: (be[i], 0, 0)),
                      pl.BlockSpec((None, ff, D_MODEL), lambda i, be, nu: (be[i], 0, 0)),
                      pl.BlockSpec(memory_space=pl.ANY)],
            out_specs=pl.BlockSpec(memory_space=pl.ANY),
            scratch_shapes=[pltpu.VMEM((2 * mb * ROW_SLABS, LANES), F32),
                            pltpu.VMEM((mb, D_MODEL), BF16),
                            pltpu.VMEM((mb * ROW_SLABS, LANES), F32),
                            pltpu.SemaphoreType.DMA((2,)),
                            pltpu.SemaphoreType.DMA((1,))]),
        out_shape=jax.ShapeDtypeStruct(((MOE_TOPK * plane_rows + mb) * ROW_SLABS, LANES), F32),
        compiler_params=_cparams("arbitrary"),
        name="experts",
    )(blk_e, nused, rows_src.reshape(n_blk, 1, mb), rows_src.reshape(n_blk, 1, mb),
      rows_dst.reshape(n_blk, 1, mb), rows_w[:, None], w_gu, w_down, h_slabs)
    return y


def _combine_kernel(*refs, tm, per_row, ff):
    y_refs = refs[:MOE_TOPK]
    h_ref, wgu_ref, wd_ref, x_ref, g_ref, gt_ref, o_ref, fsum, hmat, f_scr = refs[MOE_TOPK:]

    def add_planes(r0):
        rows = pl.ds(r0 * ROW_SLABS, ROW_CHUNK * ROW_SLABS)
        acc = y_refs[0][rows, :]
        for y_ref in y_refs[1:]:
            acc = acc + y_ref[rows, :]
        fsum[rows, :] = acc
    _row_chunks(tm, add_planes)
    for s in range(ROW_SLABS):
        hmat[:, s * LANES:(s + 1) * LANES] = h_ref[pl.ds(s, tm, stride=ROW_SLABS), :].astype(BF16)
    gu = jnp.dot(hmat[...], wgu_ref[...], preferred_element_type=F32)
    hid = (_silu(gu[:, :ff]) * gu[:, ff:]).astype(BF16)
    f_scr[...] = jnp.dot(hid, wd_ref[...], preferred_element_type=F32)
    for s in range(ROW_SLABS):
        f_scr[:, s * LANES:(s + 1) * LANES] += fsum[pl.ds(s, tm, stride=ROW_SLABS), :]

    def finish(r0):
        rows = pl.ds(r0, ROW_CHUNK)
        y = _rms(f_scr[rows, :]) * g_ref[...]
        o_ref[rows, :] = x_ref[rows, :] + _mod_rows(gt_ref, r0, per_row) * y
    _row_chunks(tm, finish)


def _combine(y_planes, h_slabs, row0, w_sh_gu_b, w_sh_down_b, x1, ln_g, gt, rows_per_batch, tm):
    rows = x1.shape[0]
    ff = w_sh_down_b.shape[0]
    t0 = row0 // tm
    plane_tiles = h_slabs.shape[0] // (tm * ROW_SLABS)
    gt_a, gt_spec, per_row = _mod_operand(gt, tm, rows_per_batch)
    row_spec = pl.BlockSpec((tm, D_MODEL), lambda i: (i, 0))
    plane_specs = [pl.BlockSpec((tm * ROW_SLABS, LANES), lambda i, k=k: (k * plane_tiles + t0 + i, 0))
                   for k in range(MOE_TOPK)]
    return pl.pallas_call(
        functools.partial(_combine_kernel, tm=tm, per_row=per_row, ff=ff),
        grid=(rows // tm,),
        in_specs=plane_specs + [
                  pl.BlockSpec((tm * ROW_SLABS, LANES), lambda i: (t0 + i, 0)),
                  pl.BlockSpec((D_MODEL, 2 * ff), lambda i: (0, 0)),
                  pl.BlockSpec((ff, D_MODEL), lambda i: (0, 0)),
                  row_spec,
                  pl.BlockSpec((1, D_MODEL), lambda i: (0, 0)),
                  gt_spec],
        out_specs=row_spec,
        out_shape=jax.ShapeDtypeStruct((rows, D_MODEL), F32),
        scratch_shapes=[pltpu.VMEM((tm * ROW_SLABS, LANES), F32),
                        pltpu.VMEM((tm, D_MODEL), BF16),
                        pltpu.VMEM((tm, D_MODEL), F32)],
        compiler_params=_cparams("parallel"),
        name="combine",
    )(*([y_planes] * MOE_TOPK), h_slabs, w_sh_gu_b, w_sh_down_b, x1, ln_g[None, :], gt_a)


def _moe(x1_p, x1_s, mod_p, mod_s, rpb_p, rpb_s, ln_pre, ln_post, w_router, router_bias, w_gu, w_down,
         w_sh_gu_b, w_sh_down_b):
    tm_s = x1_s.shape[0]
    h_p, ei_p, ew_p = _route(x1_p, ln_pre, mod_p[0], mod_p[1], w_router, router_bias, rpb_p, 256)
    h_s, ei_s, ew_s = _route(x1_s, ln_pre, mod_s[0], mod_s[1], w_router, router_bias, rpb_s, tm_s)
    h_all = jnp.concatenate([h_p, h_s], axis=0)
    eidx = jnp.concatenate([ei_p, ei_s], axis=1).T
    ew = jnp.concatenate([ew_p, ew_s], axis=1).T
    y4 = _experts(h_all, eidx, ew, w_gu, w_down)
    out_p = _combine(y4, h_all, 0, w_sh_gu_b, w_sh_down_b, x1_p, ln_post, mod_p[2], rpb_p, 128)
    out_s = _combine(y4, h_all, x1_p.shape[0], w_sh_gu_b, w_sh_down_b, x1_s, ln_post, mod_s[2], rpb_s, tm_s)
    return out_p, out_s


def _t5_bucket(dist):
    n = jnp.maximum(dist, 0)
    max_exact = REL_BUCKETS // 2
    nf = jnp.maximum(n, 1).astype(F32)
    large = max_exact + (jnp.log(nf / max_exact) / math.log(REL_MAX_DIST / max_exact)
                         * (REL_BUCKETS - max_exact)).astype(I32)
    return jnp.where(n < max_exact, n, jnp.minimum(large, REL_BUCKETS - 1))


def _bucket_bias(bucket, rel_ref, head):
    out = jnp.zeros(bucket.shape, F32)
    for b in range(REL_BUCKETS):
        out = jnp.where(bucket == b, rel_ref[b, head], out)
    return out


def _bias_tile_kernel(rel_ref, o_ref, *, d_min, lo, hi):
    d = pl.program_id(0) + d_min
    row = lax.broadcasted_iota(I32, (LANES, LANES), 0)
    col = lax.broadcasted_iota(I32, (LANES, LANES), 1)
    dist = d * LANES + row - col
    bucket = _t5_bucket(dist)
    visible = (dist >= lo) & (dist < hi)
    for h in range(NSA_HEADS):
        o_ref[h, 0] = jnp.where(visible, _bucket_bias(bucket, rel_ref, h), NEG_INF)


def _bias_tiles(rel_bias, d_min, n_d, lo, hi):
    return pl.pallas_call(
        functools.partial(_bias_tile_kernel, d_min=d_min, lo=lo, hi=hi),
        grid=(n_d,),
        in_specs=[pl.BlockSpec(memory_space=pltpu.SMEM)],
        out_specs=pl.BlockSpec((NSA_HEADS, 1, LANES, LANES), lambda d: (0, d, 0, 0)),
        out_shape=jax.ShapeDtypeStruct((NSA_HEADS, n_d, LANES, LANES), F32),
        compiler_params=_cparams("parallel"),
        name="bias_tiles",
    )(rel_bias)


CMP_ROWS = 128


def _cmp_select_kernel(rel_ref, q_ref, k_ref, v_ref, o_ref, neg_ref, idx_ref, *, tq, nb, pos0):
    hkv = pl.program_id(1)
    rq = min(tq, CMP_ROWS)
    k = k_ref[...].astype(BF16)
    v = v_ref[...].astype(BF16)
    blk = lax.broadcasted_iota(I32, (rq, nb), 1)
    blkf = blk.astype(F32)
    tile0 = pl.program_id(2) * tq

    def step(c, carry):
        r0 = pl.multiple_of(c * rq, rq)
        rows = pl.ds(r0, rq)
        qpos = pos0 + tile0 + r0 + lax.broadcasted_iota(I32, (rq, nb), 0)
        dist = qpos - (blk * CMP_BLOCK + (CMP_BLOCK - 1))
        bucket = _t5_bucket(dist)
        seen = dist >= 0
        psum = jnp.zeros((rq, nb), F32)
        for g in range(NSA_GROUP):
            cols = slice(g * HEAD_DIM, (g + 1) * HEAD_DIM)
            s = lax.dot_general(q_ref[rows, cols].astype(BF16), k, (((1,), (1,)), ((), ())),
                                preferred_element_type=F32) * ATTN_SCALE
            bias = jnp.zeros((rq, nb), F32)
            for b in range(REL_BUCKETS):
                bias = jnp.where(bucket == b, rel_ref[b, hkv * NSA_GROUP + g], bias)
            s = jnp.where(seen, s + bias, NEG_INF)
            e = jnp.exp(s - jnp.max(s, axis=1, keepdims=True))
            p = e / jnp.sum(e, axis=1, keepdims=True) * seen.astype(F32)
            o_ref[rows, cols] = jnp.dot(p.astype(BF16), v, preferred_element_type=F32)
            psum = psum + p
        cur = lax.shift_right_logical(qpos, CMP_SHIFT)
        score = jnp.where(blk < cur, psum, -1.0)
        chosen = blk == cur
        lane = lax.broadcasted_iota(I32, (rq, SEL_TOPK), 1)
        picks = jnp.where(lane == 0, cur[:, :SEL_TOPK].astype(F32), -1.0)
        for r in range(1, SEL_TOPK):
            m = jnp.max(score, axis=1, keepdims=True)
            first = jnp.min(jnp.where(score == m, blkf, float(nb)), axis=1, keepdims=True)
            hit = blkf == first
            ok = m >= 0.0
            chosen = chosen | (hit & ok)
            picks = jnp.where(lane == r, jnp.where(ok, first, -1.0), picks)
            score = jnp.where(hit, -2.0, score)
        neg_ref[rows, :] = jnp.where(chosen, 0.0, NEG_INF).astype(BF16)
        idx_ref[rows, :] = picks.astype(I32)
        return carry
    lax.fori_loop(0, tq // rq, step, 0)


def _cmp_select(q3, col_blk0, kvc, rel_bias, tq, pos0):
    b, lq, _ = q3.shape
    nb = kvc.shape[1]
    gw = NSA_GROUP * HEAD_DIM
    return pl.pallas_call(
        functools.partial(_cmp_select_kernel, tq=tq, nb=nb, pos0=pos0),
        grid=(b, NSA_KV_HEADS, lq // tq),
        in_specs=[pl.BlockSpec(memory_space=pltpu.SMEM),
                  pl.BlockSpec((None, tq, gw), lambda i, h, t: (i, t, col_blk0 + h)),
                  pl.BlockSpec((None, nb, HEAD_DIM), lambda i, h, t: (i, 0, h)),
                  pl.BlockSpec((None, nb, HEAD_DIM), lambda i, h, t: (i, 0, NSA_KV_HEADS + h))],
        out_specs=[pl.BlockSpec((None, tq, gw), lambda i, h, t: (i, t, h)),
                   pl.BlockSpec((None, None, tq, nb), lambda i, h, t: (h, i, t, 0)),
                   pl.BlockSpec((None, None, tq, SEL_TOPK), lambda i, h, t: (h, i, t, 0))],
        out_shape=[jax.ShapeDtypeStruct((b, lq, NSA_HEADS * HEAD_DIM), F32),
                   jax.ShapeDtypeStruct((NSA_KV_HEADS, b, lq, nb), BF16),
                   jax.ShapeDtypeStruct((NSA_KV_HEADS, b, lq, SEL_TOPK), I32)],
        compiler_params=_cparams("parallel", "parallel", "parallel"),
        name="cmp_select",
    )(rel_bias, q3, kvc, kvc)


ATT_T = 512
ATT_SUB = 128


def _flash_kernel(*refs, selected, d_min):
    if selected:
        q_ref, neg_ref, k_ref, v_ref, t_ref, o_ref, m_scr, l_scr, acc_scr = refs
    else:
        q_ref, k_ref, v_ref, t_ref, o_ref, m_scr, l_scr, acc_scr = refs
    qi, kk = pl.program_id(2), pl.program_id(3)
    kj = kk if selected else qi - 1 + kk
    nsub = ATT_T // ATT_SUB

    @pl.when(kk == 0)
    def _():
        m_scr[...] = jnp.full(m_scr.shape, NEG_INF, F32)
        l_scr[...] = jnp.zeros(l_scr.shape, F32)
        acc_scr[...] = jnp.zeros(acc_scr.shape, F32)

    @pl.when((kj >= 0) & (kj <= qi))
    def _():
        kb = k_ref[...].astype(BF16)
        if selected:
            nb = neg_ref.shape[1]
            key_blk = lax.shift_right_logical(kj * ATT_T + lax.broadcasted_iota(I32, (ATT_T, nb), 0), CMP_SHIFT)
            onehot = jnp.where(key_blk == lax.broadcasted_iota(I32, (ATT_T, nb), 1), 1.0, 0.0)
            kb = jnp.concatenate([kb, onehot.astype(BF16)], axis=1)
        vb = v_ref[...].astype(BF16)
        d0 = (qi - kj) * nsub - d_min
        for g in range(NSA_GROUP):
            def step(a, carry, g=g):
                rows = pl.ds(pl.multiple_of(a * ATT_SUB, ATT_SUB), ATT_SUB)
                qa = q_ref[rows, g * HEAD_DIM:(g + 1) * HEAD_DIM].astype(BF16)
                if selected:
                    qa = jnp.concatenate([qa, neg_ref[rows, :]], axis=1)
                s = lax.dot_general(qa, kb, (((1,), (1,)), ((), ())), preferred_element_type=F32)
                bias = jnp.concatenate([t_ref[g, d0 + a - c] for c in range(nsub)], axis=1)
                s = s * ATTN_SCALE + bias
                m_prev = m_scr[g, rows, :]
                m_new = jnp.maximum(m_prev, jnp.max(s, axis=1, keepdims=True))
                alpha = jnp.exp(m_prev - m_new)
                p = jnp.exp(s - m_new)
                l_scr[g, rows, :] = alpha * l_scr[g, rows, :] + jnp.sum(p, axis=1, keepdims=True)
                acc_scr[g, rows, :] = alpha * acc_scr[g, rows, :] + jnp.dot(p.astype(BF16), vb,
                                                                            preferred_element_type=F32)
                m_scr[g, rows, :] = m_new
                return carry
            lax.fori_loop(0, nsub, step, 0)

    @pl.when(kk == pl.num_programs(3) - 1)
    def _():
        for g in range(NSA_GROUP):
            o_ref[:, g * HEAD_DIM:(g + 1) * HEAD_DIM] = acc_scr[g] / l_scr[g]


def _flash(proj, neg, tiles, b, l, branch, d_min):
    selected = neg is not None
    nq = l // ATT_T
    nk = nq if selected else 2
    gw = NSA_GROUP * HEAD_DIM
    n_d = tiles.shape[1]
    k_col = (P_KV + branch * KV_W) // HEAD_DIM

    def kj_of(qi, kk):
        return jnp.clip(kk if selected else qi - 1 + kk, 0, qi)

    in_specs = [pl.BlockSpec((ATT_T, gw), lambda i, h, qi, kk: (i * nq + qi, P_Q // gw + h))]
    args = [proj]
    if selected:
        in_specs.append(pl.BlockSpec((None, ATT_T, neg.shape[2]), lambda i, h, qi, kk: (h, i * nq + qi, 0)))
        args.append(neg)
    in_specs += [pl.BlockSpec((ATT_T, HEAD_DIM), lambda i, h, qi, kk: (i * nq + kj_of(qi, kk), k_col + h)),
                 pl.BlockSpec((ATT_T, HEAD_DIM),
                              lambda i, h, qi, kk: (i * nq + kj_of(qi, kk), k_col + NSA_KV_HEADS + h)),
                 pl.BlockSpec((NSA_GROUP, n_d, LANES, LANES), lambda i, h, qi, kk: (h, 0, 0, 0))]
    args += [proj, proj, tiles]
    return pl.pallas_call(
        functools.partial(_flash_kernel, selected=selected, d_min=d_min),
        grid=(b, NSA_KV_HEADS, nq, nk),
        in_specs=in_specs,
        out_specs=pl.BlockSpec((ATT_T, gw), lambda i, h, qi, kk: (i * nq + qi, h)),
        out_shape=jax.ShapeDtypeStruct((b * l, NSA_HEADS * HEAD_DIM), F32),
        scratch_shapes=[pltpu.VMEM((NSA_GROUP, ATT_T, 1), F32), pltpu.VMEM((NSA_GROUP, ATT_T, 1), F32),
                        pltpu.VMEM((NSA_GROUP, ATT_T, HEAD_DIM), F32)],
        compiler_params=_cparams("parallel", "parallel", "parallel", "arbitrary"),
        name="flash_sel" if selected else "flash_win",
    )(*args)


QPAD = SUBLANES
NEW_PAD = LANES


def _masked_attend(s, mask, parts):
    s = [jnp.where(m, x, NEG_INF) for x, m in zip(s, mask)]
    top = s[0].max(axis=1, keepdims=True)
    for x in s[1:]:
        top = jnp.maximum(top, x.max(axis=1, keepdims=True))
    e = [jnp.exp(x - top) for x in s]
    den = sum(x.sum(axis=1, keepdims=True) for x in e)
    out = None
    for x, m, v in zip(e, mask, parts):
        term = jnp.dot((x / den * m.astype(F32)).astype(BF16), v, preferred_element_type=F32)
        out = term if out is None else out + term
    return out


def _group_rows(q_ref, hkv):
    return jnp.concatenate([q_ref[:, (hkv * NSA_GROUP + g) * HEAD_DIM:(hkv * NSA_GROUP + g + 1) * HEAD_DIM]
                            for g in range(NSA_GROUP)], axis=0).astype(BF16)


def _rows_bias(bucket, rel_ref, hkv):
    return jnp.concatenate([_bucket_bias(bucket[g * QPAD:(g + 1) * QPAD], rel_ref, hkv * NSA_GROUP + g)
                            for g in range(NSA_GROUP)], axis=0)


def _sel_sample_kernel(pick_ref, page_ref, rel_ref, q_ref, new_ref, kpos_ref, own_ref, pool_hbm, o_ref,
                       kbuf, vbuf, sem, *, n_b, n_tok, n_pick, past, pages_per_seq):
    i = pl.program_id(0)
    slot = lax.rem(i, 2)
    n_keys = n_tok * n_pick * CMP_BLOCK
    half_pages = PAGE_ROWS // CMP_BLOCK

    def fetch(b, to_slot, hkv, tok, j, part):
        blk = jnp.maximum(pick_ref[((hkv * n_b + b) * QPAD + tok) * SEL_TOPK + 1 + j], 0)
        page = page_ref[b * pages_per_seq + blk // half_pages]
        row0 = pl.multiple_of((page * half_pages + lax.rem(blk, half_pages)) * CMP_BLOCK, CMP_BLOCK)
        dst = (kbuf, vbuf)[part]
        return pltpu.make_async_copy(
            pool_hbm.at[pl.ds(row0, CMP_BLOCK), pl.ds((part * NSA_KV_HEADS + hkv) * HEAD_DIM, HEAD_DIM)],
            dst.at[to_slot, hkv, pl.ds((tok * n_pick + j) * CMP_BLOCK, CMP_BLOCK), :],
            sem.at[to_slot])

    def fetch_all(b, to_slot, wait):
        for hkv in range(NSA_KV_HEADS):
            for tok in range(n_tok):
                for j in range(n_pick):
                    for part in range(2):
                        cp = fetch(b, to_slot, hkv, tok, j, part)
                        cp.wait() if wait else cp.start()

    @pl.when(i == 0)
    def _():
        fetch_all(0, 0, False)
        zeros = jnp.zeros((NEW_PAD, HEAD_DIM), F32)
        for s in range(2):
            for hkv in range(NSA_KV_HEADS):
                kbuf[s, hkv, n_keys:n_keys + NEW_PAD, :] = zeros
                vbuf[s, hkv, n_keys:n_keys + NEW_PAD, :] = zeros

    @pl.when(i + 1 < n_b)
    def _():
        fetch_all(i + 1, 1 - slot, False)

    fetch_all(i, slot, True)
    kpos = kpos_ref[...]
    n_all = n_keys + NEW_PAD
    tok_of_row = lax.rem(lax.broadcasted_iota(I32, (NSA_GROUP * QPAD, n_all), 0), QPAD)
    for hkv in range(NSA_KV_HEADS):
        kbuf[slot, hkv, n_keys:n_keys + n_tok, :] = new_ref[:, hkv * HEAD_DIM:(hkv + 1) * HEAD_DIM]
        vbuf[slot, hkv, n_keys:n_keys + n_tok, :] = new_ref[:, (NSA_KV_HEADS + hkv) * HEAD_DIM:
                                                            (NSA_KV_HEADS + hkv + 1) * HEAD_DIM]
        q = _group_rows(q_ref, hkv)
        s = lax.dot_general(q, kbuf[slot, hkv].astype(BF16), (((1,), (1,)), ((), ())),
                            preferred_element_type=F32) * ATTN_SCALE
        kp = kpos[hkv:hkv + 1, :]
        own = own_ref[hkv:hkv + 1, :]
        dist = past + tok_of_row - kp
        mask = (kp >= 0) & (dist >= 0) & ((own < 0) | (own == tok_of_row))
        s = s + _rows_bias(_t5_bucket(dist), rel_ref, hkv)
        out = _masked_attend([s], [mask], [vbuf[slot, hkv].astype(BF16)])
        for g in range(NSA_GROUP):
            c = (hkv * NSA_GROUP + g) * HEAD_DIM
            o_ref[:, c:c + HEAD_DIM] = out[g * QPAD:(g + 1) * QPAD, :]


PAGE_ROWS = 128


def _sel_sample(picks, page_table, rel_bias, q_pad, proj3, pool2d, past, n_tok):
    n_b = q_pad.shape[0]
    n_pick = SEL_TOPK - 1
    n_keys = n_tok * n_pick * CMP_BLOCK
    blk = picks[:, :, :n_tok, 1:]
    kpos = jnp.where(blk[..., None] >= 0, blk[..., None] * CMP_BLOCK + jnp.arange(CMP_BLOCK, dtype=I32), -1)
    kpos = kpos.transpose(1, 0, 2, 3, 4).reshape(n_b, NSA_KV_HEADS, n_keys)
    new_pos = jnp.where(jnp.arange(NEW_PAD) < n_tok, past + jnp.arange(NEW_PAD), -1).astype(I32)
    kpos = jnp.concatenate([kpos, jnp.broadcast_to(new_pos, (n_b, NSA_KV_HEADS, NEW_PAD))], axis=2)
    own = jnp.concatenate([jnp.repeat(jnp.arange(n_tok, dtype=I32), n_pick * CMP_BLOCK),
                           jnp.full((NEW_PAD,), -1, I32)])
    own = jnp.broadcast_to(own, (NSA_KV_HEADS, n_keys + NEW_PAD))
    sel_col = (P_KV + KV_W) // KV_W
    return pl.pallas_call(
        functools.partial(_sel_sample_kernel, n_b=n_b, n_tok=n_tok, n_pick=n_pick, past=past,
                          pages_per_seq=page_table.shape[1]),
        grid_spec=pltpu.PrefetchScalarGridSpec(
            num_scalar_prefetch=2,
            grid=(n_b,),
            in_specs=[pl.BlockSpec(memory_space=pltpu.SMEM),
                      pl.BlockSpec((None, QPAD, NSA_HEADS * HEAD_DIM), lambda i, pk, pg: (i, 0, 0)),
                      pl.BlockSpec((None, n_tok, KV_W), lambda i, pk, pg: (i, 0, sel_col)),
                      pl.BlockSpec((None, NSA_KV_HEADS, n_keys + NEW_PAD), lambda i, pk, pg: (i, 0, 0)),
                      pl.BlockSpec((NSA_KV_HEADS, n_keys + NEW_PAD), lambda i, pk, pg: (0, 0)),
                      pl.BlockSpec(memory_space=pl.ANY)],
            out_specs=pl.BlockSpec((None, QPAD, NSA_HEADS * HEAD_DIM), lambda i, pk, pg: (i, 0, 0)),
            scratch_shapes=[pltpu.VMEM((2, NSA_KV_HEADS, n_keys + NEW_PAD, HEAD_DIM), F32),
                            pltpu.VMEM((2, NSA_KV_HEADS, n_keys + NEW_PAD, HEAD_DIM), F32),
                            pltpu.SemaphoreType.DMA((2,))]),
        out_shape=jax.ShapeDtypeStruct((n_b, QPAD, NSA_HEADS * HEAD_DIM), F32),
        compiler_params=_cparams("arbitrary"),
        name="sel_sample",
    )(picks.reshape(-1), page_table.reshape(-1), rel_bias, q_pad, proj3, kpos, own, pool2d)


def _win_sample_kernel(rel_ref, q_ref, new_ref, buf_ref, o_ref, roll_ref, new_pad, *, n_tok, past):
    wb = buf_ref.shape[0]
    new_pad[...] = jnp.zeros(new_pad.shape, F32)
    new_pad[0:n_tok, :] = new_ref[...]
    rows = NSA_GROUP * QPAD
    tok_old = lax.rem(lax.broadcasted_iota(I32, (rows, wb), 0), QPAD)
    tok_new = lax.rem(lax.broadcasted_iota(I32, (rows, QPAD), 0), QPAD)
    dist_old = tok_old + wb - lax.broadcasted_iota(I32, (rows, wb), 1)
    new_col = lax.broadcasted_iota(I32, (rows, QPAD), 1)
    dist_new = tok_new - new_col
    kpos_old = past - wb + lax.broadcasted_iota(I32, (rows, wb), 1)
    mask_old = (dist_old >= 0) & (dist_old < WINDOW) & (kpos_old >= 0)
    mask_new = (dist_new >= 0) & (dist_new < WINDOW) & (new_col < n_tok)
    tb = (((1,), (1,)), ((), ()))
    for hkv in range(NSA_KV_HEADS):
        kc = slice(hkv * HEAD_DIM, (hkv + 1) * HEAD_DIM)
        vc = slice((NSA_KV_HEADS + hkv) * HEAD_DIM, (NSA_KV_HEADS + hkv + 1) * HEAD_DIM)
        q = _group_rows(q_ref, hkv)
        s_old = lax.dot_general(q, buf_ref[:, kc].astype(BF16), tb, preferred_element_type=F32) * ATTN_SCALE
        s_new = lax.dot_general(q, new_pad[:, kc].astype(BF16), tb, preferred_element_type=F32) * ATTN_SCALE
        s_old = s_old + _rows_bias(_t5_bucket(dist_old), rel_ref, hkv)
        s_new = s_new + _rows_bias(_t5_bucket(dist_new), rel_ref, hkv)
        out = _masked_attend([s_old, s_new], [mask_old, mask_new],
                             [buf_ref[:, vc].astype(BF16), new_pad[:, vc].astype(BF16)])
        for g in range(NSA_GROUP):
            c = (hkv * NSA_GROUP + g) * HEAD_DIM
            o_ref[:, c:c + HEAD_DIM] = out[g * QPAD:(g + 1) * QPAD, :]
    roll_ref[0:wb - n_tok, :] = buf_ref[n_tok:wb, :]
    roll_ref[wb - n_tok:wb, :] = new_ref[...]


def _win_sample(rel_bias, q_pad, proj3, win_buf2d, past, n_tok):
    n_b, wb, _ = win_buf2d.shape
    win_col = (P_KV + 2 * KV_W) // KV_W
    return pl.pallas_call(
        functools.partial(_win_sample_kernel, n_tok=n_tok, past=past),
        grid=(n_b,),
        in_specs=[pl.BlockSpec(memory_space=pltpu.SMEM),
                  pl.BlockSpec((None, QPAD, NSA_HEADS * HEAD_DIM), lambda i: (i, 0, 0)),
                  pl.BlockSpec((None, n_tok, KV_W), lambda i: (i, 0, win_col)),
                  pl.BlockSpec((None, wb, KV_W), lambda i: (i, 0, 0))],
        out_specs=[pl.BlockSpec((None, QPAD, NSA_HEADS * HEAD_DIM), lambda i: (i, 0, 0)),
                   pl.BlockSpec((None, wb, KV_W), lambda i: (i, 0, 0))],
        out_shape=[jax.ShapeDtypeStruct((n_b, QPAD, NSA_HEADS * HEAD_DIM), F32),
                   jax.ShapeDtypeStruct((n_b, wb, KV_W), F32)],
        scratch_shapes=[pltpu.VMEM((QPAD, KV_W), F32)],
        compiler_params=_cparams("parallel"),
        name="win_sample",
    )(rel_bias, q_pad, proj3, win_buf2d)


CMP_GROUP = 64
CMP_PITCH = CMP_BLOCK + 8
N_KV_SLABS = KV_W // HEAD_DIM


def _compress_kernel(blk_ref, src_hbm, w1_ref, pe_ref, b1_ref, w2_ref, o_ref, buf, c1_scr, sem, *, col0, n_grp):
    i = pl.program_id(0)
    slot = lax.rem(i, 2)
    g = CMP_GROUP

    def fetch(grp, to_slot, k, slab):
        row0 = pl.multiple_of(blk_ref[grp * g + k] * CMP_BLOCK, CMP_BLOCK)
        return pltpu.make_async_copy(
            src_hbm.at[pl.ds(row0, CMP_BLOCK), pl.ds(col0 + slab * HEAD_DIM, HEAD_DIM)],
            buf.at[to_slot, slab, pl.ds(k * CMP_PITCH, CMP_BLOCK), :],
            sem.at[to_slot])

    def fetch_group(grp, to_slot, wait):
        for k in range(g):
            for slab in range(N_KV_SLABS):
                cp = fetch(grp, to_slot, k, slab)
                cp.wait() if wait else cp.start()

    @pl.when(i == 0)
    def _():
        fetch_group(0, 0, False)
        for s in range(2):
            pe_rows = jnp.broadcast_to(pe_ref[s:s + 1, :], (SUBLANES, CMP_BLOCK * HEAD_DIM)).astype(BF16)
            c1_scr[s:s + 1, :] = jnp.dot(pe_rows, w1_ref[s], preferred_element_type=F32)[0:1, :] + b1_ref[s:s + 1, :]

    @pl.when(i + 1 < n_grp)
    def _():
        fetch_group(i + 1, 1 - slot, False)

    fetch_group(i, slot, True)
    for s in range(2):
        def pair(rp, acc, s=s):
            halves = []
            for h in range(NSA_KV_HEADS):
                view = buf.at[slot, s * NSA_KV_HEADS + h]
                x0 = view[pl.ds(2 * rp, g, stride=CMP_PITCH), :]
                x1 = view[pl.ds(2 * rp + 1, g, stride=CMP_PITCH), :]
                halves.append(jnp.concatenate([x0, x1], axis=1).astype(BF16))
            w = w1_ref[s, pl.ds(pl.multiple_of(rp * 2 * HEAD_DIM, 2 * HEAD_DIM), 2 * HEAD_DIM), :]
            return acc + jnp.dot(jnp.concatenate(halves, axis=0), w, preferred_element_type=F32)
        acc = lax.fori_loop(0, CMP_BLOCK // 2, pair, jnp.zeros((NSA_KV_HEADS * g, HEAD_DIM), F32))
        hid = _silu(acc + c1_scr[s:s + 1, :]).astype(BF16)
        out = jnp.dot(hid, w2_ref[s], preferred_element_type=F32)
        for h in range(NSA_KV_HEADS):
            c = (s * NSA_KV_HEADS + h) * HEAD_DIM
            o_ref[:, c:c + HEAD_DIM] = out[h * g:(h + 1) * g, :]


def _compress(src2d, col0, blk_rows, cmp_pe, cmp_w1, cmp_b1, cmp_w2):
    n_blocks = blk_rows.shape[0]
    n_grp = n_blocks // CMP_GROUP
    pe_flat = cmp_pe.transpose(1, 0, 2).reshape(2, CMP_BLOCK * HEAD_DIM)
    return pl.pallas_call(
        functools.partial(_compress_kernel, col0=col0, n_grp=n_grp),
        grid_spec=pltpu.PrefetchScalarGridSpec(
            num_scalar_prefetch=1,
            grid=(n_grp,),
            in_specs=[pl.BlockSpec(memory_space=pl.ANY),
                      pl.BlockSpec((2, CMP_BLOCK * HEAD_DIM, HEAD_DIM), lambda i, br: (0, 0, 0)),
                      pl.BlockSpec((2, CMP_BLOCK * HEAD_DIM), lambda i, br: (0, 0)),
                      pl.BlockSpec((2, HEAD_DIM), lambda i, br: (0, 0)),
                      pl.BlockSpec((2, HEAD_DIM, HEAD_DIM), lambda i, br: (0, 0, 0))],
            out_specs=pl.BlockSpec((CMP_GROUP, KV_W), lambda i, br: (i, 0)),
            scratch_shapes=[pltpu.VMEM((2, N_KV_SLABS, CMP_GROUP * CMP_PITCH, HEAD_DIM), F32),
                            pltpu.VMEM((2, HEAD_DIM), F32),
                            pltpu.SemaphoreType.DMA((2,))]),
        out_shape=jax.ShapeDtypeStruct((n_blocks, KV_W), F32),
        compiler_params=_cparams("arbitrary"),
        name="compress",
    )(blk_rows, src2d, cmp_w1.astype(BF16), pe_flat, cmp_b1, cmp_w2.astype(BF16))


CONV_PAD = SUBLANES
HI = lax.Precision.HIGHEST


def _dot_hi(a, b):
    return jnp.dot(a, b, precision=HI, preferred_element_type=F32)


def _unit_lower_inverse(lmat, c):
    eye = (lax.broadcasted_iota(I32, (c, c), 0) == lax.broadcasted_iota(I32, (c, c), 1)).astype(F32)
    x = eye - lmat
    p = lmat
    span = 2
    while span < c:
        p = _dot_hi(p, p)
        x = x + _dot_hi(x, p)
        span *= 2
    return x


def _gdn_kernel(qkv_ref, z_ref, sm_ref, cw_ref, alog_ref, dtb_ref, nw_ref, conv0_ref, s0_ref,
                o_ref, sout_ref, cout_ref, xbuf, qkvc, s_scr, *, tl, chunk, l_valid, nt):
    t = pl.program_id(1)
    n_t = nt
    tail = CONV_WIDTH - 1

    @pl.when(t == 0)
    def _():
        xbuf[CONV_PAD - tail:CONV_PAD, :] = conv0_ref[...]
        s_scr[...] = s0_ref[...]

    xbuf[CONV_PAD:CONV_PAD + tl, :] = qkv_ref[...]
    for cb in range(GDN_CONV_DIM // LANES):
        cols = slice(cb * LANES, (cb + 1) * LANES)
        y = xbuf[CONV_PAD - tail:CONV_PAD - tail + tl, cols] * cw_ref[0:1, cols]
        for j in range(1, CONV_WIDTH):
            y = y + xbuf[CONV_PAD - tail + j:CONV_PAD - tail + j + tl, cols] * cw_ref[j:j + 1, cols]
        qkvc[:, cols] = _silu(y)

    lv = l_valid - t * tl
    small = sm_ref[...]
    live = lax.broadcasted_iota(I32, (tl, LANES), 0) < lv
    beta = jnp.where(live, jax.nn.sigmoid(small), 0.0)
    g = jnp.where(live, -jnp.exp(alog_ref[...]) * jax.nn.softplus(small + dtb_ref[...]), 0.0)
    ri = lax.broadcasted_iota(I32, (tl, tl), 0)
    ci = lax.broadcasted_iota(I32, (tl, tl), 1)
    cshift = chunk.bit_length() - 1
    same = lax.shift_right_logical(ri, cshift) == lax.shift_right_logical(ci, cshift)
    gc = _dot_hi((same & (ci <= ri)).astype(F32), g)
    gl = _dot_hi(same.astype(F32), g)
    gc_t = gc.T
    low = lax.broadcasted_iota(I32, (chunk, chunk), 0) >= lax.broadcasted_iota(I32, (chunk, chunk), 1)
    strict = lax.broadcasted_iota(I32, (chunk, chunk), 0) > lax.broadcasted_iota(I32, (chunk, chunk), 1)
    tb = (((1,), (1,)), ((), ()))

    for h in range(GDN_HEADS):
        hc = slice(h * HEAD_DIM, (h + 1) * HEAD_DIM)
        dc = SM_DECAY + h
        state = s_scr[h]
        for c in range(tl // chunk):
            rows = slice(c * chunk, (c + 1) * chunk)
            qh = qkvc[rows, h * HEAD_DIM:(h + 1) * HEAD_DIM]
            kh = qkvc[rows, GDN_QK + h * HEAD_DIM:GDN_QK + (h + 1) * HEAD_DIM]
            vh = qkvc[rows, 2 * GDN_QK + h * HEAD_DIM:2 * GDN_QK + (h + 1) * HEAD_DIM]
            qn = qh * lax.rsqrt(jnp.sum(qh * qh, axis=-1, keepdims=True) + NORM_EPS) * (HEAD_DIM ** -0.5)
            kn = kh * lax.rsqrt(jnp.sum(kh * kh, axis=-1, keepdims=True) + NORM_EPS)
            bcol = beta[rows, SM_BETA + h:SM_BETA + h + 1]
            gcol = gc[rows, dc:dc + 1]
            glcol = gl[rows, dc:dc + 1]
            grow = gc_t[dc:dc + 1, rows]
            decay = jnp.exp(jnp.where(low, gcol - grow, NEG_INF))
            kb = kn * bcol
            lmat = jnp.where(strict, lax.dot_general(kb, kn, tb, preferred_element_type=F32) * decay, 0.0)
            inv = _unit_lower_inverse(lmat, chunk)
            e_g = jnp.exp(gcol)
            u = _dot_hi(inv, vh * bcol)
            w = _dot_hi(inv, kb * e_g)
            qk = jnp.where(low, lax.dot_general(qn, kn, tb, preferred_element_type=F32) * decay, 0.0)
            qg = qn * e_g
            kg = kn * jnp.exp(glcol - gcol)
            v_new = u - jnp.dot(w, state, preferred_element_type=F32)
            o = jnp.dot(qg, state, preferred_element_type=F32) + jnp.dot(qk, v_new, preferred_element_type=F32)
            state = state * jnp.exp(glcol[0:1, :]) + lax.dot_general(kg, v_new, (((0,), (0,)), ((), ())),
                                                                     preferred_element_type=F32)
            o = o * lax.rsqrt(jnp.mean(o * o, axis=-1, keepdims=True) + NORM_EPS) * nw_ref[...]
            o_ref[rows, hc] = o * _silu(z_ref[rows, hc])
        s_scr[h] = state

    @pl.when(t == n_t - 1)
    def _():
        sout_ref[...] = s_scr[...]
        last = l_valid - (nt - 1) * tl
        cout_ref[...] = xbuf[pl.ds(CONV_PAD - tail + last, tail), :]

    xbuf[CONV_PAD - tail:CONV_PAD, :] = xbuf[CONV_PAD - tail + tl:CONV_PAD + tl, :]


def _gdn(proj, b, l_pad, l_valid, conv0, s0, conv_w, a_log, dt_bias, norm_w, tl, chunk):
    nt = l_pad // tl
    lane_pad = lambda v: jnp.zeros((1, LANES), F32).at[0, SM_DECAY:SM_DECAY + GDN_HEADS].set(v)
    return pl.pallas_call(
        functools.partial(_gdn_kernel, tl=tl, chunk=chunk, l_valid=l_valid, nt=nt),
        grid=(b, nt),
        in_specs=[pl.BlockSpec((tl, GDN_CONV_DIM), lambda i, t: (i * nt + t, P_QKV // GDN_CONV_DIM)),
                  pl.BlockSpec((tl, GDN_QK), lambda i, t: (i * nt + t, P_Z // GDN_QK)),
                  pl.BlockSpec((tl, LANES), lambda i, t: (i * nt + t, P_SMALL // LANES)),
                  pl.BlockSpec((CONV_WIDTH, GDN_CONV_DIM), lambda i, t: (0, 0)),
                  pl.BlockSpec((1, LANES), lambda i, t: (0, 0)),
                  pl.BlockSpec((1, LANES), lambda i, t: (0, 0)),
                  pl.BlockSpec((1, HEAD_DIM), lambda i, t: (0, 0)),
                  pl.BlockSpec((None, CONV_WIDTH - 1, GDN_CONV_DIM), lambda i, t: (i, 0, 0)),
                  pl.BlockSpec((None, GDN_HEADS, HEAD_DIM, HEAD_DIM), lambda i, t: (i, 0, 0, 0))],
        out_specs=[pl.BlockSpec((tl, GDN_QK), lambda i, t: (i * nt + t, 0)),
                   pl.BlockSpec((None, GDN_HEADS, HEAD_DIM, HEAD_DIM), lambda i, t: (i, 0, 0, 0)),
                   pl.BlockSpec((None, CONV_WIDTH - 1, GDN_CONV_DIM), lambda i, t: (i, 0, 0))],
        out_shape=[jax.ShapeDtypeStruct((b * l_pad, GDN_QK), F32),
                   jax.ShapeDtypeStruct((b, GDN_HEADS, HEAD_DIM, HEAD_DIM), F32),
                   jax.ShapeDtypeStruct((b, CONV_WIDTH - 1, GDN_CONV_DIM), F32)],
        scratch_shapes=[pltpu.VMEM((CONV_PAD + tl, GDN_CONV_DIM), F32),
                        pltpu.VMEM((tl, GDN_CONV_DIM), F32),
                        pltpu.VMEM((GDN_HEADS, HEAD_DIM, HEAD_DIM), F32)],
        compiler_params=_cparams("parallel", "arbitrary"),
        name="gdn",
    )(proj, proj, proj, conv_w, lane_pad(a_log), lane_pad(dt_bias), norm_w[None, :], conv0, s0)


SEL_BLOCK = CMP_BLOCK
WIN_QBLOCK = 128
SEL_QBLOCK = 32
PAGE_SIZE = 128


def l2_normalize(x):
    return x * lax.rsqrt(jnp.sum(x * x, axis=-1, keepdims=True) + NORM_EPS)


def t5_bucket(dist):
    n = jnp.maximum(dist, 0)
    max_exact = REL_BUCKETS // 2
    nf = jnp.maximum(n, 1).astype(jnp.float32)
    large = max_exact + (jnp.log(nf / max_exact) / math.log(REL_MAX_DIST / max_exact)
                         * (REL_BUCKETS - max_exact)).astype(jnp.int32)
    return jnp.where(n < max_exact, n, jnp.minimum(large, REL_BUCKETS - 1))


def masked_probs(s, mask):
    s = jnp.where(mask, s.astype(jnp.float32), NEG_INF)
    return jax.nn.softmax(s, axis=-1) * mask


def short_conv(x, buf, w):
    L = x.shape[1]
    xp = jnp.concatenate([buf.astype(x.dtype), x], axis=1)
    y = sum(xp[:, j:j + L] * w[j] for j in range(CONV_WIDTH))
    return jax.nn.silu(y), xp[:, L:]


def gated_delta_chunked(q, k, v, g, beta, s0):
    B, H, L, dk = q.shape
    dv = v.shape[-1]
    C = math.gcd(L, GDN_CHUNK)
    n = L // C

    def chunks(t):
        return t.reshape(B, H, n, C, *t.shape[3:])

    q, k, v, g, beta = (chunks(t) for t in (q, k, v, g, beta))
    gc = jnp.cumsum(g, axis=-1)
    lower = jnp.tril(jnp.ones((C, C), bool))
    strict = jnp.tril(jnp.ones((C, C), bool), -1)
    decay = jnp.exp(jnp.where(lower, gc[..., :, None] - gc[..., None, :], NEG_INF))
    kb = k * beta[..., None]
    lmat = jnp.where(strict, jnp.einsum('bhncd,bhnjd->bhncj', kb, k) * decay, 0.0)
    rhs = jnp.concatenate([v * beta[..., None], kb * jnp.exp(gc)[..., None]], axis=-1)
    sol = lax.linalg.triangular_solve(lmat + jnp.eye(C, dtype=lmat.dtype), rhs,
                                      left_side=True, lower=True, unit_diagonal=True)
    u, w = sol[..., :dv], sol[..., dv:]
    qk = jnp.where(lower, jnp.einsum('bhncd,bhnjd->bhncj', q, k) * decay, 0.0)
    qg = q * jnp.exp(gc)[..., None]
    kg = k * jnp.exp(gc[..., -1:] - gc)[..., None]
    g_last = jnp.exp(gc[..., -1])

    def step(S, xs):
        u_i, w_i, qk_i, qg_i, kg_i, gl_i = xs
        v_new = u_i - jnp.einsum('bhcd,bhde->bhce', w_i, S)
        o = jnp.einsum('bhcd,bhde->bhce', qg_i, S) + jnp.einsum('bhcj,bhje->bhce', qk_i, v_new)
        S = S * gl_i[..., None, None] + jnp.einsum('bhcd,bhce->bhde', kg_i, v_new)
        return S, o

    xs = tuple(jnp.moveaxis(t, 2, 0) for t in (u, w, qk, qg, kg, g_last))
    S, o = lax.scan(step, s0, xs)
    return jnp.moveaxis(o, 0, 2).reshape(B, H, L, dv), S


def gdn_mixer(qkv, z, b_raw, a_raw, conv_buf, s0, conv_w, a_log, dt_bias, norm_w):
    B, L, _ = qkv.shape
    qkv_c, new_buf = short_conv(qkv, conv_buf, conv_w)
    qc, kc, vc = jnp.split(qkv_c, [GDN_QK, 2 * GDN_QK], axis=-1)

    def heads(t, d):
        return t.reshape(B, L, GDN_HEADS, d).transpose(0, 2, 1, 3).astype(jnp.float32)

    q = l2_normalize(heads(qc, HEAD_DIM)) * (HEAD_DIM ** -0.5)
    k = l2_normalize(heads(kc, HEAD_DIM))
    v = heads(vc, HEAD_DIM)
    beta = jax.nn.sigmoid(b_raw.astype(jnp.float32)).transpose(0, 2, 1)
    g = (-jnp.exp(a_log.astype(jnp.float32))
         * jax.nn.softplus(a_raw.astype(jnp.float32) + dt_bias.astype(jnp.float32))).transpose(0, 2, 1)
    o, s_new = gated_delta_chunked(q, k, v, g, beta, s0.astype(jnp.float32))
    o = o.transpose(0, 2, 1, 3)
    o = (o * lax.rsqrt(jnp.mean(o * o, axis=-1, keepdims=True) + NORM_EPS) * norm_w.astype(jnp.float32)
         * jax.nn.silu(z.reshape(B, L, GDN_HEADS, HEAD_DIM).astype(jnp.float32)))
    return o.reshape(B, L, GDN_HEADS * HEAD_DIM).astype(qkv.dtype), new_buf, s_new.astype(s0.dtype)


def compress_blocks(kv, pe, w1, b1, w2):
    B, Lk = kv.shape[:2]
    nb = Lk // CMP_BLOCK
    blk = kv[:, :nb * CMP_BLOCK].reshape(B, nb, CMP_BLOCK, 2, NSA_KV_HEADS, HEAD_DIM)
    blk = blk + pe[:, :, None, :]
    flat = blk.transpose(0, 1, 3, 4, 2, 5).reshape(B, nb, 2, NSA_KV_HEADS, CMP_BLOCK * HEAD_DIM)
    hid = jax.nn.silu(jnp.einsum('bnshf,sfe->bnshe', flat, w1) + b1[:, None, :])
    return jnp.einsum('bnshe,sed->bnshd', hid, w2)


def cmp_attend(q, qpos, kvc, rel_g):
    nb = kvc.shape[1]
    bend = jnp.arange(nb, dtype=jnp.int32) * CMP_BLOCK + (CMP_BLOCK - 1)
    dist = qpos[:, None] - bend[None, :]
    bias = rel_g[t5_bucket(dist)].transpose(2, 3, 0, 1)
    s = jnp.einsum('bhgqd,bnhd->bhgqn', q, kvc[:, :, 0]).astype(jnp.float32) * ATTN_SCALE + bias
    p = masked_probs(s, dist >= 0)
    o = jnp.einsum('bhgqn,bnhd->bhgqd', p, kvc[:, :, 1].astype(jnp.float32))
    return o, p


def select_blocks(p, qpos):
    score = jnp.sum(p, axis=2)
    B, Hkv, Q, nb = score.shape
    cur = qpos // SEL_BLOCK
    score = jnp.where(jnp.arange(nb)[None, :] < cur[:, None], score, -1.0)
    width = max(nb, SEL_TOPK - 1)
    score = jnp.pad(score, ((0, 0), (0, 0), (0, 0), (0, width - nb)), constant_values=-1.0)
    top_s, top_i = lax.top_k(score, SEL_TOPK - 1)
    cur_b = jnp.broadcast_to(cur[None, None, :, None], (B, Hkv, Q, 1)).astype(jnp.int32)
    idx = jnp.concatenate([cur_b, top_i.astype(jnp.int32)], axis=-1)
    valid = jnp.concatenate([jnp.ones((B, Hkv, Q, 1), bool), top_s >= 0], axis=-1)
    return idx, valid


def sel_attend(q, qpos, idx, valid, fetch, rel_g):
    B, Hkv, G, Q, dh = q.shape
    qc = math.gcd(Q, SEL_QBLOCK)
    nc = Q // qc
    qs = q.reshape(B, Hkv, G, nc, qc, dh).transpose(3, 0, 1, 2, 4, 5)
    ids = idx.reshape(B, Hkv, nc, qc, SEL_TOPK).transpose(2, 0, 1, 3, 4)
    vals = valid.reshape(B, Hkv, nc, qc, SEL_TOPK).transpose(2, 0, 1, 3, 4)
    ps = qpos.reshape(nc, qc)
    hidx = jnp.arange(Hkv)[None, :, None, None]
    offs = jnp.arange(SEL_BLOCK, dtype=jnp.int32)
    nkeys = SEL_TOPK * SEL_BLOCK

    def one(args):
        qb, ib, vb, pb = args
        kv = fetch(ib)
        kk = kv[..., 0, :].reshape(B, Hkv, qc, nkeys, dh)
        vv = kv[..., 1, :].reshape(B, Hkv, qc, nkeys, dh)
        kpos = (ib[..., None] * SEL_BLOCK + offs).reshape(B, Hkv, qc, nkeys)
        dist = pb[None, None, :, None] - kpos
        mask = jnp.broadcast_to(vb[..., None], (B, Hkv, qc, SEL_TOPK, SEL_BLOCK)).reshape(B, Hkv, qc, nkeys) & (dist >= 0)
        bias = rel_g[t5_bucket(dist), hidx].transpose(0, 1, 4, 2, 3)
        s = jnp.einsum('bhgqd,bhqkd->bhgqk', qb, kk).astype(jnp.float32) * ATTN_SCALE + bias
        p = masked_probs(s, mask[:, :, None])
        return jnp.einsum('bhgqk,bhqkd->bhgqd', p, vv.astype(jnp.float32))

    o = lax.map(one, (qs, ids, vals, ps))
    return o.transpose(1, 2, 3, 0, 4, 5).reshape(B, Hkv, G, Q, dh)


def win_attend(q, qpos, kv, kpos, rel_g):
    dist = qpos[:, None] - kpos[None, :]
    mask = (dist >= 0) & (dist < WINDOW) & (kpos[None, :] >= 0)
    bias = rel_g[t5_bucket(dist)].transpose(2, 3, 0, 1)
    s = jnp.einsum('bhgqd,bkhd->bhgqk', q, kv[:, :, 0]).astype(jnp.float32) * ATTN_SCALE + bias
    p = masked_probs(s, mask)
    return jnp.einsum('bhgqk,bkhd->bhgqd', p, kv[:, :, 1].astype(jnp.float32))


def win_attend_prompt(q, kv, rel_g):
    B, Hkv, G, L, dh = q.shape
    wq = math.gcd(L, WIN_QBLOCK)
    nq = L // wq
    kvp = jnp.pad(kv, ((0, 0), (WINDOW, 0), (0, 0), (0, 0), (0, 0)))
    qs = q.reshape(B, Hkv, G, nq, wq, dh).transpose(3, 0, 1, 2, 4, 5)

    def one(args):
        qb, i = args
        start = i * wq
        band = lax.dynamic_slice_in_dim(kvp, start, WINDOW + wq, axis=1)
        qpos = start + jnp.arange(wq, dtype=jnp.int32)
        kpos = start - WINDOW + jnp.arange(WINDOW + wq, dtype=jnp.int32)
        return win_attend(qb, qpos, band, kpos, rel_g)

    o = lax.map(one, (qs, jnp.arange(nq, dtype=jnp.int32)))
    return o.transpose(1, 2, 3, 0, 4, 5).reshape(B, Hkv, G, L, dh)


def make_nsa_prompt(rel_g, cmp_params):
    def attend(q, kv_cmp, kv_sel, kv_win):
        B, L = kv_cmp.shape[:2]
        qpos = jnp.arange(L, dtype=jnp.int32)
        o_cmp, p = cmp_attend(q, qpos, compress_blocks(kv_cmp, *cmp_params), rel_g)
        idx, valid = select_blocks(p, qpos)
        nblk = -(-L // SEL_BLOCK)
        store = jnp.pad(kv_sel, ((0, 0), (0, nblk * SEL_BLOCK - L), (0, 0), (0, 0), (0, 0)))
        bidx = jnp.arange(B)[:, None, None, None, None]
        hidx = jnp.arange(NSA_KV_HEADS)[None, :, None, None, None]
        offs = jnp.arange(SEL_BLOCK, dtype=jnp.int32)

        def fetch(ib):
            rows = jnp.clip(ib, 0, nblk - 1)[..., None] * SEL_BLOCK + offs
            return store[bidx, rows, :, hidx]

        o_sel = sel_attend(q, qpos, idx, valid, fetch, rel_g)
        o_win = win_attend_prompt(q, kv_win, rel_g)
        return o_cmp, o_sel, o_win, (kv_cmp, kv_sel, kv_win[:, L - min(WINDOW, L):])
    return attend


def make_nsa_sample(rel_g, cmp_params, pool_cmp, pool_sel, win_buf, page_table):
    def attend(q, kv_cmp, kv_sel, kv_win):
        Bd, L = kv_cmp.shape[:2]
        n_pages = page_table.shape[1]
        past = n_pages * PAGE_SIZE
        qpos = past + jnp.arange(L, dtype=jnp.int32)
        past_cmp = pool_cmp[page_table].reshape(Bd, past, 2, NSA_KV_HEADS, HEAD_DIM).astype(kv_cmp.dtype)
        kvc = jnp.concatenate([compress_blocks(past_cmp, *cmp_params),
                               compress_blocks(kv_cmp, *cmp_params)], axis=1)
        o_cmp, p = cmp_attend(q, qpos, kvc, rel_g)
        idx, valid = select_blocks(p, qpos)
        bpp = PAGE_SIZE // SEL_BLOCK
        n_past_blk = n_pages * bpp
        n_new_blk = -(-L // SEL_BLOCK)
        new_rows = jnp.pad(kv_sel, ((0, 0), (0, n_new_blk * SEL_BLOCK - L), (0, 0), (0, 0), (0, 0)))
        bidx = jnp.arange(Bd)[:, None, None, None, None]
        hidx = jnp.arange(NSA_KV_HEADS)[None, :, None, None, None]
        offs = jnp.arange(SEL_BLOCK, dtype=jnp.int32)

        def fetch(ib):
            ip = jnp.clip(ib, 0, n_past_blk - 1)
            phys = page_table[bidx[..., 0], ip // bpp][..., None]
            from_past = pool_sel[phys, (ip % bpp)[..., None] * SEL_BLOCK + offs, :, hidx]
            rows_new = jnp.clip(ib - n_past_blk, 0, n_new_blk - 1)[..., None] * SEL_BLOCK + offs
            from_new = new_rows[bidx, rows_new, :, hidx]
            return jnp.where((ib >= n_past_blk)[..., None, None, None], from_new, from_past.astype(from_new.dtype))

        o_sel = sel_attend(q, qpos, idx, valid, fetch, rel_g)
        wb = win_buf.shape[1]
        kw = jnp.concatenate([win_buf.astype(kv_win.dtype), kv_win], axis=1)
        kpos = past - wb + jnp.arange(wb + L, dtype=jnp.int32)
        o_win = win_attend(q, qpos, kw, kpos, rel_g)
        return o_cmp, o_sel, o_win, (kv_cmp, kv_sel, kw[:, L:])
    return attend


def _heads_to_rows(o):
    b, hkv, g, l, dh = o.shape
    return o.transpose(0, 3, 1, 2, 4).reshape(b * l, hkv * g * dh)


def _jax_mixers(proj, b, l, nsa_attend, conv_buf, s0, conv_w, a_log, dt_bias, norm_w):
    p3 = proj.reshape(b, l, P_DIM)
    qkv = p3[..., P_QKV:P_Z]
    z = p3[..., P_Z:P_Q]
    nsa_q = p3[..., P_Q:P_KV]
    nsa_kv = p3[..., P_KV:P_SMALL]
    small = p3[..., P_SMALL:P_SMALL + LANES]
    b_raw = small[..., SM_BETA:SM_BETA + GDN_HEADS]
    a_raw = small[..., SM_DECAY:SM_DECAY + GDN_HEADS]
    o_gdn, new_conv, new_s = gdn_mixer(qkv, z, b_raw, a_raw, conv_buf, s0, conv_w, a_log, dt_bias, norm_w)
    q = nsa_q.reshape(b, l, NSA_KV_HEADS, NSA_GROUP, HEAD_DIM).transpose(0, 2, 3, 1, 4)
    kv = nsa_kv.reshape(b, l, N_BRANCH, 2, NSA_KV_HEADS, HEAD_DIM)
    o_cmp, o_sel, o_win, nsa_state = nsa_attend(q, kv[:, :, 0], kv[:, :, 1], kv[:, :, 2])
    return (o_gdn.reshape(b * l, GDN_QK), _heads_to_rows(o_cmp), _heads_to_rows(o_sel), _heads_to_rows(o_win),
            nsa_state, new_s, new_conv)


def kernel(x_prompt, x_sample, cache_cmp_kv, cache_sel_kv, cache_win_kv, state_gdn, state_conv, page_table,
           c_prompt, c_sample, rel_bias, w_ada, b_ada, ln_mix_pre, ln_mix_post, ln_ffn_pre, ln_ffn_post,
           w_in, w_out, conv_w, gdn_a_log, gdn_dt_bias, gdn_norm, cmp_pe, cmp_w1, cmp_b1, cmp_w2,
           w_router, router_bias, w_exp_gu, w_exp_down, w_sh_gu, w_sh_down):
    bp, lp, _ = x_prompt.shape
    bs, ls, _ = x_sample.shape
    xp = x_prompt.reshape(bp * lp, D_MODEL)
    xs = x_sample.reshape(bs * ls, D_MODEL)
    mod = _ada(jnp.concatenate([c_prompt, c_sample], axis=0), w_ada[0], b_ada[0])
    sh1, sc1, gt1, sh2, sc2, gt2 = jnp.split(mod, 6, axis=1)
    w_in_p = _pack_w_in(w_in[0])
    proj_p = _inproj(xp, ln_mix_pre[0], sc1[:bp], sh1[:bp], w_in_p, lp, 512)
    proj_s = _inproj(xs, ln_mix_pre[0], sc1[bp:], sh1[bp:], w_in_p, ls, bs * ls)
    cmp_params = (cmp_pe[0], cmp_w1[0], cmp_b1[0], cmp_w2[0])
    gdn_params = (conv_w[0], gdn_a_log[0], gdn_dt_bias[0], gdn_norm[0])
    rel_bias = rel_bias.astype(F32)
    proj3_p = proj_p.reshape(bp, lp, P_DIM)
    proj3_s = proj_s.reshape(bs, ls, P_DIM)
    assert ls < CMP_BLOCK and ls <= QPAD, "the sample step adds less than one compressed block"

    conv0 = jnp.zeros((bp, CONV_WIDTH - 1, GDN_CONV_DIM), state_conv.dtype)
    s00 = jnp.zeros((bp, GDN_HEADS, HEAD_DIM, HEAD_DIM), state_gdn.dtype)
    o_gdn_p, gdn_p, conv_p = _gdn(proj_p, bp, lp, lp, conv0, s00, *gdn_params, 2 * GDN_CHUNK, GDN_CHUNK)
    nb_p = lp // CMP_BLOCK
    kvc_p = _compress(proj_p, P_KV, jnp.arange(bp * nb_p, dtype=I32), *cmp_params).reshape(bp, nb_p, KV_W)
    o_cmp_p, neg_p, _ = _cmp_select(proj3_p, P_Q // (NSA_GROUP * HEAD_DIM), kvc_p, rel_bias, ATT_T, 0)
    nsub = ATT_T // ATT_SUB
    d_min = 1 - nsub
    t_sel = _bias_tiles(rel_bias, d_min, (lp // ATT_T + 1) * nsub - 1, 0, 1 << 30)
    t_win = _bias_tiles(rel_bias, d_min, 3 * nsub - 1, 0, WINDOW)
    o_sel_p = _flash(proj_p, neg_p.reshape(NSA_KV_HEADS, bp * lp, nb_p), t_sel, bp, lp, 1, d_min)
    o_win_p = _flash(proj_p, None, t_win, bp, lp, 2, d_min)

    n_pages = page_table.shape[1]
    past = n_pages * PAGE_ROWS
    pool_rows = cache_cmp_kv.shape[1] * PAGE_ROWS
    halves = PAGE_ROWS // CMP_BLOCK
    blk_s = (page_table[..., None] * halves + jnp.arange(halves, dtype=I32)).reshape(-1)
    kvc_s = _compress(cache_cmp_kv[0].reshape(pool_rows, KV_W), 0, blk_s, *cmp_params)
    kvc_s = kvc_s.reshape(bs, n_pages * halves, KV_W)
    q_pad = jnp.pad(proj3_s[..., P_Q:P_KV], ((0, 0), (0, QPAD - ls), (0, 0)))
    o_cmp_s, _, picks = _cmp_select(q_pad, 0, kvc_s, rel_bias, QPAD, past)
    o_sel_s = _sel_sample(picks, page_table, rel_bias, q_pad, proj3_s, cache_sel_kv[0].reshape(pool_rows, KV_W),
                          past, ls)
    wb = cache_win_kv.shape[2]
    o_win_s, win_roll = _win_sample(rel_bias, q_pad, proj3_s, cache_win_kv[0].reshape(bs, wb, KV_W), past, ls)
    proj_s_pad = jnp.pad(proj3_s, ((0, 0), (0, QPAD - ls), (0, 0))).reshape(bs * QPAD, P_DIM)
    o_gdn_s, gdn_s, conv_s = _gdn(proj_s_pad, bs, QPAD, ls, state_conv[0], state_gdn[0], *gdn_params, QPAD, QPAD)
    cut = lambda o: o.reshape(bs, QPAD, -1)[:, :ls].reshape(bs * ls, -1)

    w_out_b = w_out[0].astype(BF16)
    x1_p = _outproj(o_gdn_p, o_cmp_p.reshape(bp * lp, -1), o_sel_p, o_win_p, proj_p, xp, w_out_b, ln_mix_post[0],
                    gt1[:bp], lp, 256)
    x1_s = _outproj(cut(o_gdn_s), cut(o_cmp_s), cut(o_sel_s), cut(o_win_s), proj_s, xs, w_out_b, ln_mix_post[0],
                    gt1[bp:], ls, bs * ls)
    y_p, y_s = _moe(x1_p, x1_s, (sc2[:bp], sh2[:bp], gt2[:bp]), (sc2[bp:], sh2[bp:], gt2[bp:]), lp, ls,
                    ln_ffn_pre[0], ln_ffn_post[0], w_router[0], router_bias[0], w_exp_gu[0], w_exp_down[0],
                    w_sh_gu[0].astype(BF16), w_sh_down[0].astype(BF16))

    kv_shape = (2, NSA_KV_HEADS, HEAD_DIM)
    branch = lambda p3, br: p3[..., P_KV + br * KV_W:P_KV + (br + 1) * KV_W]
    win_p = branch(proj3_p, 2)[:, lp - min(WINDOW, lp):]
    return (y_p.reshape(x_prompt.shape), y_s.reshape(x_sample.shape),
            branch(proj3_p, 0).reshape(1, bp, lp, *kv_shape), branch(proj3_s, 0).reshape(1, bs, ls, *kv_shape),
            branch(proj3_p, 1).reshape(1, bp, lp, *kv_shape), branch(proj3_s, 1).reshape(1, bs, ls, *kv_shape),
            win_p.reshape(1, bp, win_p.shape[1], *kv_shape), win_roll.reshape(1, bs, wb, *kv_shape),
            gdn_p[None].astype(state_gdn.dtype), gdn_s[None].astype(state_gdn.dtype),
            conv_p[None].astype(state_conv.dtype), conv_s[None].astype(state_conv.dtype))
```

```python
import functools
import math

import jax
import jax.numpy as jnp
from jax import lax
from jax.experimental import pallas as pl
from jax.experimental.pallas import tpu as pltpu

F32, BF16, I32 = jnp.float32, jnp.bfloat16, jnp.int32

D_MODEL = 2048
HEAD_DIM = 128
LANES = 128
SUBLANES = 8
ROW_SLABS = D_MODEL // LANES
GDN_HEADS = 8
GDN_QK = GDN_HEADS * HEAD_DIM
GDN_CONV_DIM = 3 * GDN_QK
CONV_WIDTH = 4
GDN_CHUNK = 64
NSA_HEADS = 8
NSA_KV_HEADS = 2
NSA_GROUP = NSA_HEADS // NSA_KV_HEADS
N_BRANCH = 3
CMP_BLOCK = 64
CMP_SHIFT = CMP_BLOCK.bit_length() - 1
SEL_TOPK = 16
WINDOW = 512
REL_BUCKETS = 32
REL_MAX_DIST = 8192
MOE_TOPK = 8
N_GROUPS = 8
TOPK_GROUPS = 4
ROUTED_SCALE = 2.5
NORM_EPS = 1e-6
NEG_INF = -1e30
ATTN_SCALE = HEAD_DIM ** -0.5
KV_W = 2 * NSA_KV_HEADS * HEAD_DIM

P_QKV = 0
P_Z = P_QKV + GDN_CONV_DIM
P_Q = P_Z + GDN_QK
P_KV = P_Q + NSA_HEADS * HEAD_DIM
P_SMALL = P_KV + N_BRANCH * KV_W
P_DIM = 7168
SM_BETA, SM_DECAY, SM_GATE = 0, GDN_HEADS, 2 * GDN_HEADS
IN_SIZES = (GDN_CONV_DIM, GDN_QK, GDN_HEADS, GDN_HEADS, NSA_HEADS * HEAD_DIM, N_BRANCH * KV_W, N_BRANCH * NSA_HEADS)

VMEM_LIMIT = 56 * 1024 * 1024
ROW_CHUNK = 32
MOE_MB = 256


def _cparams(*sem):
    return pltpu.CompilerParams(dimension_semantics=sem, vmem_limit_bytes=VMEM_LIMIT)


def _silu(x):
    return x * jax.nn.sigmoid(x)


def _row_chunks(n_rows, body):
    def step(i, carry):
        body(pl.multiple_of(i * ROW_CHUNK, ROW_CHUNK))
        return carry
    lax.fori_loop(0, n_rows // ROW_CHUNK, step, 0)


def _rms(x):
    return x * lax.rsqrt(jnp.mean(x * x, axis=-1, keepdims=True) + NORM_EPS)


def _mod_rows(ref, r0, per_row):
    return ref[pl.ds(r0, ROW_CHUNK), :] if per_row else ref[...]


def _mod_operand(mod, tm, rows_per_batch):
    if rows_per_batch % tm == 0:
        per = rows_per_batch // tm
        return (mod[:, None, :], pl.BlockSpec((None, 1, D_MODEL), lambda i, *_: (i // per, 0, 0)), False)
    return (jnp.repeat(mod, rows_per_batch, axis=0), pl.BlockSpec((tm, D_MODEL), lambda i, *_: (i, 0)), True)


def _ada_kernel(c_ref, w_ref, b_ref, o_ref):
    a = _silu(c_ref[...]).astype(BF16)
    o_ref[...] = jnp.dot(a, w_ref[...].astype(BF16), preferred_element_type=F32) + b_ref[...]


def _ada(c, w_ada, b_ada):
    n = c.shape[0]
    npad = -(-n // SUBLANES) * SUBLANES
    cp = jnp.pad(c, ((0, npad - n), (0, 0)))
    tn = 512
    out = pl.pallas_call(
        _ada_kernel,
        grid=(w_ada.shape[1] // tn,),
        in_specs=[pl.BlockSpec((npad, D_MODEL), lambda j: (0, 0)),
                  pl.BlockSpec((D_MODEL, tn), lambda j: (0, j)),
                  pl.BlockSpec((1, tn), lambda j: (0, j))],
        out_specs=pl.BlockSpec((npad, tn), lambda j: (0, j)),
        out_shape=jax.ShapeDtypeStruct((npad, w_ada.shape[1]), F32),
        compiler_params=_cparams("parallel"),
        name="ada",
    )(cp, w_ada, b_ada[None, :])
    return out[:n]


def _inproj_kernel(x_ref, g_ref, sc_ref, sh_ref, w_ref, o_ref, h_scr, *, tm, per_row):
    @pl.when(pl.program_id(1) == 0)
    def _():
        def body(r0):
            y = _rms(x_ref[pl.ds(r0, ROW_CHUNK), :]) * g_ref[...]
            h = y * (1.0 + _mod_rows(sc_ref, r0, per_row)) + _mod_rows(sh_ref, r0, per_row)
            h_scr[pl.ds(r0, ROW_CHUNK), :] = h.astype(BF16)
        _row_chunks(tm, body)
    o_ref[...] = jnp.dot(h_scr[...], w_ref[...], preferred_element_type=F32)


def _inproj(x, ln_g, sc, sh, w_in_p, rows_per_batch, tm):
    rows = x.shape[0]
    tn = 1024
    sc_a, sc_spec, per_row = _mod_operand(sc, tm, rows_per_batch)
    sh_a, sh_spec, _ = _mod_operand(sh, tm, rows_per_batch)
    return pl.pallas_call(
        functools.partial(_inproj_kernel, tm=tm, per_row=per_row),
        grid=(rows // tm, P_DIM // tn),
        in_specs=[pl.BlockSpec((tm, D_MODEL), lambda i, j: (i, 0)),
                  pl.BlockSpec((1, D_MODEL), lambda i, j: (0, 0)),
                  sc_spec, sh_spec,
                  pl.BlockSpec((D_MODEL, tn), lambda i, j: (0, j))],
        out_specs=pl.BlockSpec((tm, tn), lambda i, j: (i, j)),
        out_shape=jax.ShapeDtypeStruct((rows, P_DIM), F32),
        scratch_shapes=[pltpu.VMEM((tm, D_MODEL), BF16)],
        compiler_params=_cparams("parallel", "arbitrary"),
        name="inproj",
    )(x, ln_g[None, :], sc_a, sh_a, w_in_p)


def _pack_w_in(w_in):
    parts = jnp.split(w_in, list(np_cumsum(IN_SIZES)[:-1]), axis=1)
    qkv, z, b_raw, a_raw, nsa_q, nsa_kv, nsa_g = parts
    small = jnp.concatenate([b_raw, a_raw, nsa_g], axis=1)
    w = jnp.concatenate([qkv, z, nsa_q, nsa_kv, small], axis=1)
    return jnp.pad(w, ((0, 0), (0, P_DIM - w.shape[1]))).astype(BF16)


def np_cumsum(sizes):
    out, acc = [], 0
    for s in sizes:
        acc += s
        out.append(acc)
    return out


def _outproj_kernel(og_ref, oc_ref, os_ref, ow_ref, sm_ref, x_ref, w_ref, g_ref, gt_ref, o_ref,
                    mix_in, mix_out, *, tm, per_row):
    def build(r0):
        rows = pl.ds(r0, ROW_CHUNK)
        mix_in[rows, :GDN_QK] = og_ref[rows, :].astype(BF16)
        gates = jax.nn.sigmoid(sm_ref[rows, :])
        for hd in range(NSA_HEADS):
            cols = slice(hd * HEAD_DIM, (hd + 1) * HEAD_DIM)
            acc = None
            for br, ref in enumerate((oc_ref, os_ref, ow_ref)):
                c = SM_GATE + br * NSA_HEADS + hd
                term = gates[:, c:c + 1] * ref[rows, cols]
                acc = term if acc is None else acc + term
            mix_in[rows, GDN_QK + hd * HEAD_DIM:GDN_QK + (hd + 1) * HEAD_DIM] = acc.astype(BF16)
    _row_chunks(tm, build)
    mix_out[...] = jnp.dot(mix_in[...], w_ref[...], preferred_element_type=F32)

    def finish(r0):
        rows = pl.ds(r0, ROW_CHUNK)
        y = _rms(mix_out[rows, :]) * g_ref[...]
        o_ref[rows, :] = x_ref[rows, :] + _mod_rows(gt_ref, r0, per_row) * y
    _row_chunks(tm, finish)


def _outproj(o_gdn, o_cmp, o_sel, o_win, proj, x, w_out_b, ln_g, gt, rows_per_batch, tm):
    rows = x.shape[0]
    gt_a, gt_spec, per_row = _mod_operand(gt, tm, rows_per_batch)
    head_spec = pl.BlockSpec((tm, GDN_QK), lambda i: (i, 0))
    row_spec = pl.BlockSpec((tm, D_MODEL), lambda i: (i, 0))
    return pl.pallas_call(
        functools.partial(_outproj_kernel, tm=tm, per_row=per_row),
        grid=(rows // tm,),
        in_specs=[head_spec, head_spec, head_spec, head_spec,
                  pl.BlockSpec((tm, LANES), lambda i: (i, P_SMALL // LANES)),
                  row_spec,
                  pl.BlockSpec((D_MODEL, D_MODEL), lambda i: (0, 0)),
                  pl.BlockSpec((1, D_MODEL), lambda i: (0, 0)),
                  gt_spec],
        out_specs=row_spec,
        out_shape=jax.ShapeDtypeStruct((rows, D_MODEL), F32),
        scratch_shapes=[pltpu.VMEM((tm, D_MODEL), BF16), pltpu.VMEM((tm, D_MODEL), F32)],
        compiler_params=_cparams("parallel"),
        name="outproj",
    )(o_gdn, o_cmp, o_sel, o_win, proj, x, w_out_b, ln_g[None, :], gt_a)


def _route_kernel(x_ref, g_ref, sc_ref, sh_ref, wr_ref, rb_ref, h_ref, ei_ref, ew_ref, cnt_ref, h_scr,
                  *, tm, per_row, n_exp):
    def body(r0):
        rows = pl.ds(r0, ROW_CHUNK)
        y = _rms(x_ref[rows, :]) * g_ref[...]
        h = y * (1.0 + _mod_rows(sc_ref, r0, per_row)) + _mod_rows(sh_ref, r0, per_row)
        h_scr[rows, :] = h
        for s in range(ROW_SLABS):
            h_ref[pl.ds(r0 * ROW_SLABS + s, ROW_CHUNK, stride=ROW_SLABS), :] = h[:, s * LANES:(s + 1) * LANES]
    _row_chunks(tm, body)

    logits = lax.dot_general(wr_ref[...], h_scr[...], (((1,), (1,)), ((), ())),
                             precision=lax.Precision.HIGHEST, preferred_element_type=F32)
    s = jax.nn.sigmoid(logits)
    sb = s + rb_ref[...]
    gsz = n_exp // N_GROUPS
    sb3 = sb.reshape(N_GROUPS, gsz, tm)
    m1 = jnp.max(sb3, axis=1)
    n_top = jnp.sum((sb3 == m1[:, None, :]).astype(F32), axis=1)
    m2 = jnp.max(jnp.where(sb3 < m1[:, None, :], sb3, -jnp.inf), axis=1)
    gscore = m1 + jnp.where(n_top >= 2.0, m1, m2)
    gid = lax.broadcasted_iota(I32, (N_GROUPS, tm), 0)
    rank = jnp.zeros((N_GROUPS, tm), F32)
    for g in range(N_GROUPS):
        row = gscore[g:g + 1, :]
        ahead = (row > gscore) | ((row == gscore) & (g < gid))
        rank = rank + ahead.astype(F32)
    gsel = rank < float(TOPK_GROUPS)
    emask = jnp.broadcast_to(gsel[:, None, :], (N_GROUPS, gsz, tm)).reshape(n_exp, tm)
    v = jnp.where(emask, sb, NEG_INF)
    eid = lax.broadcasted_iota(I32, (n_exp, tm), 0)
    idxs, wts = [], []
    taken = jnp.zeros((n_exp, tm), F32)
    for _ in range(MOE_TOPK):
        m = jnp.max(v, axis=0, keepdims=True)
        idx = jnp.min(jnp.where(v == m, eid, n_exp), axis=0, keepdims=True)
        hit = eid == idx
        wts.append(jnp.sum(jnp.where(hit, s, 0.0), axis=0, keepdims=True))
        idxs.append(idx)
        taken = taken + hit.astype(F32)
        v = jnp.where(hit, -jnp.inf, v)
    w = jnp.concatenate(wts, axis=0)
    ei_ref[...] = jnp.concatenate(idxs, axis=0)
    ew_ref[...] = w / jnp.sum(w, axis=0, keepdims=True) * ROUTED_SCALE
    cnt_ref[...] = jnp.sum(taken, axis=1, keepdims=True)


def _route(x1, ln_g, sc, sh, w_router, router_bias, rows_per_batch, tm):
    rows = x1.shape[0]
    n_exp = w_router.shape[0]
    sc_a, sc_spec, per_row = _mod_operand(sc, tm, rows_per_batch)
    sh_a, sh_spec, _ = _mod_operand(sh, tm, rows_per_batch)
    return pl.pallas_call(
        functools.partial(_route_kernel, tm=tm, per_row=per_row, n_exp=n_exp),
        grid=(rows // tm,),
        in_specs=[pl.BlockSpec((tm, D_MODEL), lambda i: (i, 0)),
                  pl.BlockSpec((1, D_MODEL), lambda i: (0, 0)),
                  sc_spec, sh_spec,
                  pl.BlockSpec((n_exp, D_MODEL), lambda i: (0, 0)),
                  pl.BlockSpec((n_exp, 1), lambda i: (0, 0))],
        out_specs=[pl.BlockSpec((tm * ROW_SLABS, LANES), lambda i: (i, 0)),
                   pl.BlockSpec((MOE_TOPK, tm), lambda i: (0, i)),
                   pl.BlockSpec((MOE_TOPK, tm), lambda i: (0, i)),
                   pl.BlockSpec((None, n_exp, 1), lambda i: (i, 0, 0))],
        out_shape=[jax.ShapeDtypeStruct((rows * ROW_SLABS, LANES), F32),
                   jax.ShapeDtypeStruct((MOE_TOPK, rows), I32),
                   jax.ShapeDtypeStruct((MOE_TOPK, rows), F32),
                   jax.ShapeDtypeStruct((rows // tm, n_exp, 1), F32)],
        scratch_shapes=[pltpu.VMEM((tm, D_MODEL), F32)],
        compiler_params=_cparams("parallel"),
        name="route",
    )(x1, ln_g[None, :], sc_a, sh_a, w_router, router_bias[:, None])


def _expert_kernel(blk_e_ref, nused_ref, src_ref, nxt_ref, dst_ref, rw_ref, wgu_ref, wd_ref, h_hbm,
                   y_hbm, xbuf, xmat, ybuf, gsem, ssem, *, mb, ff, dump0):
    i = pl.program_id(0)
    nused = nused_ref[0]
    slot = lax.rem(i, 2)
    slab_rows = mb * ROW_SLABS

    def gather(idx_ref, to_slot, r):
        tok = idx_ref[0, 0, r]
        return pltpu.make_async_copy(
            h_hbm.at[pl.ds(pl.multiple_of(tok * ROW_SLABS, ROW_SLABS), ROW_SLABS), :],
            xbuf.at[pl.ds(pl.multiple_of(to_slot * slab_rows + r * ROW_SLABS, ROW_SLABS), ROW_SLABS), :],
            gsem.at[to_slot])

    def scatter(r):
        row = dst_ref[0, 0, r]
        return pltpu.make_async_copy(
            ybuf.at[pl.ds(r * ROW_SLABS, ROW_SLABS), :],
            y_hbm.at[pl.ds(pl.multiple_of(row * ROW_SLABS, ROW_SLABS), ROW_SLABS), :],
            ssem.at[0])

    @pl.when(i == 0)
    def _():
        for r in range(mb):
            gather(src_ref, 0, r).start()
        ybuf[...] = jnp.zeros_like(ybuf)
        fill = pltpu.make_async_copy(ybuf, y_hbm.at[pl.ds(dump0 * ROW_SLABS, slab_rows), :], ssem.at[0])
        fill.start()
        fill.wait()

    @pl.when(i < nused)
    def _():
        @pl.when(i + 1 < nused)
        def _():
            for r in range(mb):
                gather(nxt_ref, 1 - slot, r).start()

        for r in range(mb):
            gather(src_ref, slot, r).wait()
        base = pl.multiple_of(slot * slab_rows, slab_rows)
        for s in range(ROW_SLABS):
            xmat[:, s * LANES:(s + 1) * LANES] = xbuf[pl.ds(base + s, mb, stride=ROW_SLABS), :].astype(BF16)
        gu = jnp.dot(xmat[...], wgu_ref[...].astype(BF16), preferred_element_type=F32)
        hid = (_silu(gu[:, :ff]) * gu[:, ff:]).astype(BF16)
        y = jnp.dot(hid, wd_ref[...].astype(BF16), preferred_element_type=F32) * rw_ref[...]

        @pl.when(i > 0)
        def _():
            for r in range(mb):
                scatter(r).wait()
        for s in range(ROW_SLABS):
            ybuf[pl.ds(s, mb, stride=ROW_SLABS), :] = y[:, s * LANES:(s + 1) * LANES]
        for r in range(mb):
            scatter(r).start()

        @pl.when(i == nused - 1)
        def _():
            for r in range(mb):
                scatter(r).wait()


def _dispatch_plan(eidx, ew, counts, n_exp, mb):
    t_all = eidx.shape[0]
    n_asg = t_all * MOE_TOPK
    n_blk = -(-(n_asg + n_exp * (mb - 1)) // mb)
    n_slot = n_blk * mb
    plane_rows = t_all
    pad = (-counts) % mb
    asg = jnp.arange(n_asg, dtype=I32)
    last_key = 2 * n_exp
    pad_key = jnp.where(jnp.arange(mb - 1, dtype=I32)[None, :] < pad[:, None],
                        2 * jnp.arange(n_exp, dtype=I32)[:, None] + 1, last_key).reshape(-1)
    n_fill = n_slot - n_asg
    keys = jnp.concatenate([2 * eidx.reshape(-1), pad_key, jnp.full((n_fill - pad_key.shape[0],), last_key, I32)])
    src = jnp.concatenate([asg // MOE_TOPK, jnp.zeros((n_fill,), I32)])
    dst = jnp.concatenate([(asg % MOE_TOPK) * plane_rows + asg // MOE_TOPK, jnp.full((n_fill,), -1, I32)])
    wts = jnp.concatenate([ew.reshape(-1), jnp.zeros((n_fill,), F32)])
    _, rows_src, rows_dst, rows_w = lax.sort((keys, src, dst, wts), num_keys=1, is_stable=True)
    slot_id = jnp.arange(n_slot, dtype=I32)
    rows_dst = jnp.where(rows_dst < 0, MOE_TOPK * plane_rows + slot_id % mb, rows_dst)
    pends = jnp.cumsum(counts + pad)
    blk_e = jnp.minimum(jnp.searchsorted(pends, jnp.arange(n_blk, dtype=I32) * mb, side='right'),
                        n_exp - 1).astype(I32)
    nused = (pends[-1] // mb).astype(I32).reshape(1)
    return n_blk, plane_rows, rows_src, rows_dst, rows_w, blk_e, nused


def _experts(h_slabs, eidx, ew, counts, w_gu, w_down):
    n_exp, _, ff2 = w_gu.shape
    ff = ff2 // 2
    mb = MOE_MB
    n_blk, plane_rows, rows_src, rows_dst, rows_w, blk_e, nused = _dispatch_plan(eidx, ew, counts, n_exp, mb)
    idx_spec = lambda f: pl.BlockSpec((1, 1, mb), f, memory_space=pltpu.SMEM)
    y = pl.pallas_call(
        functools.partial(_expert_kernel, mb=mb, ff=ff, dump0=MOE_TOPK * plane_rows),
        grid_spec=pltpu.PrefetchScalarGridSpec(
            num_scalar_prefetch=2,
            grid=(n_blk,),
            in_specs=[idx_spec(lambda i, be, nu: (i, 0, 0)),
                      idx_spec(lambda i, be, nu: (jnp.minimum(i + 1, n_blk - 1), 0, 0)),
                      idx_spec(lambda i, be, nu: (i, 0, 0)),
                      pl.BlockSpec((mb, 1), lambda i, be, nu: (i, 0)),
                      pl.BlockSpec((None, D_MODEL, ff2), lambda i, be, nu: (be[i], 0, 0)),
                      pl.BlockSpec((None, ff, D_MODEL), lambda i, be, nu: (be[i], 0, 0)),
                      pl.BlockSpec(memory_space=pl.ANY)],
            out_specs=pl.BlockSpec(memory_space=pl.ANY),
            scratch_shapes=[pltpu.VMEM((2 * mb * ROW_SLABS, LANES), F32),
                            pltpu.VMEM((mb, D_MODEL), BF16),
                            pltpu.VMEM((mb * ROW_SLABS, LANES), F32),
                            pltpu.SemaphoreType.DMA((2,)),
                            pltpu.SemaphoreType.DMA((1,))]),
        out_shape=jax.ShapeDtypeStruct(((MOE_TOPK * plane_rows + mb) * ROW_SLABS, LANES), F32),
        compiler_params=_cparams("arbitrary"),
        name="experts",
    )(blk_e, nused, rows_src.reshape(n_blk, 1, mb), rows_src.reshape(n_blk, 1, mb),
      rows_dst.reshape(n_blk, 1, mb), rows_w[:, None], w_gu, w_down, h_slabs)
    return y


def _combine_kernel(*refs, tm, per_row, ff):
    y_refs = refs[:MOE_TOPK]
    h_ref, wgu_ref, wd_ref, x_ref, g_ref, gt_ref, o_ref, fsum, hmat, f_scr = refs[MOE_TOPK:]

    def add_planes(r0):
        rows = pl.ds(r0 * ROW_SLABS, ROW_CHUNK * ROW_SLABS)
        acc = y_refs[0][rows, :]
        for y_ref in y_refs[1:]:
            acc = acc + y_ref[rows, :]
        fsum[rows, :] = acc
    _row_chunks(tm, add_planes)
    for s in range(ROW_SLABS):
        hmat[:, s * LANES:(s + 1) * LANES] = h_ref[pl.ds(s, tm, stride=ROW_SLABS), :].astype(BF16)
    gu = jnp.dot(hmat[...], wgu_ref[...], preferred_element_type=F32)
    hid = (_silu(gu[:, :ff]) * gu[:, ff:]).astype(BF16)
    f_scr[...] = jnp.dot(hid, wd_ref[...], preferred_element_type=F32)
    for s in range(ROW_SLABS):
        f_scr[:, s * LANES:(s + 1) * LANES] += fsum[pl.ds(s, tm, stride=ROW_SLABS), :]

    def finish(r0):
        rows = pl.ds(r0, ROW_CHUNK)
        y = _rms(f_scr[rows, :]) * g_ref[...]
        o_ref[rows, :] = x_ref[rows, :] + _mod_rows(gt_ref, r0, per_row) * y
    _row_chunks(tm, finish)


def _combine(y_planes, h_slabs, row0, w_sh_gu_b, w_sh_down_b, x1, ln_g, gt, rows_per_batch, tm):
    rows = x1.shape[0]
    ff = w_sh_down_b.shape[0]
    t0 = row0 // tm
    plane_tiles = h_slabs.shape[0] // (tm * ROW_SLABS)
    gt_a, gt_spec, per_row = _mod_operand(gt, tm, rows_per_batch)
    row_spec = pl.BlockSpec((tm, D_MODEL), lambda i: (i, 0))
    plane_specs = [pl.BlockSpec((tm * ROW_SLABS, LANES), lambda i, k=k: (k * plane_tiles + t0 + i, 0))
                   for k in range(MOE_TOPK)]
    return pl.pallas_call(
        functools.partial(_combine_kernel, tm=tm, per_row=per_row, ff=ff),
        grid=(rows // tm,),
        in_specs=plane_specs + [
                  pl.BlockSpec((tm * ROW_SLABS, LANES), lambda i: (t0 + i, 0)),
                  pl.BlockSpec((D_MODEL, 2 * ff), lambda i: (0, 0)),
                  pl.BlockSpec((ff, D_MODEL), lambda i: (0, 0)),
                  row_spec,
                  pl.BlockSpec((1, D_MODEL), lambda i: (0, 0)),
                  gt_spec],
        out_specs=row_spec,
        out_shape=jax.ShapeDtypeStruct((rows, D_MODEL), F32),
        scratch_shapes=[pltpu.VMEM((tm * ROW_SLABS, LANES), F32),
                        pltpu.VMEM((tm, D_MODEL), BF16),
                        pltpu.VMEM((tm, D_MODEL), F32)],
        compiler_params=_cparams("parallel"),
        name="combine",
    )(*([y_planes] * MOE_TOPK), h_slabs, w_sh_gu_b, w_sh_down_b, x1, ln_g[None, :], gt_a)


def _moe(x1_p, x1_s, mod_p, mod_s, rpb_p, rpb_s, ln_pre, ln_post, w_router, router_bias, w_gu, w_down,
         w_sh_gu_b, w_sh_down_b):
    tm_s = x1_s.shape[0]
    h_p, ei_p, ew_p, cnt_p = _route(x1_p, ln_pre, mod_p[0], mod_p[1], w_router, router_bias, rpb_p, 256)
    h_s, ei_s, ew_s, cnt_s = _route(x1_s, ln_pre, mod_s[0], mod_s[1], w_router, router_bias, rpb_s, tm_s)
    h_all = jnp.concatenate([h_p, h_s], axis=0)
    eidx = jnp.concatenate([ei_p, ei_s], axis=1).T
    ew = jnp.concatenate([ew_p, ew_s], axis=1).T
    counts = (jnp.sum(cnt_p, axis=(0, 2)) + jnp.sum(cnt_s, axis=(0, 2))).astype(I32)
    y4 = _experts(h_all, eidx, ew, counts, w_gu, w_down)
    out_p = _combine(y4, h_all, 0, w_sh_gu_b, w_sh_down_b, x1_p, ln_post, mod_p[2], rpb_p, 128)
    out_s = _combine(y4, h_all, x1_p.shape[0], w_sh_gu_b, w_sh_down_b, x1_s, ln_post, mod_s[2], rpb_s, tm_s)
    return out_p, out_s


def _t5_bucket(dist):
    n = jnp.maximum(dist, 0)
    max_exact = REL_BUCKETS // 2
    nf = jnp.maximum(n, 1).astype(F32)
    large = max_exact + (jnp.log(nf / max_exact) / math.log(REL_MAX_DIST / max_exact)
                         * (REL_BUCKETS - max_exact)).astype(I32)
    return jnp.where(n < max_exact, n, jnp.minimum(large, REL_BUCKETS - 1))


def _bucket_bias(bucket, rel_ref, head):
    out = jnp.zeros(bucket.shape, F32)
    for b in range(REL_BUCKETS):
        out = jnp.where(bucket == b, rel_ref[b, head], out)
    return out


def _bias_tile_kernel(rel_ref, o_ref, *, d_min, lo, hi):
    d = pl.program_id(0) + d_min
    row = lax.broadcasted_iota(I32, (LANES, LANES), 0)
    col = lax.broadcasted_iota(I32, (LANES, LANES), 1)
    dist = d * LANES + row - col
    bucket = _t5_bucket(dist)
    visible = (dist >= lo) & (dist < hi)
    for h in range(NSA_HEADS):
        o_ref[h, 0] = jnp.where(visible, _bucket_bias(bucket, rel_ref, h), NEG_INF)


def _bias_tiles(rel_bias, d_min, n_d, lo, hi):
    return pl.pallas_call(
        functools.partial(_bias_tile_kernel, d_min=d_min, lo=lo, hi=hi),
        grid=(n_d,),
        in_specs=[pl.BlockSpec(memory_space=pltpu.SMEM)],
        out_specs=pl.BlockSpec((NSA_HEADS, 1, LANES, LANES), lambda d: (0, d, 0, 0)),
        out_shape=jax.ShapeDtypeStruct((NSA_HEADS, n_d, LANES, LANES), F32),
        compiler_params=_cparams("parallel"),
        name="bias_tiles",
    )(rel_bias)


CMP_ROWS = 128


def _cmp_select_kernel(rel_ref, q_ref, k_ref, v_ref, o_ref, neg_ref, idx_ref, *, tq, nb, pos0):
    hkv = pl.program_id(1)
    rq = min(tq, CMP_ROWS)
    k = k_ref[...].astype(BF16)
    v = v_ref[...].astype(BF16)
    blk = lax.broadcasted_iota(I32, (rq, nb), 1)
    blkf = blk.astype(F32)
    tile0 = pl.program_id(2) * tq

    def step(c):
        r0 = c * rq
        rows = slice(r0, r0 + rq)
        qpos = pos0 + tile0 + r0 + lax.broadcasted_iota(I32, (rq, nb), 0)
        dist = qpos - (blk * CMP_BLOCK + (CMP_BLOCK - 1))
        bucket = _t5_bucket(dist)
        seen = dist >= 0
        psum = jnp.zeros((rq, nb), F32)
        for g in range(NSA_GROUP):
            cols = slice(g * HEAD_DIM, (g + 1) * HEAD_DIM)
            s = lax.dot_general(q_ref[rows, cols].astype(BF16), k, (((1,), (1,)), ((), ())),
                                preferred_element_type=F32) * ATTN_SCALE
            bias = jnp.zeros((rq, nb), F32)
            for b in range(REL_BUCKETS):
                bias = jnp.where(bucket == b, rel_ref[b, hkv * NSA_GROUP + g], bias)
            s = jnp.where(seen, s + bias, NEG_INF)
            e = jnp.exp(s - jnp.max(s, axis=1, keepdims=True))
            p = e / jnp.sum(e, axis=1, keepdims=True) * seen.astype(F32)
            o_ref[rows, cols] = jnp.dot(p.astype(BF16), v, preferred_element_type=F32)
            psum = psum + p
        cur = lax.shift_right_logical(qpos, CMP_SHIFT)
        score = jnp.where(blk < cur, psum, -1.0)
        chosen = blk == cur
        lane = lax.broadcasted_iota(I32, (rq, SEL_TOPK), 1)
        picks = jnp.where(lane == 0, cur[:, :SEL_TOPK].astype(F32), -1.0)
        for r in range(1, SEL_TOPK):
            m = jnp.max(score, axis=1, keepdims=True)
            first = jnp.min(jnp.where(score == m, blkf, float(nb)), axis=1, keepdims=True)
            hit = blkf == first
            ok = m >= 0.0
            chosen = chosen | (hit & ok)
            picks = jnp.where(lane == r, jnp.where(ok, first, -1.0), picks)
            score = jnp.where(hit, -2.0, score)
        neg_ref[rows, :] = jnp.where(chosen, 0.0, NEG_INF).astype(BF16)
        idx_ref[rows, :] = picks.astype(I32)

    for c in range(tq // rq):
        step(c)


def _cmp_select(q3, col_blk0, kvc, rel_bias, tq, pos0):
    b, lq, _ = q3.shape
    nb = kvc.shape[1]
    gw = NSA_GROUP * HEAD_DIM
    return pl.pallas_call(
        functools.partial(_cmp_select_kernel, tq=tq, nb=nb, pos0=pos0),
        grid=(b, NSA_KV_HEADS, lq // tq),
        in_specs=[pl.BlockSpec(memory_space=pltpu.SMEM),
                  pl.BlockSpec((None, tq, gw), lambda i, h, t: (i, t, col_blk0 + h)),
                  pl.BlockSpec((None, nb, HEAD_DIM), lambda i, h, t: (i, 0, h)),
                  pl.BlockSpec((None, nb, HEAD_DIM), lambda i, h, t: (i, 0, NSA_KV_HEADS + h))],
        out_specs=[pl.BlockSpec((None, tq, gw), lambda i, h, t: (i, t, h)),
                   pl.BlockSpec((None, None, tq, nb), lambda i, h, t: (h, i, t, 0)),
                   pl.BlockSpec((None, None, tq, SEL_TOPK), lambda i, h, t: (h, i, t, 0))],
        out_shape=[jax.ShapeDtypeStruct((b, lq, NSA_HEADS * HEAD_DIM), F32),
                   jax.ShapeDtypeStruct((NSA_KV_HEADS, b, lq, nb), BF16),
                   jax.ShapeDtypeStruct((NSA_KV_HEADS, b, lq, SEL_TOPK), I32)],
        compiler_params=_cparams("parallel", "parallel", "parallel"),
        name="cmp_select",
    )(rel_bias, q3, kvc, kvc)


ATT_T = 512
ATT_SUB = 128


def _flash_kernel(*refs, selected, d_min):
    if selected:
        q_ref, neg_ref, k_ref, v_ref, t_ref, o_ref, m_scr, l_scr, acc_scr = refs
    else:
        q_ref, k_ref, v_ref, t_ref, o_ref, m_scr, l_scr, acc_scr = refs
    qi, kk = pl.program_id(2), pl.program_id(3)
    kj = kk if selected else qi - 1 + kk
    nsub = ATT_T // ATT_SUB

    @pl.when(kk == 0)
    def _():
        m_scr[...] = jnp.full(m_scr.shape, NEG_INF, F32)
        l_scr[...] = jnp.zeros(l_scr.shape, F32)
        acc_scr[...] = jnp.zeros(acc_scr.shape, F32)

    @pl.when((kj >= 0) & (kj <= qi))
    def _():
        kb = k_ref[...].astype(BF16)
        if selected:
            nb = neg_ref.shape[1]
            key_blk = lax.shift_right_logical(kj * ATT_T + lax.broadcasted_iota(I32, (ATT_T, nb), 0), CMP_SHIFT)
            onehot = jnp.where(key_blk == lax.broadcasted_iota(I32, (ATT_T, nb), 1), 1.0, 0.0)
            kb = jnp.concatenate([kb, onehot.astype(BF16)], axis=1)
        vb = v_ref[...].astype(BF16)
        d0 = (qi - kj) * nsub - d_min
        for a in range(nsub):
            rows = slice(a * ATT_SUB, (a + 1) * ATT_SUB)
            qa = jnp.concatenate([q_ref[rows, g * HEAD_DIM:(g + 1) * HEAD_DIM] for g in range(NSA_GROUP)],
                                 axis=0).astype(BF16)
            if selected:
                qa = jnp.concatenate([qa, jnp.concatenate([neg_ref[rows, :]] * NSA_GROUP, axis=0)], axis=1)
            s = lax.dot_general(qa, kb, (((1,), (1,)), ((), ())), preferred_element_type=F32)
            bias = jnp.concatenate([jnp.concatenate([t_ref[g, d0 + a - c] for c in range(nsub)], axis=1)
                                    for g in range(NSA_GROUP)], axis=0)
            s = s * ATTN_SCALE + bias
            m_prev = m_scr[a]
            m_new = jnp.maximum(m_prev, jnp.max(s, axis=1, keepdims=True))
            alpha = jnp.exp(m_prev - m_new)
            p = jnp.exp(s - m_new)
            l_scr[a] = alpha * l_scr[a] + jnp.sum(p, axis=1, keepdims=True)
            acc_scr[a] = alpha * acc_scr[a] + jnp.dot(p.astype(BF16), vb, preferred_element_type=F32)
            m_scr[a] = m_new

    @pl.when(kk == pl.num_programs(3) - 1)
    def _():
        for a in range(nsub):
            out = acc_scr[a] / l_scr[a]
            for g in range(NSA_GROUP):
                o_ref[a * ATT_SUB:(a + 1) * ATT_SUB, g * HEAD_DIM:(g + 1) * HEAD_DIM] = out[g * ATT_SUB:(g + 1) * ATT_SUB]


def _flash(proj, neg, tiles, b, l, branch, d_min):
    selected = neg is not None
    nq = l // ATT_T
    nk = nq if selected else 2
    gw = NSA_GROUP * HEAD_DIM
    n_d = tiles.shape[1]
    k_col = (P_KV + branch * KV_W) // HEAD_DIM

    def kj_of(qi, kk):
        return jnp.clip(kk if selected else qi - 1 + kk, 0, qi)

    in_specs = [pl.BlockSpec((ATT_T, gw), lambda i, h, qi, kk: (i * nq + qi, P_Q // gw + h))]
    args = [proj]
    if selected:
        in_specs.append(pl.BlockSpec((None, ATT_T, neg.shape[2]), lambda i, h, qi, kk: (h, i * nq + qi, 0)))
        args.append(neg)
    in_specs += [pl.BlockSpec((ATT_T, HEAD_DIM), lambda i, h, qi, kk: (i * nq + kj_of(qi, kk), k_col + h)),
                 pl.BlockSpec((ATT_T, HEAD_DIM),
                              lambda i, h, qi, kk: (i * nq + kj_of(qi, kk), k_col + NSA_KV_HEADS + h)),
                 pl.BlockSpec((NSA_GROUP, n_d, LANES, LANES), lambda i, h, qi, kk: (h, 0, 0, 0))]
    args += [proj, proj, tiles]
    return pl.pallas_call(
        functools.partial(_flash_kernel, selected=selected, d_min=d_min),
        grid=(b, NSA_KV_HEADS, nq, nk),
        in_specs=in_specs,
        out_specs=pl.BlockSpec((ATT_T, gw), lambda i, h, qi, kk: (i * nq + qi, h)),
        out_shape=jax.ShapeDtypeStruct((b * l, NSA_HEADS * HEAD_DIM), F32),
        scratch_shapes=[pltpu.VMEM((ATT_T // ATT_SUB, NSA_GROUP * ATT_SUB, 1), F32),
                        pltpu.VMEM((ATT_T // ATT_SUB, NSA_GROUP * ATT_SUB, 1), F32),
                        pltpu.VMEM((ATT_T // ATT_SUB, NSA_GROUP * ATT_SUB, HEAD_DIM), F32)],
        compiler_params=_cparams("parallel", "parallel", "parallel", "arbitrary"),
        name="flash_sel" if selected else "flash_win",
    )(*args)


QPAD = SUBLANES
NEW_PAD = LANES


def _masked_attend(s, mask, parts):
    s = [jnp.where(m, x, NEG_INF) for x, m in zip(s, mask)]
    top = s[0].max(axis=1, keepdims=True)
    for x in s[1:]:
        top = jnp.maximum(top, x.max(axis=1, keepdims=True))
    e = [jnp.exp(x - top) for x in s]
    den = sum(x.sum(axis=1, keepdims=True) for x in e)
    out = None
    for x, m, v in zip(e, mask, parts):
        term = jnp.dot((x / den * m.astype(F32)).astype(BF16), v, preferred_element_type=F32)
        out = term if out is None else out + term
    return out


def _group_rows(q_ref, hkv):
    return jnp.concatenate([q_ref[:, (hkv * NSA_GROUP + g) * HEAD_DIM:(hkv * NSA_GROUP + g + 1) * HEAD_DIM]
                            for g in range(NSA_GROUP)], axis=0).astype(BF16)


def _rows_bias(bucket, rel_ref, hkv):
    return jnp.concatenate([_bucket_bias(bucket[g * QPAD:(g + 1) * QPAD], rel_ref, hkv * NSA_GROUP + g)
                            for g in range(NSA_GROUP)], axis=0)


def _sel_sample_kernel(pick_ref, page_ref, rel_ref, q_ref, new_ref, kpos_ref, own_ref, pool_hbm, o_ref,
                       kbuf, vbuf, sem, *, n_b, n_tok, n_pick, past, pages_per_seq):
    i = pl.program_id(0)
    slot = lax.rem(i, 2)
    n_keys = n_tok * n_pick * CMP_BLOCK
    half_pages = PAGE_ROWS // CMP_BLOCK

    def fetch(b, to_slot, hkv, tok, j, part):
        blk = jnp.maximum(pick_ref[((hkv * n_b + b) * QPAD + tok) * SEL_TOPK + 1 + j], 0)
        page = page_ref[b * pages_per_seq + blk // half_pages]
        row0 = pl.multiple_of((page * half_pages + lax.rem(blk, half_pages)) * CMP_BLOCK, CMP_BLOCK)
        dst = (kbuf, vbuf)[part]
        return pltpu.make_async_copy(
            pool_hbm.at[pl.ds(row0, CMP_BLOCK), pl.ds((part * NSA_KV_HEADS + hkv) * HEAD_DIM, HEAD_DIM)],
            dst.at[to_slot, hkv, pl.ds((tok * n_pick + j) * CMP_BLOCK, CMP_BLOCK), :],
            sem.at[to_slot])

    def fetch_all(b, to_slot, wait):
        for hkv in range(NSA_KV_HEADS):
            for tok in range(n_tok):
                for j in range(n_pick):
                    for part in range(2):
                        cp = fetch(b, to_slot, hkv, tok, j, part)
                        cp.wait() if wait else cp.start()

    @pl.when(i == 0)
    def _():
        fetch_all(0, 0, False)
        zeros = jnp.zeros((NEW_PAD, HEAD_DIM), F32)
        for s in range(2):
            for hkv in range(NSA_KV_HEADS):
                kbuf[s, hkv, n_keys:n_keys + NEW_PAD, :] = zeros
                vbuf[s, hkv, n_keys:n_keys + NEW_PAD, :] = zeros

    @pl.when(i + 1 < n_b)
    def _():
        fetch_all(i + 1, 1 - slot, False)

    fetch_all(i, slot, True)
    kpos = kpos_ref[...]
    n_all = n_keys + NEW_PAD
    tok_of_row = lax.rem(lax.broadcasted_iota(I32, (NSA_GROUP * QPAD, n_all), 0), QPAD)
    for hkv in range(NSA_KV_HEADS):
        kbuf[slot, hkv, n_keys:n_keys + n_tok, :] = new_ref[:, hkv * HEAD_DIM:(hkv + 1) * HEAD_DIM]
        vbuf[slot, hkv, n_keys:n_keys + n_tok, :] = new_ref[:, (NSA_KV_HEADS + hkv) * HEAD_DIM:
                                                            (NSA_KV_HEADS + hkv + 1) * HEAD_DIM]
        q = _group_rows(q_ref, hkv)
        s = lax.dot_general(q, kbuf[slot, hkv].astype(BF16), (((1,), (1,)), ((), ())),
                            preferred_element_type=F32) * ATTN_SCALE
        kp = kpos[hkv:hkv + 1, :]
        own = own_ref[hkv:hkv + 1, :]
        dist = past + tok_of_row - kp
        mask = (kp >= 0) & (dist >= 0) & ((own < 0) | (own == tok_of_row))
        s = s + _rows_bias(_t5_bucket(dist), rel_ref, hkv)
        out = _masked_attend([s], [mask], [vbuf[slot, hkv].astype(BF16)])
        for g in range(NSA_GROUP):
            c = (hkv * NSA_GROUP + g) * HEAD_DIM
            o_ref[:, c:c + HEAD_DIM] = out[g * QPAD:(g + 1) * QPAD, :]


PAGE_ROWS = 128


def _sel_sample(picks, page_table, rel_bias, q_pad, proj3, pool2d, past, n_tok):
    n_b = q_pad.shape[0]
    n_pick = SEL_TOPK - 1
    n_keys = n_tok * n_pick * CMP_BLOCK
    blk = picks[:, :, :n_tok, 1:]
    kpos = jnp.where(blk[..., None] >= 0, blk[..., None] * CMP_BLOCK + jnp.arange(CMP_BLOCK, dtype=I32), -1)
    kpos = kpos.transpose(1, 0, 2, 3, 4).reshape(n_b, NSA_KV_HEADS, n_keys)
    new_pos = jnp.where(jnp.arange(NEW_PAD) < n_tok, past + jnp.arange(NEW_PAD), -1).astype(I32)
    kpos = jnp.concatenate([kpos, jnp.broadcast_to(new_pos, (n_b, NSA_KV_HEADS, NEW_PAD))], axis=2)
    own = jnp.concatenate([jnp.repeat(jnp.arange(n_tok, dtype=I32), n_pick * CMP_BLOCK),
                           jnp.full((NEW_PAD,), -1, I32)])
    own = jnp.broadcast_to(own, (NSA_KV_HEADS, n_keys + NEW_PAD))
    sel_col = (P_KV + KV_W) // KV_W
    return pl.pallas_call(
        functools.partial(_sel_sample_kernel, n_b=n_b, n_tok=n_tok, n_pick=n_pick, past=past,
                          pages_per_seq=page_table.shape[1]),
        grid_spec=pltpu.PrefetchScalarGridSpec(
            num_scalar_prefetch=2,
            grid=(n_b,),
            in_specs=[pl.BlockSpec(memory_space=pltpu.SMEM),
                      pl.BlockSpec((None, QPAD, NSA_HEADS * HEAD_DIM), lambda i, pk, pg: (i, 0, 0)),
                      pl.BlockSpec((None, n_tok, KV_W), lambda i, pk, pg: (i, 0, sel_col)),
                      pl.BlockSpec((None, NSA_KV_HEADS, n_keys + NEW_PAD), lambda i, pk, pg: (i, 0, 0)),
                      pl.BlockSpec((NSA_KV_HEADS, n_keys + NEW_PAD), lambda i, pk, pg: (0, 0)),
                      pl.BlockSpec(memory_space=pl.ANY)],
            out_specs=pl.BlockSpec((None, QPAD, NSA_HEADS * HEAD_DIM), lambda i, pk, pg: (i, 0, 0)),
            scratch_shapes=[pltpu.VMEM((2, NSA_KV_HEADS, n_keys + NEW_PAD, HEAD_DIM), F32),
                            pltpu.VMEM((2, NSA_KV_HEADS, n_keys + NEW_PAD, HEAD_DIM), F32),
                            pltpu.SemaphoreType.DMA((2,))]),
        out_shape=jax.ShapeDtypeStruct((n_b, QPAD, NSA_HEADS * HEAD_DIM), F32),
        compiler_params=_cparams("arbitrary"),
        name="sel_sample",
    )(picks.reshape(-1), page_table.reshape(-1), rel_bias, q_pad, proj3, kpos, own, pool2d)


def _win_sample_kernel(rel_ref, q_ref, new_ref, buf_ref, o_ref, roll_ref, new_pad, *, n_tok, past):
    wb = buf_ref.shape[0]
    new_pad[...] = jnp.zeros(new_pad.shape, F32)
    new_pad[0:n_tok, :] = new_ref[...]
    rows = NSA_GROUP * QPAD
    tok_old = lax.rem(lax.broadcasted_iota(I32, (rows, wb), 0), QPAD)
    tok_new = lax.rem(lax.broadcasted_iota(I32, (rows, QPAD), 0), QPAD)
    dist_old = tok_old + wb - lax.broadcasted_iota(I32, (rows, wb), 1)
    new_col = lax.broadcasted_iota(I32, (rows, QPAD), 1)
    dist_new = tok_new - new_col
    kpos_old = past - wb + lax.broadcasted_iota(I32, (rows, wb), 1)
    mask_old = (dist_old >= 0) & (dist_old < WINDOW) & (kpos_old >= 0)
    mask_new = (dist_new >= 0) & (dist_new < WINDOW) & (new_col < n_tok)
    tb = (((1,), (1,)), ((), ()))
    for hkv in range(NSA_KV_HEADS):
        kc = slice(hkv * HEAD_DIM, (hkv + 1) * HEAD_DIM)
        vc = slice((NSA_KV_HEADS + hkv) * HEAD_DIM, (NSA_KV_HEADS + hkv + 1) * HEAD_DIM)
        q = _group_rows(q_ref, hkv)
        s_old = lax.dot_general(q, buf_ref[:, kc].astype(BF16), tb, preferred_element_type=F32) * ATTN_SCALE
        s_new = lax.dot_general(q, new_pad[:, kc].astype(BF16), tb, preferred_element_type=F32) * ATTN_SCALE
        s_old = s_old + _rows_bias(_t5_bucket(dist_old), rel_ref, hkv)
        s_new = s_new + _rows_bias(_t5_bucket(dist_new), rel_ref, hkv)
        out = _masked_attend([s_old, s_new], [mask_old, mask_new],
                             [buf_ref[:, vc].astype(BF16), new_pad[:, vc].astype(BF16)])
        for g in range(NSA_GROUP):
            c = (hkv * NSA_GROUP + g) * HEAD_DIM
            o_ref[:, c:c + HEAD_DIM] = out[g * QPAD:(g + 1) * QPAD, :]
    roll_ref[0:wb - n_tok, :] = buf_ref[n_tok:wb, :]
    roll_ref[wb - n_tok:wb, :] = new_ref[...]


def _win_sample(rel_bias, q_pad, proj3, win_buf2d, past, n_tok):
    n_b, wb, _ = win_buf2d.shape
    win_col = (P_KV + 2 * KV_W) // KV_W
    return pl.pallas_call(
        functools.partial(_win_sample_kernel, n_tok=n_tok, past=past),
        grid=(n_b,),
        in_specs=[pl.BlockSpec(memory_space=pltpu.SMEM),
                  pl.BlockSpec((None, QPAD, NSA_HEADS * HEAD_DIM), lambda i: (i, 0, 0)),
                  pl.BlockSpec((None, n_tok, KV_W), lambda i: (i, 0, win_col)),
                  pl.BlockSpec((None, wb, KV_W), lambda i: (i, 0, 0))],
        out_specs=[pl.BlockSpec((None, QPAD, NSA_HEADS * HEAD_DIM), lambda i: (i, 0, 0)),
                   pl.BlockSpec((None, wb, KV_W), lambda i: (i, 0, 0))],
        out_shape=[jax.ShapeDtypeStruct((n_b, QPAD, NSA_HEADS * HEAD_DIM), F32),
                   jax.ShapeDtypeStruct((n_b, wb, KV_W), F32)],
        scratch_shapes=[pltpu.VMEM((QPAD, KV_W), F32)],
        compiler_params=_cparams("parallel"),
        name="win_sample",
    )(rel_bias, q_pad, proj3, win_buf2d)


CMP_GROUP = 64
CMP_PITCH = CMP_BLOCK + 8
N_KV_SLABS = KV_W // HEAD_DIM


def _compress_kernel(blk_ref, src_hbm, w1_ref, pe_ref, b1_ref, w2_ref, o_ref, buf, flat, c1_scr, sem,
                     *, col0, n_grp):
    i = pl.program_id(0)
    slot = lax.rem(i, 2)
    g = CMP_GROUP

    def fetch(grp, to_slot, k, slab):
        row0 = pl.multiple_of(blk_ref[grp * g + k] * CMP_BLOCK, CMP_BLOCK)
        return pltpu.make_async_copy(
            src_hbm.at[pl.ds(row0, CMP_BLOCK), pl.ds(col0 + slab * HEAD_DIM, HEAD_DIM)],
            buf.at[to_slot, slab, pl.ds(k * CMP_PITCH, CMP_BLOCK), :],
            sem.at[to_slot])

    def fetch_group(grp, to_slot, wait):
        for k in range(g):
            for slab in range(N_KV_SLABS):
                cp = fetch(grp, to_slot, k, slab)
                cp.wait() if wait else cp.start()

    @pl.when(i == 0)
    def _():
        fetch_group(0, 0, False)
        for s in range(2):
            pe_rows = jnp.broadcast_to(pe_ref[s:s + 1, :], (SUBLANES, CMP_BLOCK * HEAD_DIM)).astype(BF16)
            c1_scr[s:s + 1, :] = jnp.dot(pe_rows, w1_ref[s], preferred_element_type=F32)[0:1, :] + b1_ref[s:s + 1, :]

    @pl.when(i + 1 < n_grp)
    def _():
        fetch_group(i + 1, 1 - slot, False)

    fetch_group(i, slot, True)
    for s in range(2):
        for h in range(NSA_KV_HEADS):
            view = buf.at[slot, s * NSA_KV_HEADS + h]
            for r in range(CMP_BLOCK):
                flat[h * g:(h + 1) * g, r * HEAD_DIM:(r + 1) * HEAD_DIM] = (
                    view[pl.ds(r, g, stride=CMP_PITCH), :].astype(BF16))
        acc = jnp.dot(flat[...], w1_ref[s], preferred_element_type=F32)
        hid = _silu(acc + c1_scr[s:s + 1, :]).astype(BF16)
        out = jnp.dot(hid, w2_ref[s], preferred_element_type=F32)
        for h in range(NSA_KV_HEADS):
            c = (s * NSA_KV_HEADS + h) * HEAD_DIM
            o_ref[:, c:c + HEAD_DIM] = out[h * g:(h + 1) * g, :]


def _compress(src2d, col0, blk_rows, cmp_pe, cmp_w1, cmp_b1, cmp_w2):
    n_blocks = blk_rows.shape[0]
    n_grp = n_blocks // CMP_GROUP
    pe_flat = cmp_pe.transpose(1, 0, 2).reshape(2, CMP_BLOCK * HEAD_DIM)
    return pl.pallas_call(
        functools.partial(_compress_kernel, col0=col0, n_grp=n_grp),
        grid_spec=pltpu.PrefetchScalarGridSpec(
            num_scalar_prefetch=1,
            grid=(n_grp,),
            in_specs=[pl.BlockSpec(memory_space=pl.ANY),
                      pl.BlockSpec((2, CMP_BLOCK * HEAD_DIM, HEAD_DIM), lambda i, br: (0, 0, 0)),
                      pl.BlockSpec((2, CMP_BLOCK * HEAD_DIM), lambda i, br: (0, 0)),
                      pl.BlockSpec((2, HEAD_DIM), lambda i, br: (0, 0)),
                      pl.BlockSpec((2, HEAD_DIM, HEAD_DIM), lambda i, br: (0, 0, 0))],
            out_specs=pl.BlockSpec((CMP_GROUP, KV_W), lambda i, br: (i, 0)),
            scratch_shapes=[pltpu.VMEM((2, N_KV_SLABS, CMP_GROUP * CMP_PITCH, HEAD_DIM), F32),
                            pltpu.VMEM((NSA_KV_HEADS * CMP_GROUP, CMP_BLOCK * HEAD_DIM), BF16),
                            pltpu.VMEM((2, HEAD_DIM), F32),
                            pltpu.SemaphoreType.DMA((2,))]),
        out_shape=jax.ShapeDtypeStruct((n_blocks, KV_W), F32),
        compiler_params=_cparams("arbitrary"),
        name="compress",
    )(blk_rows, src2d, cmp_w1.astype(BF16), pe_flat, cmp_b1, cmp_w2.astype(BF16))


CONV_PAD = SUBLANES
def _split_bf16(x, parts):
    out = []
    for _ in range(parts):
        piece = x.astype(BF16)
        out.append(piece)
        x = x - piece.astype(F32)
    return out


def _dot_hi(a, b):
    (ah, al), (bh, bl) = _split_bf16(a, 2), _split_bf16(b, 2)
    dot = functools.partial(jnp.dot, preferred_element_type=F32)
    return dot(ah, bh) + (dot(ah, bl) + dot(al, bh))


def _dot_mask(mask, x):
    m = mask.astype(BF16)
    return sum(jnp.dot(m, piece, preferred_element_type=F32) for piece in _split_bf16(x, 3))


def _unit_lower_inverse(lmat, c):
    eye = (lax.broadcasted_iota(I32, (c, c), 0) == lax.broadcasted_iota(I32, (c, c), 1)).astype(F32)
    x = eye - lmat
    p = lmat
    span = 2
    while span < c:
        p = _dot_hi(p, p)
        x = x + _dot_hi(x, p)
        span *= 2
    return x


def _gdn_kernel(qkv_ref, z_ref, sm_ref, cw_ref, alog_ref, dtb_ref, nw_ref, conv0_ref, s0_ref,
                o_ref, sout_ref, cout_ref, xbuf, qkvc, s_scr, *, tl, chunk, l_valid, nt):
    t = pl.program_id(1)
    n_t = nt
    tail = CONV_WIDTH - 1

    @pl.when(t == 0)
    def _():
        xbuf[CONV_PAD - tail:CONV_PAD, :] = conv0_ref[...]
        s_scr[...] = s0_ref[...]

    xbuf[CONV_PAD:CONV_PAD + tl, :] = qkv_ref[...]
    for cb in range(GDN_CONV_DIM // LANES):
        cols = slice(cb * LANES, (cb + 1) * LANES)
        y = xbuf[CONV_PAD - tail:CONV_PAD - tail + tl, cols] * cw_ref[0:1, cols]
        for j in range(1, CONV_WIDTH):
            y = y + xbuf[CONV_PAD - tail + j:CONV_PAD - tail + j + tl, cols] * cw_ref[j:j + 1, cols]
        qkvc[:, cols] = _silu(y)

    lv = l_valid - t * tl
    small = sm_ref[...]
    live = lax.broadcasted_iota(I32, (tl, LANES), 0) < lv
    beta = jnp.where(live, jax.nn.sigmoid(small), 0.0)
    g = jnp.where(live, -jnp.exp(alog_ref[...]) * jax.nn.softplus(small + dtb_ref[...]), 0.0)
    ri = lax.broadcasted_iota(I32, (tl, tl), 0)
    ci = lax.broadcasted_iota(I32, (tl, tl), 1)
    cshift = chunk.bit_length() - 1
    same = lax.shift_right_logical(ri, cshift) == lax.shift_right_logical(ci, cshift)
    gc = _dot_mask(jnp.where(same & (ci <= ri), 1.0, 0.0), g)
    gl = _dot_mask(jnp.where(same, 1.0, 0.0), g)
    gc_t = gc.T
    low = lax.broadcasted_iota(I32, (chunk, chunk), 0) >= lax.broadcasted_iota(I32, (chunk, chunk), 1)
    strict = lax.broadcasted_iota(I32, (chunk, chunk), 0) > lax.broadcasted_iota(I32, (chunk, chunk), 1)
    tb = (((1,), (1,)), ((), ()))

    states = [s_scr[h] for h in range(GDN_HEADS)]
    for c in range(tl // chunk):
        rows = slice(c * chunk, (c + 1) * chunk)
        for h in range(GDN_HEADS):
            hc = slice(h * HEAD_DIM, (h + 1) * HEAD_DIM)
            dc = SM_DECAY + h
            state = states[h]
            qh = qkvc[rows, h * HEAD_DIM:(h + 1) * HEAD_DIM]
            kh = qkvc[rows, GDN_QK + h * HEAD_DIM:GDN_QK + (h + 1) * HEAD_DIM]
            vh = qkvc[rows, 2 * GDN_QK + h * HEAD_DIM:2 * GDN_QK + (h + 1) * HEAD_DIM]
            qn = qh * lax.rsqrt(jnp.sum(qh * qh, axis=-1, keepdims=True) + NORM_EPS) * (HEAD_DIM ** -0.5)
            kn = kh * lax.rsqrt(jnp.sum(kh * kh, axis=-1, keepdims=True) + NORM_EPS)
            bcol = beta[rows, SM_BETA + h:SM_BETA + h + 1]
            gcol = gc[rows, dc:dc + 1]
            glcol = gl[rows, dc:dc + 1]
            grow = gc_t[dc:dc + 1, rows]
            decay = jnp.exp(jnp.where(low, gcol - grow, NEG_INF))
            kb = kn * bcol
            lmat = jnp.where(strict, lax.dot_general(kb, kn, tb, preferred_element_type=F32) * decay, 0.0)
            inv = _unit_lower_inverse(lmat, chunk)
            e_g = jnp.exp(gcol)
            u = _dot_hi(inv, vh * bcol)
            w = _dot_hi(inv, kb * e_g)
            qk = jnp.where(low, lax.dot_general(qn, kn, tb, preferred_element_type=F32) * decay, 0.0)
            qg = qn * e_g
            kg = kn * jnp.exp(glcol - gcol)
            v_new = u - jnp.dot(w, state, preferred_element_type=F32)
            o = jnp.dot(qg, state, preferred_element_type=F32) + jnp.dot(qk, v_new, preferred_element_type=F32)
            state = state * jnp.exp(glcol[0:1, :]) + lax.dot_general(kg, v_new, (((0,), (0,)), ((), ())),
                                                                     preferred_element_type=F32)
            o = o * lax.rsqrt(jnp.mean(o * o, axis=-1, keepdims=True) + NORM_EPS) * nw_ref[...]
            o_ref[rows, hc] = o * _silu(z_ref[rows, hc])
            states[h] = state
    for h in range(GDN_HEADS):
        s_scr[h] = states[h]

    @pl.when(t == n_t - 1)
    def _():
        sout_ref[...] = s_scr[...]
        last = l_valid - (nt - 1) * tl
        cout_ref[...] = xbuf[pl.ds(CONV_PAD - tail + last, tail), :]

    xbuf[CONV_PAD - tail:CONV_PAD, :] = xbuf[CONV_PAD - tail + tl:CONV_PAD + tl, :]


def _gdn(proj, b, l_pad, l_valid, conv0, s0, conv_w, a_log, dt_bias, norm_w, tl, chunk):
    nt = l_pad // tl
    lane_pad = lambda v: jnp.zeros((1, LANES), F32).at[0, SM_DECAY:SM_DECAY + GDN_HEADS].set(v)
    return pl.pallas_call(
        functools.partial(_gdn_kernel, tl=tl, chunk=chunk, l_valid=l_valid, nt=nt),
        grid=(b, nt),
        in_specs=[pl.BlockSpec((tl, GDN_CONV_DIM), lambda i, t: (i * nt + t, P_QKV // GDN_CONV_DIM)),
                  pl.BlockSpec((tl, GDN_QK), lambda i, t: (i * nt + t, P_Z // GDN_QK)),
                  pl.BlockSpec((tl, LANES), lambda i, t: (i * nt + t, P_SMALL // LANES)),
                  pl.BlockSpec((CONV_WIDTH, GDN_CONV_DIM), lambda i, t: (0, 0)),
                  pl.BlockSpec((1, LANES), lambda i, t: (0, 0)),
                  pl.BlockSpec((1, LANES), lambda i, t: (0, 0)),
                  pl.BlockSpec((1, HEAD_DIM), lambda i, t: (0, 0)),
                  pl.BlockSpec((None, CONV_WIDTH - 1, GDN_CONV_DIM), lambda i, t: (i, 0, 0)),
                  pl.BlockSpec((None, GDN_HEADS, HEAD_DIM, HEAD_DIM), lambda i, t: (i, 0, 0, 0))],
        out_specs=[pl.BlockSpec((tl, GDN_QK), lambda i, t: (i * nt + t, 0)),
                   pl.BlockSpec((None, GDN_HEADS, HEAD_DIM, HEAD_DIM), lambda i, t: (i, 0, 0, 0)),
                   pl.BlockSpec((None, CONV_WIDTH - 1, GDN_CONV_DIM), lambda i, t: (i, 0, 0))],
        out_shape=[jax.ShapeDtypeStruct((b * l_pad, GDN_QK), F32),
                   jax.ShapeDtypeStruct((b, GDN_HEADS, HEAD_DIM, HEAD_DIM), F32),
                   jax.ShapeDtypeStruct((b, CONV_WIDTH - 1, GDN_CONV_DIM), F32)],
        scratch_shapes=[pltpu.VMEM((CONV_PAD + tl, GDN_CONV_DIM), F32),
                        pltpu.VMEM((tl, GDN_CONV_DIM), F32),
                        pltpu.VMEM((GDN_HEADS, HEAD_DIM, HEAD_DIM), F32)],
        compiler_params=_cparams("parallel", "arbitrary"),
        name="gdn",
    )(proj, proj, proj, conv_w, lane_pad(a_log), lane_pad(dt_bias), norm_w[None, :], conv0, s0)


SEL_BLOCK = CMP_BLOCK
WIN_QBLOCK = 128
SEL_QBLOCK = 32
PAGE_SIZE = 128


def l2_normalize(x):
    return x * lax.rsqrt(jnp.sum(x * x, axis=-1, keepdims=True) + NORM_EPS)


def t5_bucket(dist):
    n = jnp.maximum(dist, 0)
    max_exact = REL_BUCKETS // 2
    nf = jnp.maximum(n, 1).astype(jnp.float32)
    large = max_exact + (jnp.log(nf / max_exact) / math.log(REL_MAX_DIST / max_exact)
                         * (REL_BUCKETS - max_exact)).astype(jnp.int32)
    return jnp.where(n < max_exact, n, jnp.minimum(large, REL_BUCKETS - 1))


def masked_probs(s, mask):
    s = jnp.where(mask, s.astype(jnp.float32), NEG_INF)
    return jax.nn.softmax(s, axis=-1) * mask


def short_conv(x, buf, w):
    L = x.shape[1]
    xp = jnp.concatenate([buf.astype(x.dtype), x], axis=1)
    y = sum(xp[:, j:j + L] * w[j] for j in range(CONV_WIDTH))
    return jax.nn.silu(y), xp[:, L:]


def gated_delta_chunked(q, k, v, g, beta, s0):
    B, H, L, dk = q.shape
    dv = v.shape[-1]
    C = math.gcd(L, GDN_CHUNK)
    n = L // C

    def chunks(t):
        return t.reshape(B, H, n, C, *t.shape[3:])

    q, k, v, g, beta = (chunks(t) for t in (q, k, v, g, beta))
    gc = jnp.cumsum(g, axis=-1)
    lower = jnp.tril(jnp.ones((C, C), bool))
    strict = jnp.tril(jnp.ones((C, C), bool), -1)
    decay = jnp.exp(jnp.where(lower, gc[..., :, None] - gc[..., None, :], NEG_INF))
    kb = k * beta[..., None]
    lmat = jnp.where(strict, jnp.einsum('bhncd,bhnjd->bhncj', kb, k) * decay, 0.0)
    rhs = jnp.concatenate([v * beta[..., None], kb * jnp.exp(gc)[..., None]], axis=-1)
    sol = lax.linalg.triangular_solve(lmat + jnp.eye(C, dtype=lmat.dtype), rhs,
                                      left_side=True, lower=True, unit_diagonal=True)
    u, w = sol[..., :dv], sol[..., dv:]
    qk = jnp.where(lower, jnp.einsum('bhncd,bhnjd->bhncj', q, k) * decay, 0.0)
    qg = q * jnp.exp(gc)[..., None]
    kg = k * jnp.exp(gc[..., -1:] - gc)[..., None]
    g_last = jnp.exp(gc[..., -1])

    def step(S, xs):
        u_i, w_i, qk_i, qg_i, kg_i, gl_i = xs
        v_new = u_i - jnp.einsum('bhcd,bhde->bhce', w_i, S)
        o = jnp.einsum('bhcd,bhde->bhce', qg_i, S) + jnp.einsum('bhcj,bhje->bhce', qk_i, v_new)
        S = S * gl_i[..., None, None] + jnp.einsum('bhcd,bhce->bhde', kg_i, v_new)
        return S, o

    xs = tuple(jnp.moveaxis(t, 2, 0) for t in (u, w, qk, qg, kg, g_last))
    S, o = lax.scan(step, s0, xs)
    return jnp.moveaxis(o, 0, 2).reshape(B, H, L, dv), S


def gdn_mixer(qkv, z, b_raw, a_raw, conv_buf, s0, conv_w, a_log, dt_bias, norm_w):
    B, L, _ = qkv.shape
    qkv_c, new_buf = short_conv(qkv, conv_buf, conv_w)
    qc, kc, vc = jnp.split(qkv_c, [GDN_QK, 2 * GDN_QK], axis=-1)

    def heads(t, d):
        return t.reshape(B, L, GDN_HEADS, d).transpose(0, 2, 1, 3).astype(jnp.float32)

    q = l2_normalize(heads(qc, HEAD_DIM)) * (HEAD_DIM ** -0.5)
    k = l2_normalize(heads(kc, HEAD_DIM))
    v = heads(vc, HEAD_DIM)
    beta = jax.nn.sigmoid(b_raw.astype(jnp.float32)).transpose(0, 2, 1)
    g = (-jnp.exp(a_log.astype(jnp.float32))
         * jax.nn.softplus(a_raw.astype(jnp.float32) + dt_bias.astype(jnp.float32))).transpose(0, 2, 1)
    o, s_new = gated_delta_chunked(q, k, v, g, beta, s0.astype(jnp.float32))
    o = o.transpose(0, 2, 1, 3)
    o = (o * lax.rsqrt(jnp.mean(o * o, axis=-1, keepdims=True) + NORM_EPS) * norm_w.astype(jnp.float32)
         * jax.nn.silu(z.reshape(B, L, GDN_HEADS, HEAD_DIM).astype(jnp.float32)))
    return o.reshape(B, L, GDN_HEADS * HEAD_DIM).astype(qkv.dtype), new_buf, s_new.astype(s0.dtype)


def compress_blocks(kv, pe, w1, b1, w2):
    B, Lk = kv.shape[:2]
    nb = Lk // CMP_BLOCK
    blk = kv[:, :nb * CMP_BLOCK].reshape(B, nb, CMP_BLOCK, 2, NSA_KV_HEADS, HEAD_DIM)
    blk = blk + pe[:, :, None, :]
    flat = blk.transpose(0, 1, 3, 4, 2, 5).reshape(B, nb, 2, NSA_KV_HEADS, CMP_BLOCK * HEAD_DIM)
    hid = jax.nn.silu(jnp.einsum('bnshf,sfe->bnshe', flat, w1) + b1[:, None, :])
    return jnp.einsum('bnshe,sed->bnshd', hid, w2)


def cmp_attend(q, qpos, kvc, rel_g):
    nb = kvc.shape[1]
    bend = jnp.arange(nb, dtype=jnp.int32) * CMP_BLOCK + (CMP_BLOCK - 1)
    dist = qpos[:, None] - bend[None, :]
    bias = rel_g[t5_bucket(dist)].transpose(2, 3, 0, 1)
    s = jnp.einsum('bhgqd,bnhd->bhgqn', q, kvc[:, :, 0]).astype(jnp.float32) * ATTN_SCALE + bias
    p = masked_probs(s, dist >= 0)
    o = jnp.einsum('bhgqn,bnhd->bhgqd', p, kvc[:, :, 1].astype(jnp.float32))
    return o, p


def select_blocks(p, qpos):
    score = jnp.sum(p, axis=2)
    B, Hkv, Q, nb = score.shape
    cur = qpos // SEL_BLOCK
    score = jnp.where(jnp.arange(nb)[None, :] < cur[:, None], score, -1.0)
    width = max(nb, SEL_TOPK - 1)
    score = jnp.pad(score, ((0, 0), (0, 0), (0, 0), (0, width - nb)), constant_values=-1.0)
    top_s, top_i = lax.top_k(score, SEL_TOPK - 1)
    cur_b = jnp.broadcast_to(cur[None, None, :, None], (B, Hkv, Q, 1)).astype(jnp.int32)
    idx = jnp.concatenate([cur_b, top_i.astype(jnp.int32)], axis=-1)
    valid = jnp.concatenate([jnp.ones((B, Hkv, Q, 1), bool), top_s >= 0], axis=-1)
    return idx, valid


def sel_attend(q, qpos, idx, valid, fetch, rel_g):
    B, Hkv, G, Q, dh = q.shape
    qc = math.gcd(Q, SEL_QBLOCK)
    nc = Q // qc
    qs = q.reshape(B, Hkv, G, nc, qc, dh).transpose(3, 0, 1, 2, 4, 5)
    ids = idx.reshape(B, Hkv, nc, qc, SEL_TOPK).transpose(2, 0, 1, 3, 4)
    vals = valid.reshape(B, Hkv, nc, qc, SEL_TOPK).transpose(2, 0, 1, 3, 4)
    ps = qpos.reshape(nc, qc)
    hidx = jnp.arange(Hkv)[None, :, None, None]
    offs = jnp.arange(SEL_BLOCK, dtype=jnp.int32)
    nkeys = SEL_TOPK * SEL_BLOCK

    def one(args):
        qb, ib, vb, pb = args
        kv = fetch(ib)
        kk = kv[..., 0, :].reshape(B, Hkv, qc, nkeys, dh)
        vv = kv[..., 1, :].reshape(B, Hkv, qc, nkeys, dh)
        kpos = (ib[..., None] * SEL_BLOCK + offs).reshape(B, Hkv, qc, nkeys)
        dist = pb[None, None, :, None] - kpos
        mask = jnp.broadcast_to(vb[..., None], (B, Hkv, qc, SEL_TOPK, SEL_BLOCK)).reshape(B, Hkv, qc, nkeys) & (dist >= 0)
        bias = rel_g[t5_bucket(dist), hidx].transpose(0, 1, 4, 2, 3)
        s = jnp.einsum('bhgqd,bhqkd->bhgqk', qb, kk).astype(jnp.float32) * ATTN_SCALE + bias
        p = masked_probs(s, mask[:, :, None])
        return jnp.einsum('bhgqk,bhqkd->bhgqd', p, vv.astype(jnp.float32))

    o = lax.map(one, (qs, ids, vals, ps))
    return o.transpose(1, 2, 3, 0, 4, 5).reshape(B, Hkv, G, Q, dh)


def win_attend(q, qpos, kv, kpos, rel_g):
    dist = qpos[:, None] - kpos[None, :]
    mask = (dist >= 0) & (dist < WINDOW) & (kpos[None, :] >= 0)
    bias = rel_g[t5_bucket(dist)].transpose(2, 3, 0, 1)
    s = jnp.einsum('bhgqd,bkhd->bhgqk', q, kv[:, :, 0]).astype(jnp.float32) * ATTN_SCALE + bias
    p = masked_probs(s, mask)
    return jnp.einsum('bhgqk,bkhd->bhgqd', p, kv[:, :, 1].astype(jnp.float32))


def win_attend_prompt(q, kv, rel_g):
    B, Hkv, G, L, dh = q.shape
    wq = math.gcd(L, WIN_QBLOCK)
    nq = L // wq
    kvp = jnp.pad(kv, ((0, 0), (WINDOW, 0), (0, 0), (0, 0), (0, 0)))
    qs = q.reshape(B, Hkv, G, nq, wq, dh).transpose(3, 0, 1, 2, 4, 5)

    def one(args):
        qb, i = args
        start = i * wq
        band = lax.dynamic_slice_in_dim(kvp, start, WINDOW + wq, axis=1)
        qpos = start + jnp.arange(wq, dtype=jnp.int32)
        kpos = start - WINDOW + jnp.arange(WINDOW + wq, dtype=jnp.int32)
        return win_attend(qb, qpos, band, kpos, rel_g)

    o = lax.map(one, (qs, jnp.arange(nq, dtype=jnp.int32)))
    return o.transpose(1, 2, 3, 0, 4, 5).reshape(B, Hkv, G, L, dh)


def make_nsa_prompt(rel_g, cmp_params):
    def attend(q, kv_cmp, kv_sel, kv_win):
        B, L = kv_cmp.shape[:2]
        qpos = jnp.arange(L, dtype=jnp.int32)
        o_cmp, p = cmp_attend(q, qpos, compress_blocks(kv_cmp, *cmp_params), rel_g)
        idx, valid = select_blocks(p, qpos)
        nblk = -(-L // SEL_BLOCK)
        store = jnp.pad(kv_sel, ((0, 0), (0, nblk * SEL_BLOCK - L), (0, 0), (0, 0), (0, 0)))
        bidx = jnp.arange(B)[:, None, None, None, None]
        hidx = jnp.arange(NSA_KV_HEADS)[None, :, None, None, None]
        offs = jnp.arange(SEL_BLOCK, dtype=jnp.int32)

        def fetch(ib):
            rows = jnp.clip(ib, 0, nblk - 1)[..., None] * SEL_BLOCK + offs
            return store[bidx, rows, :, hidx]

        o_sel = sel_attend(q, qpos, idx, valid, fetch, rel_g)
        o_win = win_attend_prompt(q, kv_win, rel_g)
        return o_cmp, o_sel, o_win, (kv_cmp, kv_sel, kv_win[:, L - min(WINDOW, L):])
    return attend


def make_nsa_sample(rel_g, cmp_params, pool_cmp, pool_sel, win_buf, page_table):
    def attend(q, kv_cmp, kv_sel, kv_win):
        Bd, L = kv_cmp.shape[:2]
        n_pages = page_table.shape[1]
        past = n_pages * PAGE_SIZE
        qpos = past + jnp.arange(L, dtype=jnp.int32)
        past_cmp = pool_cmp[page_table].reshape(Bd, past, 2, NSA_KV_HEADS, HEAD_DIM).astype(kv_cmp.dtype)
        kvc = jnp.concatenate([compress_blocks(past_cmp, *cmp_params),
                               compress_blocks(kv_cmp, *cmp_params)], axis=1)
        o_cmp, p = cmp_attend(q, qpos, kvc, rel_g)
        idx, valid = select_blocks(p, qpos)
        bpp = PAGE_SIZE // SEL_BLOCK
        n_past_blk = n_pages * bpp
        n_new_blk = -(-L // SEL_BLOCK)
        new_rows = jnp.pad(kv_sel, ((0, 0), (0, n_new_blk * SEL_BLOCK - L), (0, 0), (0, 0), (0, 0)))
        bidx = jnp.arange(Bd)[:, None, None, None, None]
        hidx = jnp.arange(NSA_KV_HEADS)[None, :, None, None, None]
        offs = jnp.arange(SEL_BLOCK, dtype=jnp.int32)

        def fetch(ib):
            ip = jnp.clip(ib, 0, n_past_blk - 1)
            phys = page_table[bidx[..., 0], ip // bpp][..., None]
            from_past = pool_sel[phys, (ip % bpp)[..., None] * SEL_BLOCK + offs, :, hidx]
            rows_new = jnp.clip(ib - n_past_blk, 0, n_new_blk - 1)[..., None] * SEL_BLOCK + offs
            from_new = new_rows[bidx, rows_new, :, hidx]
            return jnp.where((ib >= n_past_blk)[..., None, None, None], from_new, from_past.astype(from_new.dtype))

        o_sel = sel_attend(q, qpos, idx, valid, fetch, rel_g)
        wb = win_buf.shape[1]
        kw = jnp.concatenate([win_buf.astype(kv_win.dtype), kv_win], axis=1)
        kpos = past - wb + jnp.arange(wb + L, dtype=jnp.int32)
        o_win = win_attend(q, qpos, kw, kpos, rel_g)
        return o_cmp, o_sel, o_win, (kv_cmp, kv_sel, kw[:, L:])
    return attend


def _heads_to_rows(o):
    b, hkv, g, l, dh = o.shape
    return o.transpose(0, 3, 1, 2, 4).reshape(b * l, hkv * g * dh)


def _jax_mixers(proj, b, l, nsa_attend, conv_buf, s0, conv_w, a_log, dt_bias, norm_w):
    p3 = proj.reshape(b, l, P_DIM)
    qkv = p3[..., P_QKV:P_Z]
    z = p3[..., P_Z:P_Q]
    nsa_q = p3[..., P_Q:P_KV]
    nsa_kv = p3[..., P_KV:P_SMALL]
    small = p3[..., P_SMALL:P_SMALL + LANES]
    b_raw = small[..., SM_BETA:SM_BETA + GDN_HEADS]
    a_raw = small[..., SM_DECAY:SM_DECAY + GDN_HEADS]
    o_gdn, new_conv, new_s = gdn_mixer(qkv, z, b_raw, a_raw, conv_buf, s0, conv_w, a_log, dt_bias, norm_w)
    q = nsa_q.reshape(b, l, NSA_KV_HEADS, NSA_GROUP, HEAD_DIM).transpose(0, 2, 3, 1, 4)
    kv = nsa_kv.reshape(b, l, N_BRANCH, 2, NSA_KV_HEADS, HEAD_DIM)
    o_cmp, o_sel, o_win, nsa_state = nsa_attend(q, kv[:, :, 0], kv[:, :, 1], kv[:, :, 2])
    return (o_gdn.reshape(b * l, GDN_QK), _heads_to_rows(o_cmp), _heads_to_rows(o_sel), _heads_to_rows(o_win),
            nsa_state, new_s, new_conv)


def kernel(x_prompt, x_sample, cache_cmp_kv, cache_sel_kv, cache_win_kv, state_gdn, state_conv, page_table,
           c_prompt, c_sample, rel_bias, w_ada, b_ada, ln_mix_pre, ln_mix_post, ln_ffn_pre, ln_ffn_post,
           w_in, w_out, conv_w, gdn_a_log, gdn_dt_bias, gdn_norm, cmp_pe, cmp_w1, cmp_b1, cmp_w2,
           w_router, router_bias, w_exp_gu, w_exp_down, w_sh_gu, w_sh_down):
    bp, lp, _ = x_prompt.shape
    bs, ls, _ = x_sample.shape
    xp = x_prompt.reshape(bp * lp, D_MODEL)
    xs = x_sample.reshape(bs * ls, D_MODEL)
    mod = _ada(jnp.concatenate([c_prompt, c_sample], axis=0), w_ada.reshape(w_ada.shape[1:]), b_ada[0])
    sh1, sc1, gt1, sh2, sc2, gt2 = jnp.split(mod, 6, axis=1)
    w_in_p = _pack_w_in(w_in[0])
    proj_p = _inproj(xp, ln_mix_pre[0], sc1[:bp], sh1[:bp], w_in_p, lp, 512)
    proj_s = _inproj(xs, ln_mix_pre[0], sc1[bp:], sh1[bp:], w_in_p, ls, bs * ls)
    cmp_params = (cmp_pe[0], cmp_w1[0], cmp_b1[0], cmp_w2[0])
    gdn_params = (conv_w[0], gdn_a_log[0], gdn_dt_bias[0], gdn_norm[0])
    rel_bias = rel_bias.astype(F32)
    proj3_p = proj_p.reshape(bp, lp, P_DIM)
    proj3_s = proj_s.reshape(bs, ls, P_DIM)
    assert ls < CMP_BLOCK and ls <= QPAD, "the sample step adds less than one compressed block"

    conv0 = jnp.zeros((bp, CONV_WIDTH - 1, GDN_CONV_DIM), state_conv.dtype)
    s00 = jnp.zeros((bp, GDN_HEADS, HEAD_DIM, HEAD_DIM), state_gdn.dtype)
    o_gdn_p, gdn_p, conv_p = _gdn(proj_p, bp, lp, lp, conv0, s00, *gdn_params, 2 * GDN_CHUNK, GDN_CHUNK)
    nb_p = lp // CMP_BLOCK
    kvc_p = _compress(proj_p, P_KV, jnp.arange(bp * nb_p, dtype=I32), *cmp_params).reshape(bp, nb_p, KV_W)
    o_cmp_p, neg_p, _ = _cmp_select(proj3_p, P_Q // (NSA_GROUP * HEAD_DIM), kvc_p, rel_bias, ATT_T, 0)
    nsub = ATT_T // ATT_SUB
    d_min = 1 - nsub
    t_sel = _bias_tiles(rel_bias, d_min, (lp // ATT_T + 1) * nsub - 1, 0, 1 << 30)
    t_win = _bias_tiles(rel_bias, d_min, 3 * nsub - 1, 0, WINDOW)
    o_sel_p = _flash(proj_p, neg_p.reshape(NSA_KV_HEADS, bp * lp, nb_p), t_sel, bp, lp, 1, d_min)
    o_win_p = _flash(proj_p, None, t_win, bp, lp, 2, d_min)

    n_pages = page_table.shape[1]
    past = n_pages * PAGE_ROWS
    pool_rows = cache_cmp_kv.shape[1] * PAGE_ROWS
    halves = PAGE_ROWS // CMP_BLOCK
    blk_s = (page_table[..., None] * halves + jnp.arange(halves, dtype=I32)).reshape(-1)
    assert cache_cmp_kv.shape[0] == 1, "one decoder layer"
    kvc_s = _compress(cache_cmp_kv.reshape(pool_rows, KV_W), 0, blk_s, *cmp_params)
    kvc_s = kvc_s.reshape(bs, n_pages * halves, KV_W)
    q_pad = jnp.pad(proj3_s[..., P_Q:P_KV], ((0, 0), (0, QPAD - ls), (0, 0)))
    o_cmp_s, _, picks = _cmp_select(q_pad, 0, kvc_s, rel_bias, QPAD, past)
    o_sel_s = _sel_sample(picks, page_table, rel_bias, q_pad, proj3_s, cache_sel_kv.reshape(pool_rows, KV_W),
                          past, ls)
    wb = cache_win_kv.shape[2]
    o_win_s, win_roll = _win_sample(rel_bias, q_pad, proj3_s, cache_win_kv.reshape(bs, wb, KV_W), past, ls)
    proj_s_pad = jnp.pad(proj3_s, ((0, 0), (0, QPAD - ls), (0, 0))).reshape(bs * QPAD, P_DIM)
    o_gdn_s, gdn_s, conv_s = _gdn(proj_s_pad, bs, QPAD, ls, state_conv.reshape(bs, CONV_WIDTH - 1, GDN_CONV_DIM),
                                  state_gdn.reshape(bs, GDN_HEADS, HEAD_DIM, HEAD_DIM), *gdn_params, QPAD, QPAD)
    cut = lambda o: o.reshape(bs, QPAD, -1)[:, :ls].reshape(bs * ls, -1)

    w_out_b = w_out[0].astype(BF16)
    x1_p = _outproj(o_gdn_p, o_cmp_p.reshape(bp * lp, -1), o_sel_p, o_win_p, proj_p, xp, w_out_b, ln_mix_post[0],
                    gt1[:bp], lp, 256)
    x1_s = _outproj(cut(o_gdn_s), cut(o_cmp_s), cut(o_sel_s), cut(o_win_s), proj_s, xs, w_out_b, ln_mix_post[0],
                    gt1[bp:], ls, bs * ls)
    y_p, y_s = _moe(x1_p, x1_s, (sc2[:bp], sh2[:bp], gt2[:bp]), (sc2[bp:], sh2[bp:], gt2[bp:]), lp, ls,
                    ln_ffn_pre[0], ln_ffn_post[0], w_router[0], router_bias[0],
                    w_exp_gu.reshape(w_exp_gu.shape[1:]), w_exp_down.reshape(w_exp_down.shape[1:]),
                    w_sh_gu[0].astype(BF16), w_sh_down[0].astype(BF16))

    kv_shape = (2, NSA_KV_HEADS, HEAD_DIM)
    branch = lambda p3, br: p3[..., P_KV + br * KV_W:P_KV + (br + 1) * KV_W]
    win_p = branch(proj3_p, 2)[:, lp - min(WINDOW, lp):]
    return (y_p.reshape(x_prompt.shape), y_s.reshape(x_sample.shape),
            branch(proj3_p, 0).reshape(1, bp, lp, *kv_shape), branch(proj3_s, 0).reshape(1, bs, ls, *kv_shape),
            branch(proj3_p, 1).reshape(1, bp, lp, *kv_shape), branch(proj3_s, 1).reshape(1, bs, ls, *kv_shape),
            win_p.reshape(1, bp, win_p.shape[1], *kv_shape), win_roll.reshape(1, bs, wb, *kv_shape),
            gdn_p[None].astype(state_gdn.dtype), gdn_s[None].astype(state_gdn.dtype),
            conv_p[None].astype(state_conv.dtype), conv_s[None].astype(state_conv.dtype))
```

```python
import functools
import math

import jax
import jax.numpy as jnp
from jax import lax
from jax.experimental import pallas as pl
from jax.experimental.pallas import tpu as pltpu

F32, BF16, I32 = jnp.float32, jnp.bfloat16, jnp.int32

D_MODEL = 2048
HEAD_DIM = 128
LANES = 128
SUBLANES = 8
ROW_SLABS = D_MODEL // LANES
GDN_HEADS = 8
GDN_QK = GDN_HEADS * HEAD_DIM
GDN_CONV_DIM = 3 * GDN_QK
CONV_WIDTH = 4
GDN_CHUNK = 64
NSA_HEADS = 8
NSA_KV_HEADS = 2
NSA_GROUP = NSA_HEADS // NSA_KV_HEADS
N_BRANCH = 3
CMP_BLOCK = 64
CMP_SHIFT = CMP_BLOCK.bit_length() - 1
SEL_TOPK = 16
WINDOW = 512
REL_BUCKETS = 32
REL_MAX_DIST = 8192
MOE_TOPK = 8
N_GROUPS = 8
TOPK_GROUPS = 4
ROUTED_SCALE = 2.5
NORM_EPS = 1e-6
NEG_INF = -1e30
ATTN_SCALE = HEAD_DIM ** -0.5
KV_W = 2 * NSA_KV_HEADS * HEAD_DIM

P_QKV = 0
P_Z = P_QKV + GDN_CONV_DIM
P_Q = P_Z + GDN_QK
P_KV = P_Q + NSA_HEADS * HEAD_DIM
P_SMALL = P_KV + N_BRANCH * KV_W
P_DIM = 7168
SM_BETA, SM_DECAY, SM_GATE = 0, GDN_HEADS, 2 * GDN_HEADS
IN_SIZES = (GDN_CONV_DIM, GDN_QK, GDN_HEADS, GDN_HEADS, NSA_HEADS * HEAD_DIM, N_BRANCH * KV_W, N_BRANCH * NSA_HEADS)

VMEM_LIMIT = 56 * 1024 * 1024
ROW_CHUNK = 32
MOE_MB = 256
ROW_DMA_PRIORITY = 1


def _cparams(*sem):
    return pltpu.CompilerParams(dimension_semantics=sem, vmem_limit_bytes=VMEM_LIMIT)


def _silu(x):
    return x * jax.nn.sigmoid(x)


def _row_chunks(n_rows, body):
    def step(i, carry):
        body(pl.multiple_of(i * ROW_CHUNK, ROW_CHUNK))
        return carry
    lax.fori_loop(0, n_rows // ROW_CHUNK, step, 0)


def _rms(x):
    return x * lax.rsqrt(jnp.mean(x * x, axis=-1, keepdims=True) + NORM_EPS)


def _mod_rows(ref, r0, per_row):
    return ref[pl.ds(r0, ROW_CHUNK), :] if per_row else ref[...]


def _mod_operand(mod, tm, rows_per_batch):
    if rows_per_batch % tm == 0:
        per = rows_per_batch // tm
        return (mod[:, None, :], pl.BlockSpec((None, 1, D_MODEL), lambda i, *_: (i // per, 0, 0)), False)
    return (jnp.repeat(mod, rows_per_batch, axis=0), pl.BlockSpec((tm, D_MODEL), lambda i, *_: (i, 0)), True)


def _ada_kernel(c_ref, w_ref, b_ref, o_ref):
    a = _silu(c_ref[...]).astype(BF16)
    o_ref[...] = jnp.dot(a, w_ref[...].astype(BF16), preferred_element_type=F32) + b_ref[...]


def _ada(c, w_ada, b_ada):
    n = c.shape[0]
    npad = -(-n // SUBLANES) * SUBLANES
    cp = jnp.pad(c, ((0, npad - n), (0, 0)))
    tn = 512
    out = pl.pallas_call(
        _ada_kernel,
        grid=(w_ada.shape[1] // tn,),
        in_specs=[pl.BlockSpec((npad, D_MODEL), lambda j: (0, 0)),
                  pl.BlockSpec((D_MODEL, tn), lambda j: (0, j)),
                  pl.BlockSpec((1, tn), lambda j: (0, j))],
        out_specs=pl.BlockSpec((npad, tn), lambda j: (0, j)),
        out_shape=jax.ShapeDtypeStruct((npad, w_ada.shape[1]), F32),
        compiler_params=_cparams("parallel"),
        name="ada",
    )(cp, w_ada, b_ada[None, :])
    return out[:n]


def _inproj_kernel(x_ref, g_ref, sc_ref, sh_ref, w_ref, o_ref, h_scr, *, tm, per_row):
    @pl.when(pl.program_id(1) == 0)
    def _():
        def body(r0):
            y = _rms(x_ref[pl.ds(r0, ROW_CHUNK), :]) * g_ref[...]
            h = y * (1.0 + _mod_rows(sc_ref, r0, per_row)) + _mod_rows(sh_ref, r0, per_row)
            h_scr[pl.ds(r0, ROW_CHUNK), :] = h.astype(BF16)
        _row_chunks(tm, body)
    o_ref[...] = jnp.dot(h_scr[...], w_ref[...], preferred_element_type=F32)


def _inproj(x, ln_g, sc, sh, w_in_p, rows_per_batch, tm):
    rows = x.shape[0]
    tn = 1024
    sc_a, sc_spec, per_row = _mod_operand(sc, tm, rows_per_batch)
    sh_a, sh_spec, _ = _mod_operand(sh, tm, rows_per_batch)
    return pl.pallas_call(
        functools.partial(_inproj_kernel, tm=tm, per_row=per_row),
        grid=(rows // tm, P_DIM // tn),
        in_specs=[pl.BlockSpec((tm, D_MODEL), lambda i, j: (i, 0)),
                  pl.BlockSpec((1, D_MODEL), lambda i, j: (0, 0)),
                  sc_spec, sh_spec,
                  pl.BlockSpec((D_MODEL, tn), lambda i, j: (0, j))],
        out_specs=pl.BlockSpec((tm, tn), lambda i, j: (i, j)),
        out_shape=jax.ShapeDtypeStruct((rows, P_DIM), F32),
        scratch_shapes=[pltpu.VMEM((tm, D_MODEL), BF16)],
        compiler_params=_cparams("parallel", "arbitrary"),
        name="inproj",
    )(x, ln_g[None, :], sc_a, sh_a, w_in_p)


def _pack_w_in(w_in):
    parts = jnp.split(w_in, list(np_cumsum(IN_SIZES)[:-1]), axis=1)
    qkv, z, b_raw, a_raw, nsa_q, nsa_kv, nsa_g = parts
    small = jnp.concatenate([b_raw, a_raw, nsa_g], axis=1)
    w = jnp.concatenate([qkv, z, nsa_q, nsa_kv, small], axis=1)
    return jnp.pad(w, ((0, 0), (0, P_DIM - w.shape[1]))).astype(BF16)


def np_cumsum(sizes):
    out, acc = [], 0
    for s in sizes:
        acc += s
        out.append(acc)
    return out


def _outproj_kernel(og_ref, oc_ref, os_ref, ow_ref, sm_ref, x_ref, w_ref, g_ref, gt_ref, o_ref,
                    mix_in, mix_out, *, tm, per_row):
    def build(r0):
        rows = pl.ds(r0, ROW_CHUNK)
        mix_in[rows, :GDN_QK] = og_ref[rows, :].astype(BF16)
        gates = jax.nn.sigmoid(sm_ref[rows, :])
        for hd in range(NSA_HEADS):
            cols = slice(hd * HEAD_DIM, (hd + 1) * HEAD_DIM)
            acc = None
            for br, ref in enumerate((oc_ref, os_ref, ow_ref)):
                c = SM_GATE + br * NSA_HEADS + hd
                term = gates[:, c:c + 1] * ref[rows, cols]
                acc = term if acc is None else acc + term
            mix_in[rows, GDN_QK + hd * HEAD_DIM:GDN_QK + (hd + 1) * HEAD_DIM] = acc.astype(BF16)
    _row_chunks(tm, build)
    mix_out[...] = jnp.dot(mix_in[...], w_ref[...], preferred_element_type=F32)

    def finish(r0):
        rows = pl.ds(r0, ROW_CHUNK)
        y = _rms(mix_out[rows, :]) * g_ref[...]
        o_ref[rows, :] = x_ref[rows, :] + _mod_rows(gt_ref, r0, per_row) * y
    _row_chunks(tm, finish)


def _outproj(o_gdn, o_cmp, o_sel, o_win, proj, x, w_out_b, ln_g, gt, rows_per_batch, tm):
    rows = x.shape[0]
    gt_a, gt_spec, per_row = _mod_operand(gt, tm, rows_per_batch)
    head_spec = pl.BlockSpec((tm, GDN_QK), lambda i: (i, 0))
    row_spec = pl.BlockSpec((tm, D_MODEL), lambda i: (i, 0))
    return pl.pallas_call(
        functools.partial(_outproj_kernel, tm=tm, per_row=per_row),
        grid=(rows // tm,),
        in_specs=[head_spec, head_spec, head_spec, head_spec,
                  pl.BlockSpec((tm, LANES), lambda i: (i, P_SMALL // LANES)),
                  row_spec,
                  pl.BlockSpec((D_MODEL, D_MODEL), lambda i: (0, 0)),
                  pl.BlockSpec((1, D_MODEL), lambda i: (0, 0)),
                  gt_spec],
        out_specs=row_spec,
        out_shape=jax.ShapeDtypeStruct((rows, D_MODEL), F32),
        scratch_shapes=[pltpu.VMEM((tm, D_MODEL), BF16), pltpu.VMEM((tm, D_MODEL), F32)],
        compiler_params=_cparams("parallel"),
        name="outproj",
    )(o_gdn, o_cmp, o_sel, o_win, proj, x, w_out_b, ln_g[None, :], gt_a)


def _route_kernel(x_ref, g_ref, sc_ref, sh_ref, wr_ref, rb_ref, h_ref, ei_ref, ew_ref, cnt_ref, h_scr,
                  *, tm, per_row, n_exp):
    def body(r0):
        rows = pl.ds(r0, ROW_CHUNK)
        y = _rms(x_ref[rows, :]) * g_ref[...]
        h = y * (1.0 + _mod_rows(sc_ref, r0, per_row)) + _mod_rows(sh_ref, r0, per_row)
        h_scr[rows, :] = h
        for s in range(ROW_SLABS):
            h_ref[pl.ds(r0 * ROW_SLABS + s, ROW_CHUNK, stride=ROW_SLABS), :] = h[:, s * LANES:(s + 1) * LANES]
    _row_chunks(tm, body)

    logits = lax.dot_general(wr_ref[...], h_scr[...], (((1,), (1,)), ((), ())),
                             precision=lax.Precision.HIGHEST, preferred_element_type=F32)
    s = jax.nn.sigmoid(logits)
    sb = s + rb_ref[...]
    gsz = n_exp // N_GROUPS
    sb3 = sb.reshape(N_GROUPS, gsz, tm)
    m1 = jnp.max(sb3, axis=1)
    n_top = jnp.sum((sb3 == m1[:, None, :]).astype(F32), axis=1)
    m2 = jnp.max(jnp.where(sb3 < m1[:, None, :], sb3, -jnp.inf), axis=1)
    gscore = m1 + jnp.where(n_top >= 2.0, m1, m2)
    gid = lax.broadcasted_iota(I32, (N_GROUPS, tm), 0)
    rank = jnp.zeros((N_GROUPS, tm), F32)
    for g in range(N_GROUPS):
        row = gscore[g:g + 1, :]
        ahead = (row > gscore) | ((row == gscore) & (g < gid))
        rank = rank + ahead.astype(F32)
    gsel = rank < float(TOPK_GROUPS)
    emask = jnp.broadcast_to(gsel[:, None, :], (N_GROUPS, gsz, tm)).reshape(n_exp, tm)
    v = jnp.where(emask, sb, NEG_INF)
    eid = lax.broadcasted_iota(I32, (n_exp, tm), 0)
    idxs, wts = [], []
    taken = jnp.zeros((n_exp, tm), F32)
    for _ in range(MOE_TOPK):
        m = jnp.max(v, axis=0, keepdims=True)
        idx = jnp.min(jnp.where(v == m, eid, n_exp), axis=0, keepdims=True)
        hit = eid == idx
        wts.append(jnp.sum(jnp.where(hit, s, 0.0), axis=0, keepdims=True))
        idxs.append(idx)
        taken = taken + hit.astype(F32)
        v = jnp.where(hit, -jnp.inf, v)
    w = jnp.concatenate(wts, axis=0)
    ei_ref[...] = jnp.concatenate(idxs, axis=0)
    ew_ref[...] = w / jnp.sum(w, axis=0, keepdims=True) * ROUTED_SCALE
    cnt_ref[...] = jnp.sum(taken, axis=1, keepdims=True)


def _route(x1, ln_g, sc, sh, w_router, router_bias, rows_per_batch, tm):
    rows = x1.shape[0]
    n_exp = w_router.shape[0]
    sc_a, sc_spec, per_row = _mod_operand(sc, tm, rows_per_batch)
    sh_a, sh_spec, _ = _mod_operand(sh, tm, rows_per_batch)
    return pl.pallas_call(
        functools.partial(_route_kernel, tm=tm, per_row=per_row, n_exp=n_exp),
        grid=(rows // tm,),
        in_specs=[pl.BlockSpec((tm, D_MODEL), lambda i: (i, 0)),
                  pl.BlockSpec((1, D_MODEL), lambda i: (0, 0)),
                  sc_spec, sh_spec,
                  pl.BlockSpec((n_exp, D_MODEL), lambda i: (0, 0)),
                  pl.BlockSpec((n_exp, 1), lambda i: (0, 0))],
        out_specs=[pl.BlockSpec((tm * ROW_SLABS, LANES), lambda i: (i, 0)),
                   pl.BlockSpec((MOE_TOPK, tm), lambda i: (0, i)),
                   pl.BlockSpec((MOE_TOPK, tm), lambda i: (0, i)),
                   pl.BlockSpec((None, n_exp, 1), lambda i: (i, 0, 0))],
        out_shape=[jax.ShapeDtypeStruct((rows * ROW_SLABS, LANES), F32),
                   jax.ShapeDtypeStruct((MOE_TOPK, rows), I32),
                   jax.ShapeDtypeStruct((MOE_TOPK, rows), F32),
                   jax.ShapeDtypeStruct((rows // tm, n_exp, 1), F32)],
        scratch_shapes=[pltpu.VMEM((tm, D_MODEL), F32)],
        compiler_params=_cparams("parallel"),
        name="route",
    )(x1, ln_g[None, :], sc_a, sh_a, w_router, router_bias[:, None])


def _expert_kernel(blk_e_ref, nused_ref, src_ref, nxt_ref, dst_ref, rw_ref, wgu_ref, wd_ref, h_hbm,
                   y_hbm, xbuf, xmat, ybuf, gsem, ssem, *, mb, ff, dump0):
    i = pl.program_id(0)
    nused = nused_ref[0]
    slot = lax.rem(i, 2)
    slab_rows = mb * ROW_SLABS

    def gather(idx_ref, to_slot, r):
        tok = idx_ref[0, 0, r]
        return pltpu.make_async_copy(
            h_hbm.at[pl.ds(pl.multiple_of(tok * ROW_SLABS, ROW_SLABS), ROW_SLABS), :],
            xbuf.at[pl.ds(pl.multiple_of(to_slot * slab_rows + r * ROW_SLABS, ROW_SLABS), ROW_SLABS), :],
            gsem.at[to_slot])

    def scatter(r):
        row = dst_ref[0, 0, r]
        return pltpu.make_async_copy(
            ybuf.at[pl.ds(r * ROW_SLABS, ROW_SLABS), :],
            y_hbm.at[pl.ds(pl.multiple_of(row * ROW_SLABS, ROW_SLABS), ROW_SLABS), :],
            ssem.at[0])

    @pl.when(i == 0)
    def _():
        for r in range(mb):
            gather(src_ref, 0, r).start(priority=ROW_DMA_PRIORITY)
        ybuf[...] = jnp.zeros_like(ybuf)
        fill = pltpu.make_async_copy(ybuf, y_hbm.at[pl.ds(dump0 * ROW_SLABS, slab_rows), :], ssem.at[0])
        fill.start()
        fill.wait()

    @pl.when(i < nused)
    def _():
        @pl.when(i + 1 < nused)
        def _():
            for r in range(mb):
                gather(nxt_ref, 1 - slot, r).start(priority=ROW_DMA_PRIORITY)

        for r in range(mb):
            gather(src_ref, slot, r).wait()
        base = pl.multiple_of(slot * slab_rows, slab_rows)
        for s in range(ROW_SLABS):
            xmat[:, s * LANES:(s + 1) * LANES] = xbuf[pl.ds(base + s, mb, stride=ROW_SLABS), :].astype(BF16)
        gu = jnp.dot(xmat[...], wgu_ref[...].astype(BF16), preferred_element_type=F32)
        hid = (_silu(gu[:, :ff]) * gu[:, ff:]).astype(BF16)
        y = jnp.dot(hid, wd_ref[...].astype(BF16), preferred_element_type=F32) * rw_ref[...]

        @pl.when(i > 0)
        def _():
            for r in range(mb):
                scatter(r).wait()
        for s in range(ROW_SLABS):
            ybuf[pl.ds(s, mb, stride=ROW_SLABS), :] = y[:, s * LANES:(s + 1) * LANES]
        for r in range(mb):
            scatter(r).start(priority=ROW_DMA_PRIORITY)

        @pl.when(i == nused - 1)
        def _():
            for r in range(mb):
                scatter(r).wait()


def _dispatch_plan(eidx, ew, counts, n_exp, mb):
    t_all = eidx.shape[0]
    n_asg = t_all * MOE_TOPK
    n_blk = -(-(n_asg + n_exp * (mb - 1)) // mb)
    n_slot = n_blk * mb
    plane_rows = t_all
    pad = (-counts) % mb
    asg = jnp.arange(n_asg, dtype=I32)
    last_key = 2 * n_exp
    pad_key = jnp.where(jnp.arange(mb - 1, dtype=I32)[None, :] < pad[:, None],
                        2 * jnp.arange(n_exp, dtype=I32)[:, None] + 1, last_key).reshape(-1)
    n_fill = n_slot - n_asg
    keys = jnp.concatenate([2 * eidx.reshape(-1), pad_key, jnp.full((n_fill - pad_key.shape[0],), last_key, I32)])
    src = jnp.concatenate([asg // MOE_TOPK, jnp.zeros((n_fill,), I32)])
    dst = jnp.concatenate([(asg % MOE_TOPK) * plane_rows + asg // MOE_TOPK, jnp.full((n_fill,), -1, I32)])
    wts = jnp.concatenate([ew.reshape(-1), jnp.zeros((n_fill,), F32)])
    _, rows_src, rows_dst, rows_w = lax.sort((keys, src, dst, wts), num_keys=1, is_stable=True)
    slot_id = jnp.arange(n_slot, dtype=I32)
    rows_dst = jnp.where(rows_dst < 0, MOE_TOPK * plane_rows + slot_id % mb, rows_dst)
    pends = jnp.cumsum(counts + pad)
    blk_e = jnp.minimum(jnp.searchsorted(pends, jnp.arange(n_blk, dtype=I32) * mb, side='right'),
                        n_exp - 1).astype(I32)
    nused = (pends[-1] // mb).astype(I32).reshape(1)
    return n_blk, plane_rows, rows_src, rows_dst, rows_w, blk_e, nused


def _experts(h_slabs, eidx, ew, counts, w_gu, w_down):
    n_exp, _, ff2 = w_gu.shape
    ff = ff2 // 2
    mb = MOE_MB
    n_blk, plane_rows, rows_src, rows_dst, rows_w, blk_e, nused = _dispatch_plan(eidx, ew, counts, n_exp, mb)
    idx_spec = lambda f: pl.BlockSpec((1, 1, mb), f, memory_space=pltpu.SMEM)
    y = pl.pallas_call(
        functools.partial(_expert_kernel, mb=mb, ff=ff, dump0=MOE_TOPK * plane_rows),
        grid_spec=pltpu.PrefetchScalarGridSpec(
            num_scalar_prefetch=2,
            grid=(n_blk,),
            in_specs=[idx_spec(lambda i, be, nu: (i, 0, 0)),
                      idx_spec(lambda i, be, nu: (jnp.minimum(i + 1, n_blk - 1), 0, 0)),
                      idx_spec(lambda i, be, nu: (i, 0, 0)),
                      pl.BlockSpec((mb, 1), lambda i, be, nu: (i, 0)),
                      pl.BlockSpec((None, D_MODEL, ff2), lambda i, be, nu: (be[i], 0, 0)),
                      pl.BlockSpec((None, ff, D_MODEL), lambda i, be, nu: (be[i], 0, 0)),
                      pl.BlockSpec(memory_space=pl.ANY)],
            out_specs=pl.BlockSpec(memory_space=pl.ANY),
            scratch_shapes=[pltpu.VMEM((2 * mb * ROW_SLABS, LANES), F32),
                            pltpu.VMEM((mb, D_MODEL), BF16),
                            pltpu.VMEM((mb * ROW_SLABS, LANES), F32),
                            pltpu.SemaphoreType.DMA((2,)),
                            pltpu.SemaphoreType.DMA((1,))]),
        out_shape=jax.ShapeDtypeStruct(((MOE_TOPK * plane_rows + mb) * ROW_SLABS, LANES), F32),
        compiler_params=_cparams("arbitrary"),
        name="experts",
    )(blk_e, nused, rows_src.reshape(n_blk, 1, mb), rows_src.reshape(n_blk, 1, mb),
      rows_dst.reshape(n_blk, 1, mb), rows_w[:, None], w_gu, w_down, h_slabs)
    return y


def _combine_kernel(*refs, tm, per_row, ff):
    y_refs = refs[:MOE_TOPK]
    h_ref, wgu_ref, wd_ref, x_ref, g_ref, gt_ref, o_ref, fsum, hmat, f_scr = refs[MOE_TOPK:]

    def add_planes(r0):
        rows = pl.ds(r0 * ROW_SLABS, ROW_CHUNK * ROW_SLABS)
        acc = y_refs[0][rows, :]
        for y_ref in y_refs[1:]:
            acc = acc + y_ref[rows, :]
        fsum[rows, :] = acc
    _row_chunks(tm, add_planes)
    for s in range(ROW_SLABS):
        hmat[:, s * LANES:(s + 1) * LANES] = h_ref[pl.ds(s, tm, stride=ROW_SLABS), :].astype(BF16)
    gu = jnp.dot(hmat[...], wgu_ref[...], preferred_element_type=F32)
    hid = (_silu(gu[:, :ff]) * gu[:, ff:]).astype(BF16)
    f_scr[...] = jnp.dot(hid, wd_ref[...], preferred_element_type=F32)
    for s in range(ROW_SLABS):
        f_scr[:, s * LANES:(s + 1) * LANES] += fsum[pl.ds(s, tm, stride=ROW_SLABS), :]

    def finish(r0):
        rows = pl.ds(r0, ROW_CHUNK)
        y = _rms(f_scr[rows, :]) * g_ref[...]
        o_ref[rows, :] = x_ref[rows, :] + _mod_rows(gt_ref, r0, per_row) * y
    _row_chunks(tm, finish)


def _combine(y_planes, h_slabs, row0, w_sh_gu_b, w_sh_down_b, x1, ln_g, gt, rows_per_batch, tm):
    rows = x1.shape[0]
    ff = w_sh_down_b.shape[0]
    t0 = row0 // tm
    plane_tiles = h_slabs.shape[0] // (tm * ROW_SLABS)
    gt_a, gt_spec, per_row = _mod_operand(gt, tm, rows_per_batch)
    row_spec = pl.BlockSpec((tm, D_MODEL), lambda i: (i, 0))
    plane_specs = [pl.BlockSpec((tm * ROW_SLABS, LANES), lambda i, k=k: (k * plane_tiles + t0 + i, 0))
                   for k in range(MOE_TOPK)]
    return pl.pallas_call(
        functools.partial(_combine_kernel, tm=tm, per_row=per_row, ff=ff),
        grid=(rows // tm,),
        in_specs=plane_specs + [
                  pl.BlockSpec((tm * ROW_SLABS, LANES), lambda i: (t0 + i, 0)),
                  pl.BlockSpec((D_MODEL, 2 * ff), lambda i: (0, 0)),
                  pl.BlockSpec((ff, D_MODEL), lambda i: (0, 0)),
                  row_spec,
                  pl.BlockSpec((1, D_MODEL), lambda i: (0, 0)),
                  gt_spec],
        out_specs=row_spec,
        out_shape=jax.ShapeDtypeStruct((rows, D_MODEL), F32),
        scratch_shapes=[pltpu.VMEM((tm * ROW_SLABS, LANES), F32),
                        pltpu.VMEM((tm, D_MODEL), BF16),
                        pltpu.VMEM((tm, D_MODEL), F32)],
        compiler_params=_cparams("parallel"),
        name="combine",
    )(*([y_planes] * MOE_TOPK), h_slabs, w_sh_gu_b, w_sh_down_b, x1, ln_g[None, :], gt_a)


def _moe(x1_p, x1_s, mod_p, mod_s, rpb_p, rpb_s, ln_pre, ln_post, w_router, router_bias, w_gu, w_down,
         w_sh_gu_b, w_sh_down_b):
    tm_s = x1_s.shape[0]
    h_p, ei_p, ew_p, cnt_p = _route(x1_p, ln_pre, mod_p[0], mod_p[1], w_router, router_bias, rpb_p, 256)
    h_s, ei_s, ew_s, cnt_s = _route(x1_s, ln_pre, mod_s[0], mod_s[1], w_router, router_bias, rpb_s, tm_s)
    h_all = jnp.concatenate([h_p, h_s], axis=0)
    eidx = jnp.concatenate([ei_p, ei_s], axis=1).T
    ew = jnp.concatenate([ew_p, ew_s], axis=1).T
    counts = (jnp.sum(cnt_p, axis=(0, 2)) + jnp.sum(cnt_s, axis=(0, 2))).astype(I32)
    y4 = _experts(h_all, eidx, ew, counts, w_gu, w_down)
    out_p = _combine(y4, h_all, 0, w_sh_gu_b, w_sh_down_b, x1_p, ln_post, mod_p[2], rpb_p, 128)
    out_s = _combine(y4, h_all, x1_p.shape[0], w_sh_gu_b, w_sh_down_b, x1_s, ln_post, mod_s[2], rpb_s, tm_s)
    return out_p, out_s


def _t5_bucket(dist):
    n = jnp.maximum(dist, 0)
    max_exact = REL_BUCKETS // 2
    nf = jnp.maximum(n, 1).astype(F32)
    large = max_exact + (jnp.log(nf / max_exact) / math.log(REL_MAX_DIST / max_exact)
                         * (REL_BUCKETS - max_exact)).astype(I32)
    return jnp.where(n < max_exact, n, jnp.minimum(large, REL_BUCKETS - 1))


def _bucket_bias(bucket, rel_ref, head):
    out = jnp.zeros(bucket.shape, F32)
    for b in range(REL_BUCKETS):
        out = jnp.where(bucket == b, rel_ref[b, head], out)
    return out


def _bias_tile_kernel(rel_ref, o_ref, *, d_min, lo, hi):
    d = pl.program_id(0) + d_min
    row = lax.broadcasted_iota(I32, (LANES, LANES), 0)
    col = lax.broadcasted_iota(I32, (LANES, LANES), 1)
    dist = d * LANES + row - col
    bucket = _t5_bucket(dist)
    visible = (dist >= lo) & (dist < hi)
    for h in range(NSA_HEADS):
        o_ref[h, 0] = jnp.where(visible, _bucket_bias(bucket, rel_ref, h), NEG_INF)


def _bias_tiles(rel_bias, d_min, n_d, lo, hi):
    return pl.pallas_call(
        functools.partial(_bias_tile_kernel, d_min=d_min, lo=lo, hi=hi),
        grid=(n_d,),
        in_specs=[pl.BlockSpec(memory_space=pltpu.SMEM)],
        out_specs=pl.BlockSpec((NSA_HEADS, 1, LANES, LANES), lambda d: (0, d, 0, 0)),
        out_shape=jax.ShapeDtypeStruct((NSA_HEADS, n_d, LANES, LANES), F32),
        compiler_params=_cparams("parallel"),
        name="bias_tiles",
    )(rel_bias)


CMP_ROWS = 128


def _cmp_select_kernel(rel_ref, q_ref, k_ref, v_ref, o_ref, neg_ref, idx_ref, *, tq, nb, pos0):
    hkv = pl.program_id(1)
    rq = min(tq, CMP_ROWS)
    k = k_ref[...].astype(BF16)
    v = v_ref[...].astype(BF16)
    blk = lax.broadcasted_iota(I32, (rq, nb), 1)
    blkf = blk.astype(F32)
    tile0 = pl.program_id(2) * tq

    def step(c):
        r0 = c * rq
        rows = slice(r0, r0 + rq)
        qpos = pos0 + tile0 + r0 + lax.broadcasted_iota(I32, (rq, nb), 0)
        dist = qpos - (blk * CMP_BLOCK + (CMP_BLOCK - 1))
        bucket = _t5_bucket(dist)
        seen = dist >= 0
        psum = jnp.zeros((rq, nb), F32)
        for g in range(NSA_GROUP):
            cols = slice(g * HEAD_DIM, (g + 1) * HEAD_DIM)
            s = lax.dot_general(q_ref[rows, cols].astype(BF16), k, (((1,), (1,)), ((), ())),
                                preferred_element_type=F32) * ATTN_SCALE
            bias = jnp.zeros((rq, nb), F32)
            for b in range(REL_BUCKETS):
                bias = jnp.where(bucket == b, rel_ref[b, hkv * NSA_GROUP + g], bias)
            s = jnp.where(seen, s + bias, NEG_INF)
            e = jnp.exp(s - jnp.max(s, axis=1, keepdims=True))
            p = e / jnp.sum(e, axis=1, keepdims=True) * seen.astype(F32)
            o_ref[rows, cols] = jnp.dot(p.astype(BF16), v, preferred_element_type=F32)
            psum = psum + p
        cur = lax.shift_right_logical(qpos, CMP_SHIFT)
        score = jnp.where(blk < cur, psum, -1.0)
        chosen = blk == cur
        lane = lax.broadcasted_iota(I32, (rq, SEL_TOPK), 1)
        picks = jnp.where(lane == 0, cur[:, :SEL_TOPK].astype(F32), -1.0)
        for r in range(1, SEL_TOPK):
            m = jnp.max(score, axis=1, keepdims=True)
            first = jnp.min(jnp.where(score == m, blkf, float(nb)), axis=1, keepdims=True)
            hit = blkf == first
            ok = m >= 0.0
            chosen = chosen | (hit & ok)
            picks = jnp.where(lane == r, jnp.where(ok, first, -1.0), picks)
            score = jnp.where(hit, -2.0, score)
        neg_ref[rows, :] = jnp.where(chosen, 0.0, NEG_INF).astype(BF16)
        idx_ref[rows, :] = picks.astype(I32)

    for c in range(tq // rq):
        step(c)


def _cmp_select(q3, col_blk0, kvc, rel_bias, tq, pos0):
    b, lq, _ = q3.shape
    nb = kvc.shape[1]
    gw = NSA_GROUP * HEAD_DIM
    return pl.pallas_call(
        functools.partial(_cmp_select_kernel, tq=tq, nb=nb, pos0=pos0),
        grid=(b, NSA_KV_HEADS, lq // tq),
        in_specs=[pl.BlockSpec(memory_space=pltpu.SMEM),
                  pl.BlockSpec((None, tq, gw), lambda i, h, t: (i, t, col_blk0 + h)),
                  pl.BlockSpec((None, nb, HEAD_DIM), lambda i, h, t: (i, 0, h)),
                  pl.BlockSpec((None, nb, HEAD_DIM), lambda i, h, t: (i, 0, NSA_KV_HEADS + h))],
        out_specs=[pl.BlockSpec((None, tq, gw), lambda i, h, t: (i, t, h)),
                   pl.BlockSpec((None, None, tq, nb), lambda i, h, t: (h, i, t, 0)),
                   pl.BlockSpec((None, None, tq, SEL_TOPK), lambda i, h, t: (h, i, t, 0))],
        out_shape=[jax.ShapeDtypeStruct((b, lq, NSA_HEADS * HEAD_DIM), F32),
                   jax.ShapeDtypeStruct((NSA_KV_HEADS, b, lq, nb), BF16),
                   jax.ShapeDtypeStruct((NSA_KV_HEADS, b, lq, SEL_TOPK), I32)],
        compiler_params=_cparams("parallel", "parallel", "parallel"),
        name="cmp_select",
    )(rel_bias, q3, kvc, kvc)


ATT_T = 512
ATT_SUB = 128


def _flash_kernel(*refs, selected, d_min):
    if selected:
        q_ref, neg_ref, k_ref, v_ref, t_ref, o_ref, m_scr, l_scr, acc_scr = refs
    else:
        q_ref, k_ref, v_ref, t_ref, o_ref, m_scr, l_scr, acc_scr = refs
    qi, kk = pl.program_id(2), pl.program_id(3)
    kj = kk if selected else qi - 1 + kk
    nsub = ATT_T // ATT_SUB

    @pl.when(kk == 0)
    def _():
        m_scr[...] = jnp.full(m_scr.shape, NEG_INF, F32)
        l_scr[...] = jnp.zeros(l_scr.shape, F32)
        acc_scr[...] = jnp.zeros(acc_scr.shape, F32)

    @pl.when((kj >= 0) & (kj <= qi))
    def _():
        kb = k_ref[...].astype(BF16)
        if selected:
            nb = neg_ref.shape[1]
            key_blk = lax.shift_right_logical(kj * ATT_T + lax.broadcasted_iota(I32, (ATT_T, nb), 0), CMP_SHIFT)
            onehot = jnp.where(key_blk == lax.broadcasted_iota(I32, (ATT_T, nb), 1), 1.0, 0.0)
            kb = jnp.concatenate([kb, onehot.astype(BF16)], axis=1)
        vb = v_ref[...].astype(BF16)
        d0 = (qi - kj) * nsub - d_min
        subs = range(nsub)
        rows = [slice(a * ATT_SUB, (a + 1) * ATT_SUB) for a in subs]
        qa = [jnp.concatenate([q_ref[rows[a], g * HEAD_DIM:(g + 1) * HEAD_DIM] for g in range(NSA_GROUP)],
                              axis=0).astype(BF16) for a in subs]
        if selected:
            qa = [jnp.concatenate([qa[a], jnp.concatenate([neg_ref[rows[a], :]] * NSA_GROUP, axis=0)], axis=1)
                  for a in subs]
        s = [lax.dot_general(qa[a], kb, (((1,), (1,)), ((), ())), preferred_element_type=F32) for a in subs]
        bias = [jnp.concatenate([jnp.concatenate([t_ref[g, d0 + a - c] for c in subs], axis=1)
                                 for g in range(NSA_GROUP)], axis=0) for a in subs]
        s = [s[a] * ATTN_SCALE + bias[a] for a in subs]
        m_prev = [m_scr[a] for a in subs]
        m_new = [jnp.maximum(m_prev[a], jnp.max(s[a], axis=1, keepdims=True)) for a in subs]
        alpha = [jnp.exp(m_prev[a] - m_new[a]) for a in subs]
        p = [jnp.exp(s[a] - m_new[a]) for a in subs]
        pv = [jnp.dot(p[a].astype(BF16), vb, preferred_element_type=F32) for a in subs]
        for a in subs:
            l_scr[a] = alpha[a] * l_scr[a] + jnp.sum(p[a], axis=1, keepdims=True)
            acc_scr[a] = alpha[a] * acc_scr[a] + pv[a]
            m_scr[a] = m_new[a]

    @pl.when(kk == pl.num_programs(3) - 1)
    def _():
        for a in range(nsub):
            out = acc_scr[a] / l_scr[a]
            for g in range(NSA_GROUP):
                o_ref[a * ATT_SUB:(a + 1) * ATT_SUB, g * HEAD_DIM:(g + 1) * HEAD_DIM] = out[g * ATT_SUB:(g + 1) * ATT_SUB]


def _flash(proj, neg, tiles, b, l, branch, d_min):
    selected = neg is not None
    nq = l // ATT_T
    nk = nq if selected else 2
    gw = NSA_GROUP * HEAD_DIM
    n_d = tiles.shape[1]
    k_col = (P_KV + branch * KV_W) // HEAD_DIM

    def kj_of(qi, kk):
        return jnp.clip(kk if selected else qi - 1 + kk, 0, qi)

    in_specs = [pl.BlockSpec((ATT_T, gw), lambda i, h, qi, kk: (i * nq + qi, P_Q // gw + h))]
    args = [proj]
    if selected:
        in_specs.append(pl.BlockSpec((None, ATT_T, neg.shape[2]), lambda i, h, qi, kk: (h, i * nq + qi, 0)))
        args.append(neg)
    in_specs += [pl.BlockSpec((ATT_T, HEAD_DIM), lambda i, h, qi, kk: (i * nq + kj_of(qi, kk), k_col + h)),
                 pl.BlockSpec((ATT_T, HEAD_DIM),
                              lambda i, h, qi, kk: (i * nq + kj_of(qi, kk), k_col + NSA_KV_HEADS + h)),
                 pl.BlockSpec((NSA_GROUP, n_d, LANES, LANES), lambda i, h, qi, kk: (h, 0, 0, 0))]
    args += [proj, proj, tiles]
    return pl.pallas_call(
        functools.partial(_flash_kernel, selected=selected, d_min=d_min),
        grid=(b, NSA_KV_HEADS, nq, nk),
        in_specs=in_specs,
        out_specs=pl.BlockSpec((ATT_T, gw), lambda i, h, qi, kk: (i * nq + qi, h)),
        out_shape=jax.ShapeDtypeStruct((b * l, NSA_HEADS * HEAD_DIM), F32),
        scratch_shapes=[pltpu.VMEM((ATT_T // ATT_SUB, NSA_GROUP * ATT_SUB, 1), F32),
                        pltpu.VMEM((ATT_T // ATT_SUB, NSA_GROUP * ATT_SUB, 1), F32),
                        pltpu.VMEM((ATT_T // ATT_SUB, NSA_GROUP * ATT_SUB, HEAD_DIM), F32)],
        compiler_params=_cparams("parallel", "parallel", "parallel", "arbitrary"),
        name="flash_sel" if selected else "flash_win",
    )(*args)


QPAD = SUBLANES
NEW_PAD = LANES


def _masked_attend(s, mask, parts):
    s = [jnp.where(m, x, NEG_INF) for x, m in zip(s, mask)]
    top = s[0].max(axis=1, keepdims=True)
    for x in s[1:]:
        top = jnp.maximum(top, x.max(axis=1, keepdims=True))
    e = [jnp.exp(x - top) for x in s]
    den = sum(x.sum(axis=1, keepdims=True) for x in e)
    out = None
    for x, m, v in zip(e, mask, parts):
        term = jnp.dot((x / den * m.astype(F32)).astype(BF16), v, preferred_element_type=F32)
        out = term if out is None else out + term
    return out


def _group_rows(q_ref, hkv):
    return jnp.concatenate([q_ref[:, (hkv * NSA_GROUP + g) * HEAD_DIM:(hkv * NSA_GROUP + g + 1) * HEAD_DIM]
                            for g in range(NSA_GROUP)], axis=0).astype(BF16)


def _rows_bias(bucket, rel_ref, hkv):
    return jnp.concatenate([_bucket_bias(bucket[g * QPAD:(g + 1) * QPAD], rel_ref, hkv * NSA_GROUP + g)
                            for g in range(NSA_GROUP)], axis=0)


def _sel_sample_kernel(pick_ref, page_ref, rel_ref, q_ref, new_ref, kpos_ref, own_ref, pool_hbm, o_ref,
                       blkbuf, kbuf, vbuf, sem, *, n_b, n_tok, n_pick, past, pages_per_seq):
    i = pl.program_id(0)
    n_keys = n_tok * n_pick * CMP_BLOCK
    half_pages = PAGE_ROWS // CMP_BLOCK
    blk_rows = CMP_BLOCK * N_KV_SLABS

    def fetch(b, hkv, tok, j):
        blk = jnp.maximum(pick_ref[((hkv * n_b + b) * QPAD + tok) * SEL_TOPK + 1 + j], 0)
        page = page_ref[b * pages_per_seq + blk // half_pages]
        row0 = pl.multiple_of((page * half_pages + lax.rem(blk, half_pages)) * blk_rows, blk_rows)
        p = (hkv * n_tok + tok) * n_pick + j
        return pltpu.make_async_copy(pool_hbm.at[pl.ds(row0, blk_rows), :],
                                     blkbuf.at[pl.ds(p * blk_rows, blk_rows), :], sem.at[0])

    def fetch_all(b, wait):
        for hkv in range(NSA_KV_HEADS):
            for tok in range(n_tok):
                for j in range(n_pick):
                    cp = fetch(b, hkv, tok, j)
                    cp.wait() if wait else cp.start()

    @pl.when(i == 0)
    def _():
        fetch_all(0, False)
        zeros = jnp.zeros((NEW_PAD, HEAD_DIM), F32)
        for hkv in range(NSA_KV_HEADS):
            kbuf[hkv, n_keys:n_keys + NEW_PAD, :] = zeros
            vbuf[hkv, n_keys:n_keys + NEW_PAD, :] = zeros

    fetch_all(i, True)
    for hkv in range(NSA_KV_HEADS):
        for p in range(n_tok * n_pick):
            base = (hkv * n_tok * n_pick + p) * blk_rows
            keys = pl.ds(p * CMP_BLOCK, CMP_BLOCK)
            kbuf[hkv, keys, :] = blkbuf[pl.ds(base + hkv, CMP_BLOCK, stride=N_KV_SLABS), :]
            vbuf[hkv, keys, :] = blkbuf[pl.ds(base + NSA_KV_HEADS + hkv, CMP_BLOCK, stride=N_KV_SLABS), :]

    @pl.when(i + 1 < n_b)
    def _():
        fetch_all(i + 1, False)

    kpos = kpos_ref[...]
    n_all = n_keys + NEW_PAD
    tok_of_row = lax.rem(lax.broadcasted_iota(I32, (NSA_GROUP * QPAD, n_all), 0), QPAD)
    for hkv in range(NSA_KV_HEADS):
        kbuf[hkv, n_keys:n_keys + n_tok, :] = new_ref[:, hkv * HEAD_DIM:(hkv + 1) * HEAD_DIM]
        vbuf[hkv, n_keys:n_keys + n_tok, :] = new_ref[:, (NSA_KV_HEADS + hkv) * HEAD_DIM:
                                                      (NSA_KV_HEADS + hkv + 1) * HEAD_DIM]
        q = _group_rows(q_ref, hkv)
        s = lax.dot_general(q, kbuf[hkv].astype(BF16), (((1,), (1,)), ((), ())),
                            preferred_element_type=F32) * ATTN_SCALE
        kp = kpos[hkv:hkv + 1, :]
        own = own_ref[hkv:hkv + 1, :]
        dist = past + tok_of_row - kp
        mask = (kp >= 0) & (dist >= 0) & ((own < 0) | (own == tok_of_row))
        s = s + _rows_bias(_t5_bucket(dist), rel_ref, hkv)
        out = _masked_attend([s], [mask], [vbuf[hkv].astype(BF16)])
        for g in range(NSA_GROUP):
            c = (hkv * NSA_GROUP + g) * HEAD_DIM
            o_ref[:, c:c + HEAD_DIM] = out[g * QPAD:(g + 1) * QPAD, :]


PAGE_ROWS = 128


def _sel_sample(picks, page_table, rel_bias, q_pad, proj3, pool2d, past, n_tok):
    n_b = q_pad.shape[0]
    n_pick = SEL_TOPK - 1
    n_keys = n_tok * n_pick * CMP_BLOCK
    blk = picks[:, :, :n_tok, 1:]
    kpos = jnp.where(blk[..., None] >= 0, blk[..., None] * CMP_BLOCK + jnp.arange(CMP_BLOCK, dtype=I32), -1)
    kpos = kpos.transpose(1, 0, 2, 3, 4).reshape(n_b, NSA_KV_HEADS, n_keys)
    new_pos = jnp.where(jnp.arange(NEW_PAD) < n_tok, past + jnp.arange(NEW_PAD), -1).astype(I32)
    kpos = jnp.concatenate([kpos, jnp.broadcast_to(new_pos, (n_b, NSA_KV_HEADS, NEW_PAD))], axis=2)
    own = jnp.concatenate([jnp.repeat(jnp.arange(n_tok, dtype=I32), n_pick * CMP_BLOCK),
                           jnp.full((NEW_PAD,), -1, I32)])
    own = jnp.broadcast_to(own, (NSA_KV_HEADS, n_keys + NEW_PAD))
    sel_col = (P_KV + KV_W) // KV_W
    return pl.pallas_call(
        functools.partial(_sel_sample_kernel, n_b=n_b, n_tok=n_tok, n_pick=n_pick, past=past,
                          pages_per_seq=page_table.shape[1]),
        grid_spec=pltpu.PrefetchScalarGridSpec(
            num_scalar_prefetch=2,
            grid=(n_b,),
            in_specs=[pl.BlockSpec(memory_space=pltpu.SMEM),
                      pl.BlockSpec((None, QPAD, NSA_HEADS * HEAD_DIM), lambda i, pk, pg: (i, 0, 0)),
                      pl.BlockSpec((None, n_tok, KV_W), lambda i, pk, pg: (i, 0, sel_col)),
                      pl.BlockSpec((None, NSA_KV_HEADS, n_keys + NEW_PAD), lambda i, pk, pg: (i, 0, 0)),
                      pl.BlockSpec((NSA_KV_HEADS, n_keys + NEW_PAD), lambda i, pk, pg: (0, 0)),
                      pl.BlockSpec(memory_space=pl.ANY)],
            out_specs=pl.BlockSpec((None, QPAD, NSA_HEADS * HEAD_DIM), lambda i, pk, pg: (i, 0, 0)),
            scratch_shapes=[pltpu.VMEM((NSA_KV_HEADS * n_tok * n_pick * CMP_BLOCK * N_KV_SLABS, HEAD_DIM), F32),
                            pltpu.VMEM((NSA_KV_HEADS, n_keys + NEW_PAD, HEAD_DIM), F32),
                            pltpu.VMEM((NSA_KV_HEADS, n_keys + NEW_PAD, HEAD_DIM), F32),
                            pltpu.SemaphoreType.DMA((1,))]),
        out_shape=jax.ShapeDtypeStruct((n_b, QPAD, NSA_HEADS * HEAD_DIM), F32),
        compiler_params=_cparams("arbitrary"),
        name="sel_sample",
    )(picks.reshape(-1), page_table.reshape(-1), rel_bias, q_pad, proj3, kpos, own, pool2d)


def _win_sample_kernel(rel_ref, q_ref, new_ref, buf_ref, o_ref, roll_ref, new_pad, *, n_tok, past):
    wb = buf_ref.shape[0]
    new_pad[...] = jnp.zeros(new_pad.shape, F32)
    new_pad[0:n_tok, :] = new_ref[...]
    rows = NSA_GROUP * QPAD
    tok_old = lax.rem(lax.broadcasted_iota(I32, (rows, wb), 0), QPAD)
    tok_new = lax.rem(lax.broadcasted_iota(I32, (rows, QPAD), 0), QPAD)
    dist_old = tok_old + wb - lax.broadcasted_iota(I32, (rows, wb), 1)
    new_col = lax.broadcasted_iota(I32, (rows, QPAD), 1)
    dist_new = tok_new - new_col
    kpos_old = past - wb + lax.broadcasted_iota(I32, (rows, wb), 1)
    mask_old = (dist_old >= 0) & (dist_old < WINDOW) & (kpos_old >= 0)
    mask_new = (dist_new >= 0) & (dist_new < WINDOW) & (new_col < n_tok)
    tb = (((1,), (1,)), ((), ()))
    for hkv in range(NSA_KV_HEADS):
        kc = slice(hkv * HEAD_DIM, (hkv + 1) * HEAD_DIM)
        vc = slice((NSA_KV_HEADS + hkv) * HEAD_DIM, (NSA_KV_HEADS + hkv + 1) * HEAD_DIM)
        q = _group_rows(q_ref, hkv)
        s_old = lax.dot_general(q, buf_ref[:, kc].astype(BF16), tb, preferred_element_type=F32) * ATTN_SCALE
        s_new = lax.dot_general(q, new_pad[:, kc].astype(BF16), tb, preferred_element_type=F32) * ATTN_SCALE
        s_old = s_old + _rows_bias(_t5_bucket(dist_old), rel_ref, hkv)
        s_new = s_new + _rows_bias(_t5_bucket(dist_new), rel_ref, hkv)
        out = _masked_attend([s_old, s_new], [mask_old, mask_new],
                             [buf_ref[:, vc].astype(BF16), new_pad[:, vc].astype(BF16)])
        for g in range(NSA_GROUP):
            c = (hkv * NSA_GROUP + g) * HEAD_DIM
            o_ref[:, c:c + HEAD_DIM] = out[g * QPAD:(g + 1) * QPAD, :]
    roll_ref[0:wb - n_tok, :] = buf_ref[n_tok:wb, :]
    roll_ref[wb - n_tok:wb, :] = new_ref[...]


def _win_sample(rel_bias, q_pad, proj3, win_buf2d, past, n_tok):
    n_b, wb, _ = win_buf2d.shape
    win_col = (P_KV + 2 * KV_W) // KV_W
    return pl.pallas_call(
        functools.partial(_win_sample_kernel, n_tok=n_tok, past=past),
        grid=(n_b,),
        in_specs=[pl.BlockSpec(memory_space=pltpu.SMEM),
                  pl.BlockSpec((None, QPAD, NSA_HEADS * HEAD_DIM), lambda i: (i, 0, 0)),
                  pl.BlockSpec((None, n_tok, KV_W), lambda i: (i, 0, win_col)),
                  pl.BlockSpec((None, wb, KV_W), lambda i: (i, 0, 0))],
        out_specs=[pl.BlockSpec((None, QPAD, NSA_HEADS * HEAD_DIM), lambda i: (i, 0, 0)),
                   pl.BlockSpec((None, wb, KV_W), lambda i: (i, 0, 0))],
        out_shape=[jax.ShapeDtypeStruct((n_b, QPAD, NSA_HEADS * HEAD_DIM), F32),
                   jax.ShapeDtypeStruct((n_b, wb, KV_W), F32)],
        scratch_shapes=[pltpu.VMEM((QPAD, KV_W), F32)],
        compiler_params=_cparams("parallel"),
        name="win_sample",
    )(rel_bias, q_pad, proj3, win_buf2d)


CMP_GROUP = 64
CMP_PITCH = CMP_BLOCK + 8
N_KV_SLABS = KV_W // HEAD_DIM
CMP_ROW_PITCH = CMP_BLOCK * N_KV_SLABS + 8


def _compress_kernel(blk_ref, src_hbm, w1_ref, pe_ref, b1_ref, w2_ref, o_ref, buf, flat, c1_scr, sem,
                     *, col0, n_grp):
    i = pl.program_id(0)
    slot = lax.rem(i, 2)
    g = CMP_GROUP
    row_view = col0 is None
    blk_rows = CMP_BLOCK * N_KV_SLABS

    def fetch(grp, to_slot, k, slab):
        blk = blk_ref[grp * g + k]
        if row_view:
            return pltpu.make_async_copy(
                src_hbm.at[pl.ds(pl.multiple_of(blk * blk_rows, blk_rows), blk_rows), :],
                buf.at[to_slot, pl.ds(k * CMP_ROW_PITCH, blk_rows), :], sem.at[to_slot])
        return pltpu.make_async_copy(
            src_hbm.at[pl.ds(pl.multiple_of(blk * CMP_BLOCK, CMP_BLOCK), CMP_BLOCK),
                       pl.ds(col0 + slab * HEAD_DIM, HEAD_DIM)],
            buf.at[to_slot, pl.ds((slab * g + k) * CMP_PITCH, CMP_BLOCK), :], sem.at[to_slot])

    def fetch_group(grp, to_slot, wait):
        for k in range(g):
            for slab in range(1 if row_view else N_KV_SLABS):
                cp = fetch(grp, to_slot, k, slab)
                cp.wait() if wait else cp.start()

    def block_rows(r, slab):
        if row_view:
            return buf[slot, pl.ds(r * N_KV_SLABS + slab, g, stride=CMP_ROW_PITCH), :]
        return buf[slot, pl.ds(slab * g * CMP_PITCH + r, g, stride=CMP_PITCH), :]

    @pl.when(i == 0)
    def _():
        fetch_group(0, 0, False)
        for s in range(2):
            pe_rows = jnp.broadcast_to(pe_ref[s:s + 1, :], (SUBLANES, CMP_BLOCK * HEAD_DIM)).astype(BF16)
            c1_scr[s:s + 1, :] = jnp.dot(pe_rows, w1_ref[s], preferred_element_type=F32)[0:1, :] + b1_ref[s:s + 1, :]

    @pl.when(i + 1 < n_grp)
    def _():
        fetch_group(i + 1, 1 - slot, False)

    fetch_group(i, slot, True)
    for s in range(2):
        for h in range(NSA_KV_HEADS):
            for r in range(CMP_BLOCK):
                flat[h * g:(h + 1) * g, r * HEAD_DIM:(r + 1) * HEAD_DIM] = (
                    block_rows(r, s * NSA_KV_HEADS + h).astype(BF16))
        acc = jnp.dot(flat[...], w1_ref[s], preferred_element_type=F32)
        hid = _silu(acc + c1_scr[s:s + 1, :]).astype(BF16)
        out = jnp.dot(hid, w2_ref[s], preferred_element_type=F32)
        for h in range(NSA_KV_HEADS):
            c = (s * NSA_KV_HEADS + h) * HEAD_DIM
            o_ref[:, c:c + HEAD_DIM] = out[h * g:(h + 1) * g, :]


def _compress(src2d, col0, blk_rows, cmp_pe, cmp_w1, cmp_b1, cmp_w2):
    n_blocks = blk_rows.shape[0]
    n_grp = n_blocks // CMP_GROUP
    stage_rows = CMP_GROUP * (CMP_ROW_PITCH if col0 is None else N_KV_SLABS * CMP_PITCH)
    pe_flat = cmp_pe.transpose(1, 0, 2).reshape(2, CMP_BLOCK * HEAD_DIM)
    return pl.pallas_call(
        functools.partial(_compress_kernel, col0=col0, n_grp=n_grp),
        grid_spec=pltpu.PrefetchScalarGridSpec(
            num_scalar_prefetch=1,
            grid=(n_grp,),
            in_specs=[pl.BlockSpec(memory_space=pl.ANY),
                      pl.BlockSpec((2, CMP_BLOCK * HEAD_DIM, HEAD_DIM), lambda i, br: (0, 0, 0)),
                      pl.BlockSpec((2, CMP_BLOCK * HEAD_DIM), lambda i, br: (0, 0)),
                      pl.BlockSpec((2, HEAD_DIM), lambda i, br: (0, 0)),
                      pl.BlockSpec((2, HEAD_DIM, HEAD_DIM), lambda i, br: (0, 0, 0))],
            out_specs=pl.BlockSpec((CMP_GROUP, KV_W), lambda i, br: (i, 0)),
            scratch_shapes=[pltpu.VMEM((2, stage_rows, HEAD_DIM), F32),
                            pltpu.VMEM((NSA_KV_HEADS * CMP_GROUP, CMP_BLOCK * HEAD_DIM), BF16),
                            pltpu.VMEM((2, HEAD_DIM), F32),
                            pltpu.SemaphoreType.DMA((2,))]),
        out_shape=jax.ShapeDtypeStruct((n_blocks, KV_W), F32),
        compiler_params=_cparams("arbitrary"),
        name="compress",
    )(blk_rows, src2d, cmp_w1.astype(BF16), pe_flat, cmp_b1, cmp_w2.astype(BF16))


CONV_PAD = SUBLANES
def _split_bf16(x, parts):
    out = []
    for _ in range(parts):
        piece = x.astype(BF16)
        out.append(piece)
        x = x - piece.astype(F32)
    return out


def _dot_hi(a, b):
    (ah, al), (bh, bl) = _split_bf16(a, 2), _split_bf16(b, 2)
    dot = functools.partial(jnp.dot, preferred_element_type=F32)
    return dot(ah, bh) + (dot(ah, bl) + dot(al, bh))


def _dot_mask(mask, x):
    m = mask.astype(BF16)
    return sum(jnp.dot(m, piece, preferred_element_type=F32) for piece in _split_bf16(x, 3))


def _unit_lower_inverses(lmats, c):
    eye = (lax.broadcasted_iota(I32, (c, c), 0) == lax.broadcasted_iota(I32, (c, c), 1)).astype(F32)
    x = [eye - m for m in lmats]
    p = list(lmats)
    span = 2
    while span < c:
        p = [_dot_hi(m, m) for m in p]
        x = [a + _dot_hi(a, m) for a, m in zip(x, p)]
        span *= 2
    return x


def _gdn_kernel(qkv_ref, z_ref, sm_ref, cw_ref, alog_ref, dtb_ref, nw_ref, conv0_ref, s0_ref,
                o_ref, sout_ref, cout_ref, xbuf, qkvc, s_scr, *, tl, chunk, l_valid, nt):
    t = pl.program_id(1)
    n_t = nt
    tail = CONV_WIDTH - 1

    @pl.when(t == 0)
    def _():
        xbuf[CONV_PAD - tail:CONV_PAD, :] = conv0_ref[...]
        s_scr[...] = s0_ref[...]

    xbuf[CONV_PAD:CONV_PAD + tl, :] = qkv_ref[...]
    for cb in range(GDN_CONV_DIM // LANES):
        cols = slice(cb * LANES, (cb + 1) * LANES)
        y = xbuf[CONV_PAD - tail:CONV_PAD - tail + tl, cols] * cw_ref[0:1, cols]
        for j in range(1, CONV_WIDTH):
            y = y + xbuf[CONV_PAD - tail + j:CONV_PAD - tail + j + tl, cols] * cw_ref[j:j + 1, cols]
        qkvc[:, cols] = _silu(y)

    lv = l_valid - t * tl
    small = sm_ref[...]
    live = lax.broadcasted_iota(I32, (tl, LANES), 0) < lv
    beta = jnp.where(live, jax.nn.sigmoid(small), 0.0)
    g = jnp.where(live, -jnp.exp(alog_ref[...]) * jax.nn.softplus(small + dtb_ref[...]), 0.0)
    ri = lax.broadcasted_iota(I32, (tl, tl), 0)
    ci = lax.broadcasted_iota(I32, (tl, tl), 1)
    cshift = chunk.bit_length() - 1
    same = lax.shift_right_logical(ri, cshift) == lax.shift_right_logical(ci, cshift)
    gc = _dot_mask(jnp.where(same & (ci <= ri), 1.0, 0.0), g)
    gl = _dot_mask(jnp.where(same, 1.0, 0.0), g)
    gc_t = gc.T
    low = lax.broadcasted_iota(I32, (chunk, chunk), 0) >= lax.broadcasted_iota(I32, (chunk, chunk), 1)
    strict = lax.broadcasted_iota(I32, (chunk, chunk), 0) > lax.broadcasted_iota(I32, (chunk, chunk), 1)
    tb = (((1,), (1,)), ((), ()))

    heads = range(GDN_HEADS)
    dot = functools.partial(jnp.dot, preferred_element_type=F32)
    jobs = [(c, h) for c in range(tl // chunk) for h in heads]
    rows_of = lambda c: slice(c * chunk, (c + 1) * chunk)
    col = lambda arr, c, lane: arr[rows_of(c), lane:lane + 1]

    def unit_rows(c, base, h):
        x = qkvc[rows_of(c), base + h * HEAD_DIM:base + (h + 1) * HEAD_DIM]
        return x * lax.rsqrt(jnp.sum(x * x, axis=-1, keepdims=True) + NORM_EPS)

    qn = [unit_rows(c, 0, h) * (HEAD_DIM ** -0.5) for c, h in jobs]
    kn = [unit_rows(c, GDN_QK, h) for c, h in jobs]
    vh = [qkvc[rows_of(c), 2 * GDN_QK + h * HEAD_DIM:2 * GDN_QK + (h + 1) * HEAD_DIM] for c, h in jobs]
    bcol = [col(beta, c, SM_BETA + h) for c, h in jobs]
    gcol = [col(gc, c, SM_DECAY + h) for c, h in jobs]
    glcol = [col(gl, c, SM_DECAY + h) for c, h in jobs]
    decay = [jnp.exp(jnp.where(low, g_c - gc_t[SM_DECAY + h:SM_DECAY + h + 1, rows_of(c)], NEG_INF))
             for (c, h), g_c in zip(jobs, gcol)]
    kb = [k * b for k, b in zip(kn, bcol)]
    lmat = [jnp.where(strict, lax.dot_general(a, k, tb, preferred_element_type=F32) * d, 0.0)
            for a, k, d in zip(kb, kn, decay)]
    inv = _unit_lower_inverses(lmat, chunk)
    e_g = [jnp.exp(g_c) for g_c in gcol]
    u = [_dot_hi(m, v * b) for m, v, b in zip(inv, vh, bcol)]
    w = [_dot_hi(m, a * e) for m, a, e in zip(inv, kb, e_g)]
    qk = [jnp.where(low, lax.dot_general(q, k, tb, preferred_element_type=F32) * d, 0.0)
          for q, k, d in zip(qn, kn, decay)]
    qg = [q * e for q, e in zip(qn, e_g)]
    kg = [k * jnp.exp(gl_c - g_c) for k, gl_c, g_c in zip(kn, glcol, gcol)]
    g_end = [jnp.exp(gl_c[0:1, :]) for gl_c in glcol]

    states = [s_scr[h] for h in heads]
    for c in range(tl // chunk):
        j0 = c * GDN_HEADS
        v_new = [u[j0 + h] - dot(w[j0 + h], states[h]) for h in heads]
        o = [dot(qg[j0 + h], states[h]) + dot(qk[j0 + h], v_new[h]) for h in heads]
        states = [states[h] * g_end[j0 + h] + lax.dot_general(kg[j0 + h], v_new[h], (((0,), (0,)), ((), ())),
                                                              preferred_element_type=F32) for h in heads]
        for h in heads:
            hc = slice(h * HEAD_DIM, (h + 1) * HEAD_DIM)
            y = o[h] * lax.rsqrt(jnp.mean(o[h] * o[h], axis=-1, keepdims=True) + NORM_EPS) * nw_ref[...]
            o_ref[rows_of(c), hc] = y * _silu(z_ref[rows_of(c), hc])
    for h in heads:
        s_scr[h] = states[h]

    @pl.when(t == n_t - 1)
    def _():
        sout_ref[...] = s_scr[...]
        last = l_valid - (nt - 1) * tl
        cout_ref[...] = xbuf[pl.ds(CONV_PAD - tail + last, tail), :]

    xbuf[CONV_PAD - tail:CONV_PAD, :] = xbuf[CONV_PAD - tail + tl:CONV_PAD + tl, :]


def _gdn(proj, b, l_pad, l_valid, conv0, s0, conv_w, a_log, dt_bias, norm_w, tl, chunk):
    nt = l_pad // tl
    lane_pad = lambda v: jnp.zeros((1, LANES), F32).at[0, SM_DECAY:SM_DECAY + GDN_HEADS].set(v)
    return pl.pallas_call(
        functools.partial(_gdn_kernel, tl=tl, chunk=chunk, l_valid=l_valid, nt=nt),
        grid=(b, nt),
        in_specs=[pl.BlockSpec((tl, GDN_CONV_DIM), lambda i, t: (i * nt + t, P_QKV // GDN_CONV_DIM)),
                  pl.BlockSpec((tl, GDN_QK), lambda i, t: (i * nt + t, P_Z // GDN_QK)),
                  pl.BlockSpec((tl, LANES), lambda i, t: (i * nt + t, P_SMALL // LANES)),
                  pl.BlockSpec((CONV_WIDTH, GDN_CONV_DIM), lambda i, t: (0, 0)),
                  pl.BlockSpec((1, LANES), lambda i, t: (0, 0)),
                  pl.BlockSpec((1, LANES), lambda i, t: (0, 0)),
                  pl.BlockSpec((1, HEAD_DIM), lambda i, t: (0, 0)),
                  pl.BlockSpec((None, CONV_WIDTH - 1, GDN_CONV_DIM), lambda i, t: (i, 0, 0)),
                  pl.BlockSpec((None, GDN_HEADS, HEAD_DIM, HEAD_DIM), lambda i, t: (i, 0, 0, 0))],
        out_specs=[pl.BlockSpec((tl, GDN_QK), lambda i, t: (i * nt + t, 0)),
                   pl.BlockSpec((None, GDN_HEADS, HEAD_DIM, HEAD_DIM), lambda i, t: (i, 0, 0, 0)),
                   pl.BlockSpec((None, CONV_WIDTH - 1, GDN_CONV_DIM), lambda i, t: (i, 0, 0))],
        out_shape=[jax.ShapeDtypeStruct((b * l_pad, GDN_QK), F32),
                   jax.ShapeDtypeStruct((b, GDN_HEADS, HEAD_DIM, HEAD_DIM), F32),
                   jax.ShapeDtypeStruct((b, CONV_WIDTH - 1, GDN_CONV_DIM), F32)],
        scratch_shapes=[pltpu.VMEM((CONV_PAD + tl, GDN_CONV_DIM), F32),
                        pltpu.VMEM((tl, GDN_CONV_DIM), F32),
                        pltpu.VMEM((GDN_HEADS, HEAD_DIM, HEAD_DIM), F32)],
        compiler_params=_cparams("parallel", "arbitrary"),
        name="gdn",
    )(proj, proj, proj, conv_w, lane_pad(a_log), lane_pad(dt_bias), norm_w[None, :], conv0, s0)


SEL_BLOCK = CMP_BLOCK
WIN_QBLOCK = 128
SEL_QBLOCK = 32
PAGE_SIZE = 128


def l2_normalize(x):
    return x * lax.rsqrt(jnp.sum(x * x, axis=-1, keepdims=True) + NORM_EPS)


def t5_bucket(dist):
    n = jnp.maximum(dist, 0)
    max_exact = REL_BUCKETS // 2
    nf = jnp.maximum(n, 1).astype(jnp.float32)
    large = max_exact + (jnp.log(nf / max_exact) / math.log(REL_MAX_DIST / max_exact)
                         * (REL_BUCKETS - max_exact)).astype(jnp.int32)
    return jnp.where(n < max_exact, n, jnp.minimum(large, REL_BUCKETS - 1))


def masked_probs(s, mask):
    s = jnp.where(mask, s.astype(jnp.float32), NEG_INF)
    return jax.nn.softmax(s, axis=-1) * mask


def short_conv(x, buf, w):
    L = x.shape[1]
    xp = jnp.concatenate([buf.astype(x.dtype), x], axis=1)
    y = sum(xp[:, j:j + L] * w[j] for j in range(CONV_WIDTH))
    return jax.nn.silu(y), xp[:, L:]


def gated_delta_chunked(q, k, v, g, beta, s0):
    B, H, L, dk = q.shape
    dv = v.shape[-1]
    C = math.gcd(L, GDN_CHUNK)
    n = L // C

    def chunks(t):
        return t.reshape(B, H, n, C, *t.shape[3:])

    q, k, v, g, beta = (chunks(t) for t in (q, k, v, g, beta))
    gc = jnp.cumsum(g, axis=-1)
    lower = jnp.tril(jnp.ones((C, C), bool))
    strict = jnp.tril(jnp.ones((C, C), bool), -1)
    decay = jnp.exp(jnp.where(lower, gc[..., :, None] - gc[..., None, :], NEG_INF))
    kb = k * beta[..., None]
    lmat = jnp.where(strict, jnp.einsum('bhncd,bhnjd->bhncj', kb, k) * decay, 0.0)
    rhs = jnp.concatenate([v * beta[..., None], kb * jnp.exp(gc)[..., None]], axis=-1)
    sol = lax.linalg.triangular_solve(lmat + jnp.eye(C, dtype=lmat.dtype), rhs,
                                      left_side=True, lower=True, unit_diagonal=True)
    u, w = sol[..., :dv], sol[..., dv:]
    qk = jnp.where(lower, jnp.einsum('bhncd,bhnjd->bhncj', q, k) * decay, 0.0)
    qg = q * jnp.exp(gc)[..., None]
    kg = k * jnp.exp(gc[..., -1:] - gc)[..., None]
    g_last = jnp.exp(gc[..., -1])

    def step(S, xs):
        u_i, w_i, qk_i, qg_i, kg_i, gl_i = xs
        v_new = u_i - jnp.einsum('bhcd,bhde->bhce', w_i, S)
        o = jnp.einsum('bhcd,bhde->bhce', qg_i, S) + jnp.einsum('bhcj,bhje->bhce', qk_i, v_new)
        S = S * gl_i[..., None, None] + jnp.einsum('bhcd,bhce->bhde', kg_i, v_new)
        return S, o

    xs = tuple(jnp.moveaxis(t, 2, 0) for t in (u, w, qk, qg, kg, g_last))
    S, o = lax.scan(step, s0, xs)
    return jnp.moveaxis(o, 0, 2).reshape(B, H, L, dv), S


def gdn_mixer(qkv, z, b_raw, a_raw, conv_buf, s0, conv_w, a_log, dt_bias, norm_w):
    B, L, _ = qkv.shape
    qkv_c, new_buf = short_conv(qkv, conv_buf, conv_w)
    qc, kc, vc = jnp.split(qkv_c, [GDN_QK, 2 * GDN_QK], axis=-1)

    def heads(t, d):
        return t.reshape(B, L, GDN_HEADS, d).transpose(0, 2, 1, 3).astype(jnp.float32)

    q = l2_normalize(heads(qc, HEAD_DIM)) * (HEAD_DIM ** -0.5)
    k = l2_normalize(heads(kc, HEAD_DIM))
    v = heads(vc, HEAD_DIM)
    beta = jax.nn.sigmoid(b_raw.astype(jnp.float32)).transpose(0, 2, 1)
    g = (-jnp.exp(a_log.astype(jnp.float32))
         * jax.nn.softplus(a_raw.astype(jnp.float32) + dt_bias.astype(jnp.float32))).transpose(0, 2, 1)
    o, s_new = gated_delta_chunked(q, k, v, g, beta, s0.astype(jnp.float32))
    o = o.transpose(0, 2, 1, 3)
    o = (o * lax.rsqrt(jnp.mean(o * o, axis=-1, keepdims=True) + NORM_EPS) * norm_w.astype(jnp.float32)
         * jax.nn.silu(z.reshape(B, L, GDN_HEADS, HEAD_DIM).astype(jnp.float32)))
    return o.reshape(B, L, GDN_HEADS * HEAD_DIM).astype(qkv.dtype), new_buf, s_new.astype(s0.dtype)


def compress_blocks(kv, pe, w1, b1, w2):
    B, Lk = kv.shape[:2]
    nb = Lk // CMP_BLOCK
    blk = kv[:, :nb * CMP_BLOCK].reshape(B, nb, CMP_BLOCK, 2, NSA_KV_HEADS, HEAD_DIM)
    blk = blk + pe[:, :, None, :]
    flat = blk.transpose(0, 1, 3, 4, 2, 5).reshape(B, nb, 2, NSA_KV_HEADS, CMP_BLOCK * HEAD_DIM)
    hid = jax.nn.silu(jnp.einsum('bnshf,sfe->bnshe', flat, w1) + b1[:, None, :])
    return jnp.einsum('bnshe,sed->bnshd', hid, w2)


def cmp_attend(q, qpos, kvc, rel_g):
    nb = kvc.shape[1]
    bend = jnp.arange(nb, dtype=jnp.int32) * CMP_BLOCK + (CMP_BLOCK - 1)
    dist = qpos[:, None] - bend[None, :]
    bias = rel_g[t5_bucket(dist)].transpose(2, 3, 0, 1)
    s = jnp.einsum('bhgqd,bnhd->bhgqn', q, kvc[:, :, 0]).astype(jnp.float32) * ATTN_SCALE + bias
    p = masked_probs(s, dist >= 0)
    o = jnp.einsum('bhgqn,bnhd->bhgqd', p, kvc[:, :, 1].astype(jnp.float32))
    return o, p


def select_blocks(p, qpos):
    score = jnp.sum(p, axis=2)
    B, Hkv, Q, nb = score.shape
    cur = qpos // SEL_BLOCK
    score = jnp.where(jnp.arange(nb)[None, :] < cur[:, None], score, -1.0)
    width = max(nb, SEL_TOPK - 1)
    score = jnp.pad(score, ((0, 0), (0, 0), (0, 0), (0, width - nb)), constant_values=-1.0)
    top_s, top_i = lax.top_k(score, SEL_TOPK - 1)
    cur_b = jnp.broadcast_to(cur[None, None, :, None], (B, Hkv, Q, 1)).astype(jnp.int32)
    idx = jnp.concatenate([cur_b, top_i.astype(jnp.int32)], axis=-1)
    valid = jnp.concatenate([jnp.ones((B, Hkv, Q, 1), bool), top_s >= 0], axis=-1)
    return idx, valid


def sel_attend(q, qpos, idx, valid, fetch, rel_g):
    B, Hkv, G, Q, dh = q.shape
    qc = math.gcd(Q, SEL_QBLOCK)
    nc = Q // qc
    qs = q.reshape(B, Hkv, G, nc, qc, dh).transpose(3, 0, 1, 2, 4, 5)
    ids = idx.reshape(B, Hkv, nc, qc, SEL_TOPK).transpose(2, 0, 1, 3, 4)
    vals = valid.reshape(B, Hkv, nc, qc, SEL_TOPK).transpose(2, 0, 1, 3, 4)
    ps = qpos.reshape(nc, qc)
    hidx = jnp.arange(Hkv)[None, :, None, None]
    offs = jnp.arange(SEL_BLOCK, dtype=jnp.int32)
    nkeys = SEL_TOPK * SEL_BLOCK

    def one(args):
        qb, ib, vb, pb = args
        kv = fetch(ib)
        kk = kv[..., 0, :].reshape(B, Hkv, qc, nkeys, dh)
        vv = kv[..., 1, :].reshape(B, Hkv, qc, nkeys, dh)
        kpos = (ib[..., None] * SEL_BLOCK + offs).reshape(B, Hkv, qc, nkeys)
        dist = pb[None, None, :, None] - kpos
        mask = jnp.broadcast_to(vb[..., None], (B, Hkv, qc, SEL_TOPK, SEL_BLOCK)).reshape(B, Hkv, qc, nkeys) & (dist >= 0)
        bias = rel_g[t5_bucket(dist), hidx].transpose(0, 1, 4, 2, 3)
        s = jnp.einsum('bhgqd,bhqkd->bhgqk', qb, kk).astype(jnp.float32) * ATTN_SCALE + bias
        p = masked_probs(s, mask[:, :, None])
        return jnp.einsum('bhgqk,bhqkd->bhgqd', p, vv.astype(jnp.float32))

    o = lax.map(one, (qs, ids, vals, ps))
    return o.transpose(1, 2, 3, 0, 4, 5).reshape(B, Hkv, G, Q, dh)


def win_attend(q, qpos, kv, kpos, rel_g):
    dist = qpos[:, None] - kpos[None, :]
    mask = (dist >= 0) & (dist < WINDOW) & (kpos[None, :] >= 0)
    bias = rel_g[t5_bucket(dist)].transpose(2, 3, 0, 1)
    s = jnp.einsum('bhgqd,bkhd->bhgqk', q, kv[:, :, 0]).astype(jnp.float32) * ATTN_SCALE + bias
    p = masked_probs(s, mask)
    return jnp.einsum('bhgqk,bkhd->bhgqd', p, kv[:, :, 1].astype(jnp.float32))


def win_attend_prompt(q, kv, rel_g):
    B, Hkv, G, L, dh = q.shape
    wq = math.gcd(L, WIN_QBLOCK)
    nq = L // wq
    kvp = jnp.pad(kv, ((0, 0), (WINDOW, 0), (0, 0), (0, 0), (0, 0)))
    qs = q.reshape(B, Hkv, G, nq, wq, dh).transpose(3, 0, 1, 2, 4, 5)

    def one(args):
        qb, i = args
        start = i * wq
        band = lax.dynamic_slice_in_dim(kvp, start, WINDOW + wq, axis=1)
        qpos = start + jnp.arange(wq, dtype=jnp.int32)
        kpos = start - WINDOW + jnp.arange(WINDOW + wq, dtype=jnp.int32)
        return win_attend(qb, qpos, band, kpos, rel_g)

    o = lax.map(one, (qs, jnp.arange(nq, dtype=jnp.int32)))
    return o.transpose(1, 2, 3, 0, 4, 5).reshape(B, Hkv, G, L, dh)


def make_nsa_prompt(rel_g, cmp_params):
    def attend(q, kv_cmp, kv_sel, kv_win):
        B, L = kv_cmp.shape[:2]
        qpos = jnp.arange(L, dtype=jnp.int32)
        o_cmp, p = cmp_attend(q, qpos, compress_blocks(kv_cmp, *cmp_params), rel_g)
        idx, valid = select_blocks(p, qpos)
        nblk = -(-L // SEL_BLOCK)
        store = jnp.pad(kv_sel, ((0, 0), (0, nblk * SEL_BLOCK - L), (0, 0), (0, 0), (0, 0)))
        bidx = jnp.arange(B)[:, None, None, None, None]
        hidx = jnp.arange(NSA_KV_HEADS)[None, :, None, None, None]
        offs = jnp.arange(SEL_BLOCK, dtype=jnp.int32)

        def fetch(ib):
            rows = jnp.clip(ib, 0, nblk - 1)[..., None] * SEL_BLOCK + offs
            return store[bidx, rows, :, hidx]

        o_sel = sel_attend(q, qpos, idx, valid, fetch, rel_g)
        o_win = win_attend_prompt(q, kv_win, rel_g)
        return o_cmp, o_sel, o_win, (kv_cmp, kv_sel, kv_win[:, L - min(WINDOW, L):])
    return attend


def make_nsa_sample(rel_g, cmp_params, pool_cmp, pool_sel, win_buf, page_table):
    def attend(q, kv_cmp, kv_sel, kv_win):
        Bd, L = kv_cmp.shape[:2]
        n_pages = page_table.shape[1]
        past = n_pages * PAGE_SIZE
        qpos = past + jnp.arange(L, dtype=jnp.int32)
        past_cmp = pool_cmp[page_table].reshape(Bd, past, 2, NSA_KV_HEADS, HEAD_DIM).astype(kv_cmp.dtype)
        kvc = jnp.concatenate([compress_blocks(past_cmp, *cmp_params),
                               compress_blocks(kv_cmp, *cmp_params)], axis=1)
        o_cmp, p = cmp_attend(q, qpos, kvc, rel_g)
        idx, valid = select_blocks(p, qpos)
        bpp = PAGE_SIZE // SEL_BLOCK
        n_past_blk = n_pages * bpp
        n_new_blk = -(-L // SEL_BLOCK)
        new_rows = jnp.pad(kv_sel, ((0, 0), (0, n_new_blk * SEL_BLOCK - L), (0, 0), (0, 0), (0, 0)))
        bidx = jnp.arange(Bd)[:, None, None, None, None]
        hidx = jnp.arange(NSA_KV_HEADS)[None, :, None, None, None]
        offs = jnp.arange(SEL_BLOCK, dtype=jnp.int32)

        def fetch(ib):
            ip = jnp.clip(ib, 0, n_past_blk - 1)
            phys = page_table[bidx[..., 0], ip // bpp][..., None]
            from_past = pool_sel[phys, (ip % bpp)[..., None] * SEL_BLOCK + offs, :, hidx]
            rows_new = jnp.clip(ib - n_past_blk, 0, n_new_blk - 1)[..., None] * SEL_BLOCK + offs
            from_new = new_rows[bidx, rows_new, :, hidx]
            return jnp.where((ib >= n_past_blk)[..., None, None, None], from_new, from_past.astype(from_new.dtype))

        o_sel = sel_attend(q, qpos, idx, valid, fetch, rel_g)
        wb = win_buf.shape[1]
        kw = jnp.concatenate([win_buf.astype(kv_win.dtype), kv_win], axis=1)
        kpos = past - wb + jnp.arange(wb + L, dtype=jnp.int32)
        o_win = win_attend(q, qpos, kw, kpos, rel_g)
        return o_cmp, o_sel, o_win, (kv_cmp, kv_sel, kw[:, L:])
    return attend


def _heads_to_rows(o):
    b, hkv, g, l, dh = o.shape
    return o.transpose(0, 3, 1, 2, 4).reshape(b * l, hkv * g * dh)


def _jax_mixers(proj, b, l, nsa_attend, conv_buf, s0, conv_w, a_log, dt_bias, norm_w):
    p3 = proj.reshape(b, l, P_DIM)
    qkv = p3[..., P_QKV:P_Z]
    z = p3[..., P_Z:P_Q]
    nsa_q = p3[..., P_Q:P_KV]
    nsa_kv = p3[..., P_KV:P_SMALL]
    small = p3[..., P_SMALL:P_SMALL + LANES]
    b_raw = small[..., SM_BETA:SM_BETA + GDN_HEADS]
    a_raw = small[..., SM_DECAY:SM_DECAY + GDN_HEADS]
    o_gdn, new_conv, new_s = gdn_mixer(qkv, z, b_raw, a_raw, conv_buf, s0, conv_w, a_log, dt_bias, norm_w)
    q = nsa_q.reshape(b, l, NSA_KV_HEADS, NSA_GROUP, HEAD_DIM).transpose(0, 2, 3, 1, 4)
    kv = nsa_kv.reshape(b, l, N_BRANCH, 2, NSA_KV_HEADS, HEAD_DIM)
    o_cmp, o_sel, o_win, nsa_state = nsa_attend(q, kv[:, :, 0], kv[:, :, 1], kv[:, :, 2])
    return (o_gdn.reshape(b * l, GDN_QK), _heads_to_rows(o_cmp), _heads_to_rows(o_sel), _heads_to_rows(o_win),
            nsa_state, new_s, new_conv)


def kernel(x_prompt, x_sample, cache_cmp_kv, cache_sel_kv, cache_win_kv, state_gdn, state_conv, page_table,
           c_prompt, c_sample, rel_bias, w_ada, b_ada, ln_mix_pre, ln_mix_post, ln_ffn_pre, ln_ffn_post,
           w_in, w_out, conv_w, gdn_a_log, gdn_dt_bias, gdn_norm, cmp_pe, cmp_w1, cmp_b1, cmp_w2,
           w_router, router_bias, w_exp_gu, w_exp_down, w_sh_gu, w_sh_down):
    bp, lp, _ = x_prompt.shape
    bs, ls, _ = x_sample.shape
    xp = x_prompt.reshape(bp * lp, D_MODEL)
    xs = x_sample.reshape(bs * ls, D_MODEL)
    mod = _ada(jnp.concatenate([c_prompt, c_sample], axis=0), w_ada.reshape(w_ada.shape[1:]), b_ada[0])
    sh1, sc1, gt1, sh2, sc2, gt2 = jnp.split(mod, 6, axis=1)
    w_in_p = _pack_w_in(w_in[0])
    proj_p = _inproj(xp, ln_mix_pre[0], sc1[:bp], sh1[:bp], w_in_p, lp, 512)
    proj_s = _inproj(xs, ln_mix_pre[0], sc1[bp:], sh1[bp:], w_in_p, ls, bs * ls)
    cmp_params = (cmp_pe[0], cmp_w1[0], cmp_b1[0], cmp_w2[0])
    gdn_params = (conv_w[0], gdn_a_log[0], gdn_dt_bias[0], gdn_norm[0])
    rel_bias = rel_bias.astype(F32)
    proj3_p = proj_p.reshape(bp, lp, P_DIM)
    proj3_s = proj_s.reshape(bs, ls, P_DIM)
    assert ls < CMP_BLOCK and ls <= QPAD, "the sample step adds less than one compressed block"

    conv0 = jnp.zeros((bp, CONV_WIDTH - 1, GDN_CONV_DIM), state_conv.dtype)
    s00 = jnp.zeros((bp, GDN_HEADS, HEAD_DIM, HEAD_DIM), state_gdn.dtype)
    o_gdn_p, gdn_p, conv_p = _gdn(proj_p, bp, lp, lp, conv0, s00, *gdn_params, 2 * GDN_CHUNK, GDN_CHUNK)
    nb_p = lp // CMP_BLOCK
    kvc_p = _compress(proj_p, P_KV, jnp.arange(bp * nb_p, dtype=I32), *cmp_params).reshape(bp, nb_p, KV_W)
    o_cmp_p, neg_p, _ = _cmp_select(proj3_p, P_Q // (NSA_GROUP * HEAD_DIM), kvc_p, rel_bias, ATT_T, 0)
    nsub = ATT_T // ATT_SUB
    d_min = 1 - nsub
    t_sel = _bias_tiles(rel_bias, d_min, (lp // ATT_T + 1) * nsub - 1, 0, 1 << 30)
    t_win = _bias_tiles(rel_bias, d_min, 3 * nsub - 1, 0, WINDOW)
    o_sel_p = _flash(proj_p, neg_p.reshape(NSA_KV_HEADS, bp * lp, nb_p), t_sel, bp, lp, 1, d_min)
    o_win_p = _flash(proj_p, None, t_win, bp, lp, 2, d_min)

    n_pages = page_table.shape[1]
    past = n_pages * PAGE_ROWS
    halves = PAGE_ROWS // CMP_BLOCK
    blk_s = (page_table[..., None] * halves + jnp.arange(halves, dtype=I32)).reshape(-1)
    assert cache_cmp_kv.shape[0] == 1, "one decoder layer"
    kvc_s = _compress(cache_cmp_kv.reshape(-1, HEAD_DIM), None, blk_s, *cmp_params)
    kvc_s = kvc_s.reshape(bs, n_pages * halves, KV_W)
    q_pad = jnp.pad(proj3_s[..., P_Q:P_KV], ((0, 0), (0, QPAD - ls), (0, 0)))
    o_cmp_s, _, picks = _cmp_select(q_pad, 0, kvc_s, rel_bias, QPAD, past)
    o_sel_s = _sel_sample(picks, page_table, rel_bias, q_pad, proj3_s, cache_sel_kv.reshape(-1, HEAD_DIM),
                          past, ls)
    wb = cache_win_kv.shape[2]
    o_win_s, win_roll = _win_sample(rel_bias, q_pad, proj3_s, cache_win_kv.reshape(bs, wb, KV_W), past, ls)
    proj_s_pad = jnp.pad(proj3_s, ((0, 0), (0, QPAD - ls), (0, 0))).reshape(bs * QPAD, P_DIM)
    o_gdn_s, gdn_s, conv_s = _gdn(proj_s_pad, bs, QPAD, ls, state_conv.reshape(bs, CONV_WIDTH - 1, GDN_CONV_DIM),
                                  state_gdn.reshape(bs, GDN_HEADS, HEAD_DIM, HEAD_DIM), *gdn_params, QPAD, QPAD)
    cut = lambda o: o.reshape(bs, QPAD, -1)[:, :ls].reshape(bs * ls, -1)

    w_out_b = w_out[0].astype(BF16)
    x1_p = _outproj(o_gdn_p, o_cmp_p.reshape(bp * lp, -1), o_sel_p, o_win_p, proj_p, xp, w_out_b, ln_mix_post[0],
                    gt1[:bp], lp, 256)
    x1_s = _outproj(cut(o_gdn_s), cut(o_cmp_s), cut(o_sel_s), cut(o_win_s), proj_s, xs, w_out_b, ln_mix_post[0],
                    gt1[bp:], ls, bs * ls)
    y_p, y_s = _moe(x1_p, x1_s, (sc2[:bp], sh2[:bp], gt2[:bp]), (sc2[bp:], sh2[bp:], gt2[bp:]), lp, ls,
                    ln_ffn_pre[0], ln_ffn_post[0], w_router[0], router_bias[0],
                    w_exp_gu.reshape(w_exp_gu.shape[1:]), w_exp_down.reshape(w_exp_down.shape[1:]),
                    w_sh_gu[0].astype(BF16), w_sh_down[0].astype(BF16))

    kv_shape = (2, NSA_KV_HEADS, HEAD_DIM)
    branch = lambda p3, br: p3[..., P_KV + br * KV_W:P_KV + (br + 1) * KV_W]
    win_p = branch(proj3_p, 2)[:, lp - min(WINDOW, lp):]
    return (y_p.reshape(x_prompt.shape), y_s.reshape(x_sample.shape),
            branch(proj3_p, 0).reshape(1, bp, lp, *kv_shape), branch(proj3_s, 0).reshape(1, bs, ls, *kv_shape),
            branch(proj3_p, 1).reshape(1, bp, lp, *kv_shape), branch(proj3_s, 1).reshape(1, bs, ls, *kv_shape),
            win_p.reshape(1, bp, win_p.shape[1], *kv_shape), win_roll.reshape(1, bs, wb, *kv_shape),
            gdn_p[None].astype(state_gdn.dtype), gdn_s[None].astype(state_gdn.dtype),
            conv_p[None].astype(state_conv.dtype), conv_s[None].astype(state_conv.dtype))
```

```python
import functools
import math

import jax
import jax.numpy as jnp
from jax import lax
from jax.experimental import pallas as pl
from jax.experimental.pallas import tpu as pltpu

F32, BF16, I32 = jnp.float32, jnp.bfloat16, jnp.int32

D_MODEL = 2048
HEAD_DIM = 128
LANES = 128
SUBLANES = 8
ROW_SLABS = D_MODEL // LANES
GDN_HEADS = 8
GDN_QK = GDN_HEADS * HEAD_DIM
GDN_CONV_DIM = 3 * GDN_QK
CONV_WIDTH = 4
GDN_CHUNK = 64
NSA_HEADS = 8
NSA_KV_HEADS = 2
NSA_GROUP = NSA_HEADS // NSA_KV_HEADS
N_BRANCH = 3
CMP_BLOCK = 64
CMP_SHIFT = CMP_BLOCK.bit_length() - 1
SEL_TOPK = 16
WINDOW = 512
REL_BUCKETS = 32
REL_MAX_DIST = 8192
MOE_TOPK = 8
N_GROUPS = 8
TOPK_GROUPS = 4
ROUTED_SCALE = 2.5
NORM_EPS = 1e-6
NEG_INF = -1e30
ATTN_SCALE = HEAD_DIM ** -0.5
KV_W = 2 * NSA_KV_HEADS * HEAD_DIM

P_QKV = 0
P_Z = P_QKV + GDN_CONV_DIM
P_Q = P_Z + GDN_QK
P_KV = P_Q + NSA_HEADS * HEAD_DIM
P_SMALL = P_KV + N_BRANCH * KV_W
P_DIM = 7168
SM_BETA, SM_DECAY, SM_GATE = 0, GDN_HEADS, 2 * GDN_HEADS
IN_SIZES = (GDN_CONV_DIM, GDN_QK, GDN_HEADS, GDN_HEADS, NSA_HEADS * HEAD_DIM, N_BRANCH * KV_W, N_BRANCH * NSA_HEADS)

VMEM_LIMIT = 56 * 1024 * 1024
ROW_CHUNK = 32
MOE_MB = 256
ROW_DMA_PRIORITY = 1


def _cparams(*sem):
    return pltpu.CompilerParams(dimension_semantics=sem, vmem_limit_bytes=VMEM_LIMIT)


def _silu(x):
    return x * jax.nn.sigmoid(x)


def _row_chunks(n_rows, body):
    def step(i, carry):
        body(pl.multiple_of(i * ROW_CHUNK, ROW_CHUNK))
        return carry
    lax.fori_loop(0, n_rows // ROW_CHUNK, step, 0)


def _rms(x):
    return x * lax.rsqrt(jnp.mean(x * x, axis=-1, keepdims=True) + NORM_EPS)


def _mod_rows(ref, r0, per_row):
    return ref[pl.ds(r0, ROW_CHUNK), :] if per_row else ref[...]


def _mod_operand(mod, tm, rows_per_batch):
    if rows_per_batch % tm == 0:
        per = rows_per_batch // tm
        return (mod[:, None, :], pl.BlockSpec((None, 1, D_MODEL), lambda i, *_: (i // per, 0, 0)), False)
    return (jnp.repeat(mod, rows_per_batch, axis=0), pl.BlockSpec((tm, D_MODEL), lambda i, *_: (i, 0)), True)


def _ada_kernel(c_ref, w_ref, b_ref, o_ref):
    a = _silu(c_ref[...]).astype(BF16)
    o_ref[...] = jnp.dot(a, w_ref[...].astype(BF16), preferred_element_type=F32) + b_ref[...]


def _ada(c, w_ada, b_ada):
    n = c.shape[0]
    npad = -(-n // SUBLANES) * SUBLANES
    cp = jnp.pad(c, ((0, npad - n), (0, 0)))
    tn = 512
    out = pl.pallas_call(
        _ada_kernel,
        grid=(w_ada.shape[1] // tn,),
        in_specs=[pl.BlockSpec((npad, D_MODEL), lambda j: (0, 0)),
                  pl.BlockSpec((D_MODEL, tn), lambda j: (0, j)),
                  pl.BlockSpec((1, tn), lambda j: (0, j))],
        out_specs=pl.BlockSpec((npad, tn), lambda j: (0, j)),
        out_shape=jax.ShapeDtypeStruct((npad, w_ada.shape[1]), F32),
        compiler_params=_cparams("parallel"),
        name="ada",
    )(cp, w_ada, b_ada[None, :])
    return out[:n]


def _inproj_kernel(x_ref, g_ref, sc_ref, sh_ref, w_ref, o_ref, h_scr, *, tm, per_row):
    @pl.when(pl.program_id(1) == 0)
    def _():
        def body(r0):
            y = _rms(x_ref[pl.ds(r0, ROW_CHUNK), :]) * g_ref[...]
            h = y * (1.0 + _mod_rows(sc_ref, r0, per_row)) + _mod_rows(sh_ref, r0, per_row)
            h_scr[pl.ds(r0, ROW_CHUNK), :] = h.astype(BF16)
        _row_chunks(tm, body)
    o_ref[...] = jnp.dot(h_scr[...], w_ref[...], preferred_element_type=F32)


def _inproj(x, ln_g, sc, sh, w_in_p, rows_per_batch, tm):
    rows = x.shape[0]
    tn = 1024
    sc_a, sc_spec, per_row = _mod_operand(sc, tm, rows_per_batch)
    sh_a, sh_spec, _ = _mod_operand(sh, tm, rows_per_batch)
    return pl.pallas_call(
        functools.partial(_inproj_kernel, tm=tm, per_row=per_row),
        grid=(rows // tm, P_DIM // tn),
        in_specs=[pl.BlockSpec((tm, D_MODEL), lambda i, j: (i, 0)),
                  pl.BlockSpec((1, D_MODEL), lambda i, j: (0, 0)),
                  sc_spec, sh_spec,
                  pl.BlockSpec((D_MODEL, tn), lambda i, j: (0, j))],
        out_specs=pl.BlockSpec((tm, tn), lambda i, j: (i, j)),
        out_shape=jax.ShapeDtypeStruct((rows, P_DIM), F32),
        scratch_shapes=[pltpu.VMEM((tm, D_MODEL), BF16)],
        compiler_params=_cparams("parallel", "arbitrary"),
        name="inproj",
    )(x, ln_g[None, :], sc_a, sh_a, w_in_p)


def _pack_w_in(w_in):
    parts = jnp.split(w_in, list(np_cumsum(IN_SIZES)[:-1]), axis=1)
    qkv, z, b_raw, a_raw, nsa_q, nsa_kv, nsa_g = parts
    small = jnp.concatenate([b_raw, a_raw, nsa_g], axis=1)
    w = jnp.concatenate([qkv, z, nsa_q, nsa_kv, small], axis=1)
    return jnp.pad(w, ((0, 0), (0, P_DIM - w.shape[1]))).astype(BF16)


def np_cumsum(sizes):
    out, acc = [], 0
    for s in sizes:
        acc += s
        out.append(acc)
    return out


def _outproj_kernel(og_ref, oc_ref, os_ref, ow_ref, sm_ref, x_ref, w_ref, g_ref, gt_ref, o_ref,
                    mix_in, mix_out, *, tm, per_row):
    def build(r0):
        rows = pl.ds(r0, ROW_CHUNK)
        mix_in[rows, :GDN_QK] = og_ref[rows, :].astype(BF16)
        gates = jax.nn.sigmoid(sm_ref[rows, :])
        for hd in range(NSA_HEADS):
            cols = slice(hd * HEAD_DIM, (hd + 1) * HEAD_DIM)
            acc = None
            for br, ref in enumerate((oc_ref, os_ref, ow_ref)):
                c = SM_GATE + br * NSA_HEADS + hd
                term = gates[:, c:c + 1] * ref[rows, cols]
                acc = term if acc is None else acc + term
            mix_in[rows, GDN_QK + hd * HEAD_DIM:GDN_QK + (hd + 1) * HEAD_DIM] = acc.astype(BF16)
    _row_chunks(tm, build)
    mix_out[...] = jnp.dot(mix_in[...], w_ref[...], preferred_element_type=F32)

    def finish(r0):
        rows = pl.ds(r0, ROW_CHUNK)
        y = _rms(mix_out[rows, :]) * g_ref[...]
        o_ref[rows, :] = x_ref[rows, :] + _mod_rows(gt_ref, r0, per_row) * y
    _row_chunks(tm, finish)


def _outproj(o_gdn, o_cmp, o_sel, o_win, proj, x, w_out_b, ln_g, gt, rows_per_batch, tm):
    rows = x.shape[0]
    gt_a, gt_spec, per_row = _mod_operand(gt, tm, rows_per_batch)
    head_spec = pl.BlockSpec((tm, GDN_QK), lambda i: (i, 0))
    row_spec = pl.BlockSpec((tm, D_MODEL), lambda i: (i, 0))
    return pl.pallas_call(
        functools.partial(_outproj_kernel, tm=tm, per_row=per_row),
        grid=(rows // tm,),
        in_specs=[head_spec, head_spec, head_spec, head_spec,
                  pl.BlockSpec((tm, LANES), lambda i: (i, P_SMALL // LANES)),
                  row_spec,
                  pl.BlockSpec((D_MODEL, D_MODEL), lambda i: (0, 0)),
                  pl.BlockSpec((1, D_MODEL), lambda i: (0, 0)),
                  gt_spec],
        out_specs=row_spec,
        out_shape=jax.ShapeDtypeStruct((rows, D_MODEL), F32),
        scratch_shapes=[pltpu.VMEM((tm, D_MODEL), BF16), pltpu.VMEM((tm, D_MODEL), F32)],
        compiler_params=_cparams("parallel"),
        name="outproj",
    )(o_gdn, o_cmp, o_sel, o_win, proj, x, w_out_b, ln_g[None, :], gt_a)


def _route_kernel(x_ref, g_ref, sc_ref, sh_ref, wr_ref, rb_ref, h_ref, ei_ref, ew_ref, cnt_ref, h_scr,
                  *, tm, per_row, n_exp):
    def body(r0):
        rows = pl.ds(r0, ROW_CHUNK)
        y = _rms(x_ref[rows, :]) * g_ref[...]
        h = y * (1.0 + _mod_rows(sc_ref, r0, per_row)) + _mod_rows(sh_ref, r0, per_row)
        h_scr[rows, :] = h
        for s in range(ROW_SLABS):
            h_ref[pl.ds(r0 * ROW_SLABS + s, ROW_CHUNK, stride=ROW_SLABS), :] = h[:, s * LANES:(s + 1) * LANES]
    _row_chunks(tm, body)

    logits = lax.dot_general(wr_ref[...], h_scr[...], (((1,), (1,)), ((), ())),
                             precision=lax.Precision.HIGHEST, preferred_element_type=F32)
    s = jax.nn.sigmoid(logits)
    sb = s + rb_ref[...]
    gsz = n_exp // N_GROUPS
    sb3 = sb.reshape(N_GROUPS, gsz, tm)
    m1 = jnp.max(sb3, axis=1)
    n_top = jnp.sum((sb3 == m1[:, None, :]).astype(F32), axis=1)
    m2 = jnp.max(jnp.where(sb3 < m1[:, None, :], sb3, -jnp.inf), axis=1)
    gscore = m1 + jnp.where(n_top >= 2.0, m1, m2)
    gid = lax.broadcasted_iota(I32, (N_GROUPS, tm), 0)
    rank = jnp.zeros((N_GROUPS, tm), F32)
    for g in range(N_GROUPS):
        row = gscore[g:g + 1, :]
        ahead = (row > gscore) | ((row == gscore) & (g < gid))
        rank = rank + ahead.astype(F32)
    gsel = rank < float(TOPK_GROUPS)
    emask = jnp.broadcast_to(gsel[:, None, :], (N_GROUPS, gsz, tm)).reshape(n_exp, tm)
    v = jnp.where(emask, sb, NEG_INF)
    eid = lax.broadcasted_iota(I32, (n_exp, tm), 0)
    idxs, wts = [], []
    taken = jnp.zeros((n_exp, tm), F32)
    for _ in range(MOE_TOPK):
        m = jnp.max(v, axis=0, keepdims=True)
        idx = jnp.min(jnp.where(v == m, eid, n_exp), axis=0, keepdims=True)
        hit = eid == idx
        wts.append(jnp.sum(jnp.where(hit, s, 0.0), axis=0, keepdims=True))
        idxs.append(idx)
        taken = taken + hit.astype(F32)
        v = jnp.where(hit, -jnp.inf, v)
    w = jnp.concatenate(wts, axis=0)
    ei_ref[...] = jnp.concatenate(idxs, axis=0)
    ew_ref[...] = w / jnp.sum(w, axis=0, keepdims=True) * ROUTED_SCALE
    cnt_ref[...] = jnp.sum(taken, axis=1, keepdims=True)


def _route(x1, ln_g, sc, sh, w_router, router_bias, rows_per_batch, tm):
    rows = x1.shape[0]
    n_exp = w_router.shape[0]
    sc_a, sc_spec, per_row = _mod_operand(sc, tm, rows_per_batch)
    sh_a, sh_spec, _ = _mod_operand(sh, tm, rows_per_batch)
    return pl.pallas_call(
        functools.partial(_route_kernel, tm=tm, per_row=per_row, n_exp=n_exp),
        grid=(rows // tm,),
        in_specs=[pl.BlockSpec((tm, D_MODEL), lambda i: (i, 0)),
                  pl.BlockSpec((1, D_MODEL), lambda i: (0, 0)),
                  sc_spec, sh_spec,
                  pl.BlockSpec((n_exp, D_MODEL), lambda i: (0, 0)),
                  pl.BlockSpec((n_exp, 1), lambda i: (0, 0))],
        out_specs=[pl.BlockSpec((tm * ROW_SLABS, LANES), lambda i: (i, 0)),
                   pl.BlockSpec((MOE_TOPK, tm), lambda i: (0, i)),
                   pl.BlockSpec((MOE_TOPK, tm), lambda i: (0, i)),
                   pl.BlockSpec((None, n_exp, 1), lambda i: (i, 0, 0))],
        out_shape=[jax.ShapeDtypeStruct((rows * ROW_SLABS, LANES), F32),
                   jax.ShapeDtypeStruct((MOE_TOPK, rows), I32),
                   jax.ShapeDtypeStruct((MOE_TOPK, rows), F32),
                   jax.ShapeDtypeStruct((rows // tm, n_exp, 1), F32)],
        scratch_shapes=[pltpu.VMEM((tm, D_MODEL), F32)],
        compiler_params=_cparams("parallel"),
        name="route",
    )(x1, ln_g[None, :], sc_a, sh_a, w_router, router_bias[:, None])


EXPERT_SPLIT = 4


def _expert_kernel(blk_e_ref, nused_ref, src_ref, nxt_ref, dst_ref, prv_ref, rw_ref, wgu_ref, wd_ref, h_hbm,
                   y_hbm, xbuf, xmat, ybuf, gsem, ssem, *, mb, ff, dump0):
    i = pl.program_id(0)
    nused = nused_ref[0]
    slot = lax.rem(i, 2)
    slab_rows = mb * ROW_SLABS
    group = mb // EXPERT_SPLIT

    def gather(idx_ref, to_slot, r):
        tok = idx_ref[0, 0, r]
        return pltpu.make_async_copy(
            h_hbm.at[pl.ds(pl.multiple_of(tok * ROW_SLABS, ROW_SLABS), ROW_SLABS), :],
            xbuf.at[pl.ds(pl.multiple_of(to_slot * slab_rows + r * ROW_SLABS, ROW_SLABS), ROW_SLABS), :],
            gsem.at[to_slot])

    def scatter(idx_ref, from_slot, r, live):
        row = jnp.where(live, idx_ref[0, 0, r], dump0 + r)
        return pltpu.make_async_copy(
            ybuf.at[pl.ds(pl.multiple_of(from_slot * slab_rows + r * ROW_SLABS, ROW_SLABS), ROW_SLABS), :],
            y_hbm.at[pl.ds(pl.multiple_of(row * ROW_SLABS, ROW_SLABS), ROW_SLABS), :],
            ssem.at[from_slot])

    @pl.when(i == 0)
    def _():
        for r in range(mb):
            gather(src_ref, 0, r).start(priority=ROW_DMA_PRIORITY)
        ybuf[...] = jnp.zeros_like(ybuf)
        fill = pltpu.make_async_copy(ybuf.at[pl.ds(0, slab_rows), :],
                                     y_hbm.at[pl.ds(dump0 * ROW_SLABS, slab_rows), :], ssem.at[0])
        fill.start()
        fill.wait()

    @pl.when(i < nused)
    def _():
        for r in range(mb):
            gather(src_ref, slot, r).wait()

        @pl.when(i >= 1)
        def _():
            for r in range(mb):
                scatter(dst_ref, slot, r, True).wait()

        def issue_neighbours(part):
            for r in range(part * group // 2, (part + 1) * group // 2):
                gather(nxt_ref, 1 - slot, r).start(priority=ROW_DMA_PRIORITY)
                scatter(prv_ref, 1 - slot, r, i > 0).start(priority=ROW_DMA_PRIORITY)

        base = pl.multiple_of(slot * slab_rows, slab_rows)
        for s in range(ROW_SLABS):
            xmat[:, s * LANES:(s + 1) * LANES] = xbuf[pl.ds(base + s, mb, stride=ROW_SLABS), :].astype(BF16)
        x = xmat[...]
        n_gu = 2 * ff // EXPERT_SPLIT
        gu = []
        for c in range(EXPERT_SPLIT):
            gu.append(jnp.dot(x, wgu_ref[:, c * n_gu:(c + 1) * n_gu].astype(BF16), preferred_element_type=F32))
            issue_neighbours(c)
        gu = jnp.concatenate(gu, axis=1)
        hid = (_silu(gu[:, :ff]) * gu[:, ff:]).astype(BF16)
        rw = rw_ref[...]
        n_y = D_MODEL // EXPERT_SPLIT
        for c in range(EXPERT_SPLIT):
            y = jnp.dot(hid, wd_ref[:, c * n_y:(c + 1) * n_y].astype(BF16), preferred_element_type=F32) * rw
            for s in range(n_y // LANES):
                ybuf[pl.ds(base + c * (n_y // LANES) + s, mb, stride=ROW_SLABS), :] = y[:, s * LANES:(s + 1) * LANES]
            issue_neighbours(EXPERT_SPLIT + c)

        @pl.when(i == nused - 1)
        def _():
            for r in range(mb):
                scatter(prv_ref, 1 - slot, r, True).wait()
            for r in range(mb):
                scatter(dst_ref, slot, r, True).start(priority=ROW_DMA_PRIORITY)
            for r in range(mb):
                scatter(dst_ref, slot, r, True).wait()
            for r in range(mb):
                gather(nxt_ref, 1 - slot, r).wait()


def _dispatch_plan(eidx, ew, counts, n_exp, mb):
    t_all = eidx.shape[0]
    n_asg = t_all * MOE_TOPK
    n_blk = -(-(n_asg + n_exp * (mb - 1)) // mb)
    n_slot = n_blk * mb
    plane_rows = t_all
    pad = (-counts) % mb
    asg = jnp.arange(n_asg, dtype=I32)
    last_key = 2 * n_exp
    pad_key = jnp.where(jnp.arange(mb - 1, dtype=I32)[None, :] < pad[:, None],
                        2 * jnp.arange(n_exp, dtype=I32)[:, None] + 1, last_key).reshape(-1)
    n_fill = n_slot - n_asg
    keys = jnp.concatenate([2 * eidx.reshape(-1), pad_key, jnp.full((n_fill - pad_key.shape[0],), last_key, I32)])
    src = jnp.concatenate([asg // MOE_TOPK, jnp.zeros((n_fill,), I32)])
    dst = jnp.concatenate([(asg % MOE_TOPK) * plane_rows + asg // MOE_TOPK, jnp.full((n_fill,), -1, I32)])
    wts = jnp.concatenate([ew.reshape(-1), jnp.zeros((n_fill,), F32)])
    _, rows_src, rows_dst, rows_w = lax.sort((keys, src, dst, wts), num_keys=1, is_stable=True)
    slot_id = jnp.arange(n_slot, dtype=I32)
    rows_dst = jnp.where(rows_dst < 0, MOE_TOPK * plane_rows + slot_id % mb, rows_dst)
    pends = jnp.cumsum(counts + pad)
    blk_e = jnp.minimum(jnp.searchsorted(pends, jnp.arange(n_blk, dtype=I32) * mb, side='right'),
                        n_exp - 1).astype(I32)
    nused = (pends[-1] // mb).astype(I32).reshape(1)
    return n_blk, plane_rows, rows_src, rows_dst, rows_w, blk_e, nused


def _experts(h_slabs, eidx, ew, counts, w_gu, w_down):
    n_exp, _, ff2 = w_gu.shape
    ff = ff2 // 2
    mb = MOE_MB
    n_blk, plane_rows, rows_src, rows_dst, rows_w, blk_e, nused = _dispatch_plan(eidx, ew, counts, n_exp, mb)
    idx_spec = lambda f: pl.BlockSpec((1, 1, mb), f, memory_space=pltpu.SMEM)
    y = pl.pallas_call(
        functools.partial(_expert_kernel, mb=mb, ff=ff, dump0=MOE_TOPK * plane_rows),
        grid_spec=pltpu.PrefetchScalarGridSpec(
            num_scalar_prefetch=2,
            grid=(n_blk,),
            in_specs=[idx_spec(lambda i, be, nu: (i, 0, 0)),
                      idx_spec(lambda i, be, nu: (jnp.minimum(i + 1, n_blk - 1), 0, 0)),
                      idx_spec(lambda i, be, nu: (i, 0, 0)),
                      idx_spec(lambda i, be, nu: (jnp.maximum(i - 1, 0), 0, 0)),
                      pl.BlockSpec((mb, 1), lambda i, be, nu: (i, 0)),
                      pl.BlockSpec((None, D_MODEL, ff2), lambda i, be, nu: (be[i], 0, 0)),
                      pl.BlockSpec((None, ff, D_MODEL), lambda i, be, nu: (be[i], 0, 0)),
                      pl.BlockSpec(memory_space=pl.ANY)],
            out_specs=pl.BlockSpec(memory_space=pl.ANY),
            scratch_shapes=[pltpu.VMEM((2 * mb * ROW_SLABS, LANES), F32),
                            pltpu.VMEM((mb, D_MODEL), BF16),
                            pltpu.VMEM((2 * mb * ROW_SLABS, LANES), F32),
                            pltpu.SemaphoreType.DMA((2,)),
                            pltpu.SemaphoreType.DMA((2,))]),
        out_shape=jax.ShapeDtypeStruct(((MOE_TOPK * plane_rows + mb) * ROW_SLABS, LANES), F32),
        compiler_params=_cparams("arbitrary"),
        name="experts",
    )(blk_e, nused, rows_src.reshape(n_blk, 1, mb), rows_src.reshape(n_blk, 1, mb),
      rows_dst.reshape(n_blk, 1, mb), rows_dst.reshape(n_blk, 1, mb), rows_w[:, None], w_gu, w_down, h_slabs)
    return y


def _combine_kernel(*refs, tm, per_row, ff):
    y_refs = refs[:MOE_TOPK]
    h_ref, wgu_ref, wd_ref, x_ref, g_ref, gt_ref, o_ref, fsum, hmat, f_scr = refs[MOE_TOPK:]

    def add_planes(r0):
        rows = pl.ds(r0 * ROW_SLABS, ROW_CHUNK * ROW_SLABS)
        acc = y_refs[0][rows, :]
        for y_ref in y_refs[1:]:
            acc = acc + y_ref[rows, :]
        fsum[rows, :] = acc
    _row_chunks(tm, add_planes)
    for s in range(ROW_SLABS):
        hmat[:, s * LANES:(s + 1) * LANES] = h_ref[pl.ds(s, tm, stride=ROW_SLABS), :].astype(BF16)
    gu = jnp.dot(hmat[...], wgu_ref[...], preferred_element_type=F32)
    hid = (_silu(gu[:, :ff]) * gu[:, ff:]).astype(BF16)
    f_scr[...] = jnp.dot(hid, wd_ref[...], preferred_element_type=F32)
    for s in range(ROW_SLABS):
        f_scr[:, s * LANES:(s + 1) * LANES] += fsum[pl.ds(s, tm, stride=ROW_SLABS), :]

    def finish(r0):
        rows = pl.ds(r0, ROW_CHUNK)
        y = _rms(f_scr[rows, :]) * g_ref[...]
        o_ref[rows, :] = x_ref[rows, :] + _mod_rows(gt_ref, r0, per_row) * y
    _row_chunks(tm, finish)


def _combine(y_planes, h_slabs, row0, w_sh_gu_b, w_sh_down_b, x1, ln_g, gt, rows_per_batch, tm):
    rows = x1.shape[0]
    ff = w_sh_down_b.shape[0]
    t0 = row0 // tm
    plane_tiles = h_slabs.shape[0] // (tm * ROW_SLABS)
    gt_a, gt_spec, per_row = _mod_operand(gt, tm, rows_per_batch)
    row_spec = pl.BlockSpec((tm, D_MODEL), lambda i: (i, 0))
    plane_specs = [pl.BlockSpec((tm * ROW_SLABS, LANES), lambda i, k=k: (k * plane_tiles + t0 + i, 0))
                   for k in range(MOE_TOPK)]
    return pl.pallas_call(
        functools.partial(_combine_kernel, tm=tm, per_row=per_row, ff=ff),
        grid=(rows // tm,),
        in_specs=plane_specs + [
                  pl.BlockSpec((tm * ROW_SLABS, LANES), lambda i: (t0 + i, 0)),
                  pl.BlockSpec((D_MODEL, 2 * ff), lambda i: (0, 0)),
                  pl.BlockSpec((ff, D_MODEL), lambda i: (0, 0)),
                  row_spec,
                  pl.BlockSpec((1, D_MODEL), lambda i: (0, 0)),
                  gt_spec],
        out_specs=row_spec,
        out_shape=jax.ShapeDtypeStruct((rows, D_MODEL), F32),
        scratch_shapes=[pltpu.VMEM((tm * ROW_SLABS, LANES), F32),
                        pltpu.VMEM((tm, D_MODEL), BF16),
                        pltpu.VMEM((tm, D_MODEL), F32)],
        compiler_params=_cparams("parallel"),
        name="combine",
    )(*([y_planes] * MOE_TOPK), h_slabs, w_sh_gu_b, w_sh_down_b, x1, ln_g[None, :], gt_a)


def _moe(x1_p, x1_s, mod_p, mod_s, rpb_p, rpb_s, ln_pre, ln_post, w_router, router_bias, w_gu, w_down,
         w_sh_gu_b, w_sh_down_b):
    tm_s = x1_s.shape[0]
    h_p, ei_p, ew_p, cnt_p = _route(x1_p, ln_pre, mod_p[0], mod_p[1], w_router, router_bias, rpb_p, 256)
    h_s, ei_s, ew_s, cnt_s = _route(x1_s, ln_pre, mod_s[0], mod_s[1], w_router, router_bias, rpb_s, tm_s)
    h_all = jnp.concatenate([h_p, h_s], axis=0)
    eidx = jnp.concatenate([ei_p, ei_s], axis=1).T
    ew = jnp.concatenate([ew_p, ew_s], axis=1).T
    counts = (jnp.sum(cnt_p, axis=(0, 2)) + jnp.sum(cnt_s, axis=(0, 2))).astype(I32)
    y4 = _experts(h_all, eidx, ew, counts, w_gu, w_down)
    out_p = _combine(y4, h_all, 0, w_sh_gu_b, w_sh_down_b, x1_p, ln_post, mod_p[2], rpb_p, 128)
    out_s = _combine(y4, h_all, x1_p.shape[0], w_sh_gu_b, w_sh_down_b, x1_s, ln_post, mod_s[2], rpb_s, tm_s)
    return out_p, out_s


def _t5_bucket(dist):
    n = jnp.maximum(dist, 0)
    max_exact = REL_BUCKETS // 2
    nf = jnp.maximum(n, 1).astype(F32)
    large = max_exact + (jnp.log(nf / max_exact) / math.log(REL_MAX_DIST / max_exact)
                         * (REL_BUCKETS - max_exact)).astype(I32)
    return jnp.where(n < max_exact, n, jnp.minimum(large, REL_BUCKETS - 1))


def _bucket_bias(bucket, rel_ref, head):
    out = jnp.zeros(bucket.shape, F32)
    for b in range(REL_BUCKETS):
        out = jnp.where(bucket == b, rel_ref[b, head], out)
    return out


def _bias_tile_kernel(rel_ref, o_ref, *, d_min, lo, hi):
    d = pl.program_id(0) + d_min
    row = lax.broadcasted_iota(I32, (LANES, LANES), 0)
    col = lax.broadcasted_iota(I32, (LANES, LANES), 1)
    dist = d * LANES + row - col
    bucket = _t5_bucket(dist)
    visible = (dist >= lo) & (dist < hi)
    for h in range(NSA_HEADS):
        o_ref[h, 0] = jnp.where(visible, _bucket_bias(bucket, rel_ref, h), NEG_INF)


def _bias_tiles(rel_bias, d_min, n_d, lo, hi):
    return pl.pallas_call(
        functools.partial(_bias_tile_kernel, d_min=d_min, lo=lo, hi=hi),
        grid=(n_d,),
        in_specs=[pl.BlockSpec(memory_space=pltpu.SMEM)],
        out_specs=pl.BlockSpec((NSA_HEADS, 1, LANES, LANES), lambda d: (0, d, 0, 0)),
        out_shape=jax.ShapeDtypeStruct((NSA_HEADS, n_d, LANES, LANES), F32),
        compiler_params=_cparams("parallel"),
        name="bias_tiles",
    )(rel_bias)


CMP_ROWS = 128


def _cmp_select_kernel(rel_ref, q_ref, k_ref, v_ref, o_ref, neg_ref, idx_ref, *, tq, nb, pos0):
    hkv = pl.program_id(1)
    rq = min(tq, CMP_ROWS)
    k = k_ref[...].astype(BF16)
    v = v_ref[...].astype(BF16)
    blk = lax.broadcasted_iota(I32, (rq, nb), 1)
    blkf = blk.astype(F32)
    tile0 = pl.program_id(2) * tq

    def step(c):
        r0 = c * rq
        rows = slice(r0, r0 + rq)
        qpos = pos0 + tile0 + r0 + lax.broadcasted_iota(I32, (rq, nb), 0)
        dist = qpos - (blk * CMP_BLOCK + (CMP_BLOCK - 1))
        bucket = _t5_bucket(dist)
        seen = dist >= 0
        psum = jnp.zeros((rq, nb), F32)
        for g in range(NSA_GROUP):
            cols = slice(g * HEAD_DIM, (g + 1) * HEAD_DIM)
            s = lax.dot_general(q_ref[rows, cols].astype(BF16), k, (((1,), (1,)), ((), ())),
                                preferred_element_type=F32) * ATTN_SCALE
            bias = jnp.zeros((rq, nb), F32)
            for b in range(REL_BUCKETS):
                bias = jnp.where(bucket == b, rel_ref[b, hkv * NSA_GROUP + g], bias)
            s = jnp.where(seen, s + bias, NEG_INF)
            e = jnp.exp(s - jnp.max(s, axis=1, keepdims=True))
            p = e / jnp.sum(e, axis=1, keepdims=True) * seen.astype(F32)
            o_ref[rows, cols] = jnp.dot(p.astype(BF16), v, preferred_element_type=F32)
            psum = psum + p
        cur = lax.shift_right_logical(qpos, CMP_SHIFT)
        score = jnp.where(blk < cur, psum, -1.0)
        chosen = blk == cur
        lane = lax.broadcasted_iota(I32, (rq, SEL_TOPK), 1)
        picks = jnp.where(lane == 0, cur[:, :SEL_TOPK].astype(F32), -1.0)
        for r in range(1, SEL_TOPK):
            m = jnp.max(score, axis=1, keepdims=True)
            first = jnp.min(jnp.where(score == m, blkf, float(nb)), axis=1, keepdims=True)
            hit = blkf == first
            ok = m >= 0.0
            chosen = chosen | (hit & ok)
            picks = jnp.where(lane == r, jnp.where(ok, first, -1.0), picks)
            score = jnp.where(hit, -2.0, score)
        neg_ref[rows, :] = jnp.where(chosen, 0.0, NEG_INF).astype(BF16)
        idx_ref[rows, :] = picks.astype(I32)

    for c in range(tq // rq):
        step(c)


def _cmp_select(q3, col_blk0, kvc, rel_bias, tq, pos0):
    b, lq, _ = q3.shape
    nb = kvc.shape[1]
    gw = NSA_GROUP * HEAD_DIM
    return pl.pallas_call(
        functools.partial(_cmp_select_kernel, tq=tq, nb=nb, pos0=pos0),
        grid=(b, NSA_KV_HEADS, lq // tq),
        in_specs=[pl.BlockSpec(memory_space=pltpu.SMEM),
                  pl.BlockSpec((None, tq, gw), lambda i, h, t: (i, t, col_blk0 + h)),
                  pl.BlockSpec((None, nb, HEAD_DIM), lambda i, h, t: (i, 0, h)),
                  pl.BlockSpec((None, nb, HEAD_DIM), lambda i, h, t: (i, 0, NSA_KV_HEADS + h))],
        out_specs=[pl.BlockSpec((None, tq, gw), lambda i, h, t: (i, t, h)),
                   pl.BlockSpec((None, None, tq, nb), lambda i, h, t: (h, i, t, 0)),
                   pl.BlockSpec((None, None, tq, SEL_TOPK), lambda i, h, t: (h, i, t, 0))],
        out_shape=[jax.ShapeDtypeStruct((b, lq, NSA_HEADS * HEAD_DIM), F32),
                   jax.ShapeDtypeStruct((NSA_KV_HEADS, b, lq, nb), BF16),
                   jax.ShapeDtypeStruct((NSA_KV_HEADS, b, lq, SEL_TOPK), I32)],
        compiler_params=_cparams("parallel", "parallel", "parallel"),
        name="cmp_select",
    )(rel_bias, q3, kvc, kvc)


ATT_T = 512
ATT_SUB = 128


def _flash_kernel(*refs, selected, d_min):
    if selected:
        q_ref, neg_ref, k_ref, v_ref, t_ref, o_ref, m_scr, l_scr, acc_scr = refs
    else:
        q_ref, k_ref, v_ref, t_ref, o_ref, m_scr, l_scr, acc_scr = refs
    qi, kk = pl.program_id(2), pl.program_id(3)
    kj = kk if selected else qi - 1 + kk
    nsub = ATT_T // ATT_SUB

    @pl.when(kk == 0)
    def _():
        m_scr[...] = jnp.full(m_scr.shape, NEG_INF, F32)
        l_scr[...] = jnp.zeros(l_scr.shape, F32)
        acc_scr[...] = jnp.zeros(acc_scr.shape, F32)

    @pl.when((kj >= 0) & (kj <= qi))
    def _():
        kb = k_ref[...].astype(BF16)
        if selected:
            nb = neg_ref.shape[1]
            key_blk = lax.shift_right_logical(kj * ATT_T + lax.broadcasted_iota(I32, (ATT_T, nb), 0), CMP_SHIFT)
            onehot = jnp.where(key_blk == lax.broadcasted_iota(I32, (ATT_T, nb), 1), 1.0, 0.0)
            kb = jnp.concatenate([kb, onehot.astype(BF16)], axis=1)
        vb = v_ref[...].astype(BF16)
        d0 = (qi - kj) * nsub - d_min
        subs = range(nsub)
        rows = [slice(a * ATT_SUB, (a + 1) * ATT_SUB) for a in subs]
        qa = [jnp.concatenate([q_ref[rows[a], g * HEAD_DIM:(g + 1) * HEAD_DIM] for g in range(NSA_GROUP)],
                              axis=0).astype(BF16) for a in subs]
        if selected:
            qa = [jnp.concatenate([qa[a], jnp.concatenate([neg_ref[rows[a], :]] * NSA_GROUP, axis=0)], axis=1)
                  for a in subs]
        s = [lax.dot_general(qa[a], kb, (((1,), (1,)), ((), ())), preferred_element_type=F32) for a in subs]
        bias = [jnp.concatenate([jnp.concatenate([t_ref[g, d0 + a - c] for c in subs], axis=1)
                                 for g in range(NSA_GROUP)], axis=0) for a in subs]
        s = [s[a] * ATTN_SCALE + bias[a] for a in subs]
        m_prev = [m_scr[a] for a in subs]
        m_new = [jnp.maximum(m_prev[a], jnp.max(s[a], axis=1, keepdims=True)) for a in subs]
        alpha = [jnp.exp(m_prev[a] - m_new[a]) for a in subs]
        p = [jnp.exp(s[a] - m_new[a]) for a in subs]
        pv = [jnp.dot(p[a].astype(BF16), vb, preferred_element_type=F32) for a in subs]
        for a in subs:
            l_scr[a] = alpha[a] * l_scr[a] + jnp.sum(p[a], axis=1, keepdims=True)
            acc_scr[a] = alpha[a] * acc_scr[a] + pv[a]
            m_scr[a] = m_new[a]

    @pl.when(kk == pl.num_programs(3) - 1)
    def _():
        for a in range(nsub):
            out = acc_scr[a] / l_scr[a]
            for g in range(NSA_GROUP):
                o_ref[a * ATT_SUB:(a + 1) * ATT_SUB, g * HEAD_DIM:(g + 1) * HEAD_DIM] = out[g * ATT_SUB:(g + 1) * ATT_SUB]


def _flash(proj, neg, tiles, b, l, branch, d_min):
    selected = neg is not None
    nq = l // ATT_T
    nk = nq if selected else 2
    gw = NSA_GROUP * HEAD_DIM
    n_d = tiles.shape[1]
    k_col = (P_KV + branch * KV_W) // HEAD_DIM

    def kj_of(qi, kk):
        return jnp.clip(kk if selected else qi - 1 + kk, 0, qi)

    in_specs = [pl.BlockSpec((ATT_T, gw), lambda i, h, qi, kk: (i * nq + qi, P_Q // gw + h))]
    args = [proj]
    if selected:
        in_specs.append(pl.BlockSpec((None, ATT_T, neg.shape[2]), lambda i, h, qi, kk: (h, i * nq + qi, 0)))
        args.append(neg)
    in_specs += [pl.BlockSpec((ATT_T, HEAD_DIM), lambda i, h, qi, kk: (i * nq + kj_of(qi, kk), k_col + h)),
                 pl.BlockSpec((ATT_T, HEAD_DIM),
                              lambda i, h, qi, kk: (i * nq + kj_of(qi, kk), k_col + NSA_KV_HEADS + h)),
                 pl.BlockSpec((NSA_GROUP, n_d, LANES, LANES), lambda i, h, qi, kk: (h, 0, 0, 0))]
    args += [proj, proj, tiles]
    return pl.pallas_call(
        functools.partial(_flash_kernel, selected=selected, d_min=d_min),
        grid=(b, NSA_KV_HEADS, nq, nk),
        in_specs=in_specs,
        out_specs=pl.BlockSpec((ATT_T, gw), lambda i, h, qi, kk: (i * nq + qi, h)),
        out_shape=jax.ShapeDtypeStruct((b * l, NSA_HEADS * HEAD_DIM), F32),
        scratch_shapes=[pltpu.VMEM((ATT_T // ATT_SUB, NSA_GROUP * ATT_SUB, 1), F32),
                        pltpu.VMEM((ATT_T // ATT_SUB, NSA_GROUP * ATT_SUB, 1), F32),
                        pltpu.VMEM((ATT_T // ATT_SUB, NSA_GROUP * ATT_SUB, HEAD_DIM), F32)],
        compiler_params=_cparams("parallel", "parallel", "parallel", "arbitrary"),
        name="flash_sel" if selected else "flash_win",
    )(*args)


QPAD = SUBLANES
NEW_PAD = LANES


def _masked_attend(s, mask, parts):
    s = [jnp.where(m, x, NEG_INF) for x, m in zip(s, mask)]
    top = s[0].max(axis=1, keepdims=True)
    for x in s[1:]:
        top = jnp.maximum(top, x.max(axis=1, keepdims=True))
    e = [jnp.exp(x - top) for x in s]
    den = sum(x.sum(axis=1, keepdims=True) for x in e)
    out = None
    for x, m, v in zip(e, mask, parts):
        term = jnp.dot((x / den * m.astype(F32)).astype(BF16), v, preferred_element_type=F32)
        out = term if out is None else out + term
    return out


def _group_rows(q_ref, hkv):
    return jnp.concatenate([q_ref[:, (hkv * NSA_GROUP + g) * HEAD_DIM:(hkv * NSA_GROUP + g + 1) * HEAD_DIM]
                            for g in range(NSA_GROUP)], axis=0).astype(BF16)


def _rows_bias(bucket, rel_ref, hkv):
    return jnp.concatenate([_bucket_bias(bucket[g * QPAD:(g + 1) * QPAD], rel_ref, hkv * NSA_GROUP + g)
                            for g in range(NSA_GROUP)], axis=0)


def _sel_sample_kernel(pick_ref, page_ref, rel_ref, q_ref, new_ref, kpos_ref, own_ref, pool_hbm, o_ref,
                       blkbuf, kbuf, vbuf, sem, *, n_b, n_tok, n_pick, past, pages_per_seq):
    i = pl.program_id(0)
    n_keys = n_tok * n_pick * CMP_BLOCK
    half_pages = PAGE_ROWS // CMP_BLOCK
    blk_rows = CMP_BLOCK * N_KV_SLABS

    def fetch(b, hkv, tok, j):
        blk = jnp.maximum(pick_ref[((hkv * n_b + b) * QPAD + tok) * SEL_TOPK + 1 + j], 0)
        page = page_ref[b * pages_per_seq + blk // half_pages]
        row0 = pl.multiple_of((page * half_pages + lax.rem(blk, half_pages)) * blk_rows, blk_rows)
        p = (hkv * n_tok + tok) * n_pick + j
        return pltpu.make_async_copy(pool_hbm.at[pl.ds(row0, blk_rows), :],
                                     blkbuf.at[pl.ds(p * blk_rows, blk_rows), :], sem.at[0])

    def fetch_all(b, wait):
        for hkv in range(NSA_KV_HEADS):
            for tok in range(n_tok):
                for j in range(n_pick):
                    cp = fetch(b, hkv, tok, j)
                    cp.wait() if wait else cp.start()

    @pl.when(i == 0)
    def _():
        fetch_all(0, False)
        zeros = jnp.zeros((NEW_PAD, HEAD_DIM), F32)
        for hkv in range(NSA_KV_HEADS):
            kbuf[hkv, n_keys:n_keys + NEW_PAD, :] = zeros
            vbuf[hkv, n_keys:n_keys + NEW_PAD, :] = zeros

    fetch_all(i, True)
    for hkv in range(NSA_KV_HEADS):
        for p in range(n_tok * n_pick):
            base = (hkv * n_tok * n_pick + p) * blk_rows
            keys = pl.ds(p * CMP_BLOCK, CMP_BLOCK)
            kbuf[hkv, keys, :] = blkbuf[pl.ds(base + hkv, CMP_BLOCK, stride=N_KV_SLABS), :]
            vbuf[hkv, keys, :] = blkbuf[pl.ds(base + NSA_KV_HEADS + hkv, CMP_BLOCK, stride=N_KV_SLABS), :]

    @pl.when(i + 1 < n_b)
    def _():
        fetch_all(i + 1, False)

    kpos = kpos_ref[...]
    n_all = n_keys + NEW_PAD
    tok_of_row = lax.rem(lax.broadcasted_iota(I32, (NSA_GROUP * QPAD, n_all), 0), QPAD)
    for hkv in range(NSA_KV_HEADS):
        kbuf[hkv, n_keys:n_keys + n_tok, :] = new_ref[:, hkv * HEAD_DIM:(hkv + 1) * HEAD_DIM]
        vbuf[hkv, n_keys:n_keys + n_tok, :] = new_ref[:, (NSA_KV_HEADS + hkv) * HEAD_DIM:
                                                      (NSA_KV_HEADS + hkv + 1) * HEAD_DIM]
        q = _group_rows(q_ref, hkv)
        s = lax.dot_general(q, kbuf[hkv].astype(BF16), (((1,), (1,)), ((), ())),
                            preferred_element_type=F32) * ATTN_SCALE
        kp = kpos[hkv:hkv + 1, :]
        own = own_ref[hkv:hkv + 1, :]
        dist = past + tok_of_row - kp
        mask = (kp >= 0) & (dist >= 0) & ((own < 0) | (own == tok_of_row))
        s = s + _rows_bias(_t5_bucket(dist), rel_ref, hkv)
        out = _masked_attend([s], [mask], [vbuf[hkv].astype(BF16)])
        for g in range(NSA_GROUP):
            c = (hkv * NSA_GROUP + g) * HEAD_DIM
            o_ref[:, c:c + HEAD_DIM] = out[g * QPAD:(g + 1) * QPAD, :]


PAGE_ROWS = 128


def _sel_sample(picks, page_table, rel_bias, q_pad, proj3, pool2d, past, n_tok):
    n_b = q_pad.shape[0]
    n_pick = SEL_TOPK - 1
    n_keys = n_tok * n_pick * CMP_BLOCK
    blk = picks[:, :, :n_tok, 1:]
    kpos = jnp.where(blk[..., None] >= 0, blk[..., None] * CMP_BLOCK + jnp.arange(CMP_BLOCK, dtype=I32), -1)
    kpos = kpos.transpose(1, 0, 2, 3, 4).reshape(n_b, NSA_KV_HEADS, n_keys)
    new_pos = jnp.where(jnp.arange(NEW_PAD) < n_tok, past + jnp.arange(NEW_PAD), -1).astype(I32)
    kpos = jnp.concatenate([kpos, jnp.broadcast_to(new_pos, (n_b, NSA_KV_HEADS, NEW_PAD))], axis=2)
    own = jnp.concatenate([jnp.repeat(jnp.arange(n_tok, dtype=I32), n_pick * CMP_BLOCK),
                           jnp.full((NEW_PAD,), -1, I32)])
    own = jnp.broadcast_to(own, (NSA_KV_HEADS, n_keys + NEW_PAD))
    sel_col = (P_KV + KV_W) // KV_W
    return pl.pallas_call(
        functools.partial(_sel_sample_kernel, n_b=n_b, n_tok=n_tok, n_pick=n_pick, past=past,
                          pages_per_seq=page_table.shape[1]),
        grid_spec=pltpu.PrefetchScalarGridSpec(
            num_scalar_prefetch=2,
            grid=(n_b,),
            in_specs=[pl.BlockSpec(memory_space=pltpu.SMEM),
                      pl.BlockSpec((None, QPAD, NSA_HEADS * HEAD_DIM), lambda i, pk, pg: (i, 0, 0)),
                      pl.BlockSpec((None, n_tok, KV_W), lambda i, pk, pg: (i, 0, sel_col)),
                      pl.BlockSpec((None, NSA_KV_HEADS, n_keys + NEW_PAD), lambda i, pk, pg: (i, 0, 0)),
                      pl.BlockSpec((NSA_KV_HEADS, n_keys + NEW_PAD), lambda i, pk, pg: (0, 0)),
                      pl.BlockSpec(memory_space=pl.ANY)],
            out_specs=pl.BlockSpec((None, QPAD, NSA_HEADS * HEAD_DIM), lambda i, pk, pg: (i, 0, 0)),
            scratch_shapes=[pltpu.VMEM((NSA_KV_HEADS * n_tok * n_pick * CMP_BLOCK * N_KV_SLABS, HEAD_DIM), F32),
                            pltpu.VMEM((NSA_KV_HEADS, n_keys + NEW_PAD, HEAD_DIM), F32),
                            pltpu.VMEM((NSA_KV_HEADS, n_keys + NEW_PAD, HEAD_DIM), F32),
                            pltpu.SemaphoreType.DMA((1,))]),
        out_shape=jax.ShapeDtypeStruct((n_b, QPAD, NSA_HEADS * HEAD_DIM), F32),
        compiler_params=_cparams("arbitrary"),
        name="sel_sample",
    )(picks.reshape(-1), page_table.reshape(-1), rel_bias, q_pad, proj3, kpos, own, pool2d)


def _win_sample_kernel(rel_ref, q_ref, new_ref, buf_ref, o_ref, roll_ref, new_pad, *, n_tok, past):
    wb = buf_ref.shape[0]
    new_pad[...] = jnp.zeros(new_pad.shape, F32)
    new_pad[0:n_tok, :] = new_ref[...]
    rows = NSA_GROUP * QPAD
    tok_old = lax.rem(lax.broadcasted_iota(I32, (rows, wb), 0), QPAD)
    tok_new = lax.rem(lax.broadcasted_iota(I32, (rows, QPAD), 0), QPAD)
    dist_old = tok_old + wb - lax.broadcasted_iota(I32, (rows, wb), 1)
    new_col = lax.broadcasted_iota(I32, (rows, QPAD), 1)
    dist_new = tok_new - new_col
    kpos_old = past - wb + lax.broadcasted_iota(I32, (rows, wb), 1)
    mask_old = (dist_old >= 0) & (dist_old < WINDOW) & (kpos_old >= 0)
    mask_new = (dist_new >= 0) & (dist_new < WINDOW) & (new_col < n_tok)
    tb = (((1,), (1,)), ((), ()))
    for hkv in range(NSA_KV_HEADS):
        kc = slice(hkv * HEAD_DIM, (hkv + 1) * HEAD_DIM)
        vc = slice((NSA_KV_HEADS + hkv) * HEAD_DIM, (NSA_KV_HEADS + hkv + 1) * HEAD_DIM)
        q = _group_rows(q_ref, hkv)
        s_old = lax.dot_general(q, buf_ref[:, kc].astype(BF16), tb, preferred_element_type=F32) * ATTN_SCALE
        s_new = lax.dot_general(q, new_pad[:, kc].astype(BF16), tb, preferred_element_type=F32) * ATTN_SCALE
        s_old = s_old + _rows_bias(_t5_bucket(dist_old), rel_ref, hkv)
        s_new = s_new + _rows_bias(_t5_bucket(dist_new), rel_ref, hkv)
        out = _masked_attend([s_old, s_new], [mask_old, mask_new],
                             [buf_ref[:, vc].astype(BF16), new_pad[:, vc].astype(BF16)])
        for g in range(NSA_GROUP):
            c = (hkv * NSA_GROUP + g) * HEAD_DIM
            o_ref[:, c:c + HEAD_DIM] = out[g * QPAD:(g + 1) * QPAD, :]
    roll_ref[0:wb - n_tok, :] = buf_ref[n_tok:wb, :]
    roll_ref[wb - n_tok:wb, :] = new_ref[...]


def _win_sample(rel_bias, q_pad, proj3, win_buf2d, past, n_tok):
    n_b, wb, _ = win_buf2d.shape
    win_col = (P_KV + 2 * KV_W) // KV_W
    return pl.pallas_call(
        functools.partial(_win_sample_kernel, n_tok=n_tok, past=past),
        grid=(n_b,),
        in_specs=[pl.BlockSpec(memory_space=pltpu.SMEM),
                  pl.BlockSpec((None, QPAD, NSA_HEADS * HEAD_DIM), lambda i: (i, 0, 0)),
                  pl.BlockSpec((None, n_tok, KV_W), lambda i: (i, 0, win_col)),
                  pl.BlockSpec((None, wb, KV_W), lambda i: (i, 0, 0))],
        out_specs=[pl.BlockSpec((None, QPAD, NSA_HEADS * HEAD_DIM), lambda i: (i, 0, 0)),
                   pl.BlockSpec((None, wb, KV_W), lambda i: (i, 0, 0))],
        out_shape=[jax.ShapeDtypeStruct((n_b, QPAD, NSA_HEADS * HEAD_DIM), F32),
                   jax.ShapeDtypeStruct((n_b, wb, KV_W), F32)],
        scratch_shapes=[pltpu.VMEM((QPAD, KV_W), F32)],
        compiler_params=_cparams("parallel"),
        name="win_sample",
    )(rel_bias, q_pad, proj3, win_buf2d)


CMP_GROUP = 64
CMP_PITCH = CMP_BLOCK + 8
N_KV_SLABS = KV_W // HEAD_DIM
CMP_ROW_PITCH = CMP_BLOCK * N_KV_SLABS + 8


def _compress_kernel(blk_ref, src_hbm, w1_ref, pe_ref, b1_ref, w2_ref, o_ref, buf, flat, c1_scr, sem,
                     *, col0, n_grp):
    i = pl.program_id(0)
    slot = lax.rem(i, 2)
    g = CMP_GROUP
    row_view = col0 is None
    blk_rows = CMP_BLOCK * N_KV_SLABS

    def fetch(grp, to_slot, k, slab):
        blk = blk_ref[grp * g + k]
        if row_view:
            return pltpu.make_async_copy(
                src_hbm.at[pl.ds(pl.multiple_of(blk * blk_rows, blk_rows), blk_rows), :],
                buf.at[to_slot, pl.ds(k * CMP_ROW_PITCH, blk_rows), :], sem.at[to_slot])
        return pltpu.make_async_copy(
            src_hbm.at[pl.ds(pl.multiple_of(blk * CMP_BLOCK, CMP_BLOCK), CMP_BLOCK),
                       pl.ds(col0 + slab * HEAD_DIM, HEAD_DIM)],
            buf.at[to_slot, pl.ds((slab * g + k) * CMP_PITCH, CMP_BLOCK), :], sem.at[to_slot])

    def fetch_group(grp, to_slot, wait):
        for k in range(g):
            for slab in range(1 if row_view else N_KV_SLABS):
                cp = fetch(grp, to_slot, k, slab)
                cp.wait() if wait else cp.start()

    def block_rows(r, slab):
        if row_view:
            return buf[slot, pl.ds(r * N_KV_SLABS + slab, g, stride=CMP_ROW_PITCH), :]
        return buf[slot, pl.ds(slab * g * CMP_PITCH + r, g, stride=CMP_PITCH), :]

    @pl.when(i == 0)
    def _():
        fetch_group(0, 0, False)
        for s in range(2):
            pe_rows = jnp.broadcast_to(pe_ref[s:s + 1, :], (SUBLANES, CMP_BLOCK * HEAD_DIM)).astype(BF16)
            c1_scr[s:s + 1, :] = jnp.dot(pe_rows, w1_ref[s], preferred_element_type=F32)[0:1, :] + b1_ref[s:s + 1, :]

    @pl.when(i + 1 < n_grp)
    def _():
        fetch_group(i + 1, 1 - slot, False)

    fetch_group(i, slot, True)
    for s in range(2):
        for h in range(NSA_KV_HEADS):
            for r in range(CMP_BLOCK):
                flat[h * g:(h + 1) * g, r * HEAD_DIM:(r + 1) * HEAD_DIM] = (
                    block_rows(r, s * NSA_KV_HEADS + h).astype(BF16))
        acc = jnp.dot(flat[...], w1_ref[s], preferred_element_type=F32)
        hid = _silu(acc + c1_scr[s:s + 1, :]).astype(BF16)
        out = jnp.dot(hid, w2_ref[s], preferred_element_type=F32)
        for h in range(NSA_KV_HEADS):
            c = (s * NSA_KV_HEADS + h) * HEAD_DIM
            o_ref[:, c:c + HEAD_DIM] = out[h * g:(h + 1) * g, :]


def _compress(src2d, col0, blk_rows, cmp_pe, cmp_w1, cmp_b1, cmp_w2):
    n_blocks = blk_rows.shape[0]
    n_grp = n_blocks // CMP_GROUP
    stage_rows = CMP_GROUP * (CMP_ROW_PITCH if col0 is None else N_KV_SLABS * CMP_PITCH)
    pe_flat = cmp_pe.transpose(1, 0, 2).reshape(2, CMP_BLOCK * HEAD_DIM)
    return pl.pallas_call(
        functools.partial(_compress_kernel, col0=col0, n_grp=n_grp),
        grid_spec=pltpu.PrefetchScalarGridSpec(
            num_scalar_prefetch=1,
            grid=(n_grp,),
            in_specs=[pl.BlockSpec(memory_space=pl.ANY),
                      pl.BlockSpec((2, CMP_BLOCK * HEAD_DIM, HEAD_DIM), lambda i, br: (0, 0, 0)),
                      pl.BlockSpec((2, CMP_BLOCK * HEAD_DIM), lambda i, br: (0, 0)),
                      pl.BlockSpec((2, HEAD_DIM), lambda i, br: (0, 0)),
                      pl.BlockSpec((2, HEAD_DIM, HEAD_DIM), lambda i, br: (0, 0, 0))],
            out_specs=pl.BlockSpec((CMP_GROUP, KV_W), lambda i, br: (i, 0)),
            scratch_shapes=[pltpu.VMEM((2, stage_rows, HEAD_DIM), F32),
                            pltpu.VMEM((NSA_KV_HEADS * CMP_GROUP, CMP_BLOCK * HEAD_DIM), BF16),
                            pltpu.VMEM((2, HEAD_DIM), F32),
                            pltpu.SemaphoreType.DMA((2,))]),
        out_shape=jax.ShapeDtypeStruct((n_blocks, KV_W), F32),
        compiler_params=_cparams("arbitrary"),
        name="compress",
    )(blk_rows, src2d, cmp_w1.astype(BF16), pe_flat, cmp_b1, cmp_w2.astype(BF16))


CONV_PAD = SUBLANES
def _split_bf16(x, parts):
    out = []
    for _ in range(parts):
        piece = x.astype(BF16)
        out.append(piece)
        x = x - piece.astype(F32)
    return out


def _dot_hi(a, b):
    (ah, al), (bh, bl) = _split_bf16(a, 2), _split_bf16(b, 2)
    dot = functools.partial(jnp.dot, preferred_element_type=F32)
    return dot(ah, bh) + (dot(ah, bl) + dot(al, bh))


def _dot_mask(mask, x):
    m = mask.astype(BF16)
    return sum(jnp.dot(m, piece, preferred_element_type=F32) for piece in _split_bf16(x, 3))


def _unit_lower_inverses(lmats, c):
    eye = (lax.broadcasted_iota(I32, (c, c), 0) == lax.broadcasted_iota(I32, (c, c), 1)).astype(F32)
    x = [eye - m for m in lmats]
    p = list(lmats)
    span = 2
    while span < c:
        p = [_dot_hi(m, m) for m in p]
        x = [a + _dot_hi(a, m) for a, m in zip(x, p)]
        span *= 2
    return x


def _gdn_kernel(qkv_ref, z_ref, sm_ref, cw_ref, alog_ref, dtb_ref, nw_ref, conv0_ref, s0_ref,
                o_ref, sout_ref, cout_ref, xbuf, qkvc, s_scr, *, tl, chunk, l_valid, nt):
    t = pl.program_id(1)
    n_t = nt
    tail = CONV_WIDTH - 1

    @pl.when(t == 0)
    def _():
        xbuf[CONV_PAD - tail:CONV_PAD, :] = conv0_ref[...]
        s_scr[...] = s0_ref[...]

    xbuf[CONV_PAD:CONV_PAD + tl, :] = qkv_ref[...]
    for cb in range(GDN_CONV_DIM // LANES):
        cols = slice(cb * LANES, (cb + 1) * LANES)
        y = xbuf[CONV_PAD - tail:CONV_PAD - tail + tl, cols] * cw_ref[0:1, cols]
        for j in range(1, CONV_WIDTH):
            y = y + xbuf[CONV_PAD - tail + j:CONV_PAD - tail + j + tl, cols] * cw_ref[j:j + 1, cols]
        qkvc[:, cols] = _silu(y)

    lv = l_valid - t * tl
    small = sm_ref[...]
    live = lax.broadcasted_iota(I32, (tl, LANES), 0) < lv
    beta = jnp.where(live, jax.nn.sigmoid(small), 0.0)
    g = jnp.where(live, -jnp.exp(alog_ref[...]) * jax.nn.softplus(small + dtb_ref[...]), 0.0)
    ri = lax.broadcasted_iota(I32, (tl, tl), 0)
    ci = lax.broadcasted_iota(I32, (tl, tl), 1)
    cshift = chunk.bit_length() - 1
    same = lax.shift_right_logical(ri, cshift) == lax.shift_right_logical(ci, cshift)
    gc = _dot_mask(jnp.where(same & (ci <= ri), 1.0, 0.0), g)
    gl = _dot_mask(jnp.where(same, 1.0, 0.0), g)
    gc_t = gc.T
    low = lax.broadcasted_iota(I32, (chunk, chunk), 0) >= lax.broadcasted_iota(I32, (chunk, chunk), 1)
    strict = lax.broadcasted_iota(I32, (chunk, chunk), 0) > lax.broadcasted_iota(I32, (chunk, chunk), 1)
    tb = (((1,), (1,)), ((), ()))

    heads = range(GDN_HEADS)
    dot = functools.partial(jnp.dot, preferred_element_type=F32)
    jobs = [(c, h) for c in range(tl // chunk) for h in heads]
    rows_of = lambda c: slice(c * chunk, (c + 1) * chunk)
    col = lambda arr, c, lane: arr[rows_of(c), lane:lane + 1]

    def unit_rows(c, base, h):
        x = qkvc[rows_of(c), base + h * HEAD_DIM:base + (h + 1) * HEAD_DIM]
        return x * lax.rsqrt(jnp.sum(x * x, axis=-1, keepdims=True) + NORM_EPS)

    qn = [unit_rows(c, 0, h) * (HEAD_DIM ** -0.5) for c, h in jobs]
    kn = [unit_rows(c, GDN_QK, h) for c, h in jobs]
    vh = [qkvc[rows_of(c), 2 * GDN_QK + h * HEAD_DIM:2 * GDN_QK + (h + 1) * HEAD_DIM] for c, h in jobs]
    bcol = [col(beta, c, SM_BETA + h) for c, h in jobs]
    gcol = [col(gc, c, SM_DECAY + h) for c, h in jobs]
    glcol = [col(gl, c, SM_DECAY + h) for c, h in jobs]
    decay = [jnp.exp(jnp.where(low, g_c - gc_t[SM_DECAY + h:SM_DECAY + h + 1, rows_of(c)], NEG_INF))
             for (c, h), g_c in zip(jobs, gcol)]
    kb = [k * b for k, b in zip(kn, bcol)]
    lmat = [jnp.where(strict, lax.dot_general(a, k, tb, preferred_element_type=F32) * d, 0.0)
            for a, k, d in zip(kb, kn, decay)]
    inv = _unit_lower_inverses(lmat, chunk)
    e_g = [jnp.exp(g_c) for g_c in gcol]
    u = [_dot_hi(m, v * b) for m, v, b in zip(inv, vh, bcol)]
    w = [_dot_hi(m, a * e) for m, a, e in zip(inv, kb, e_g)]
    qk = [jnp.where(low, lax.dot_general(q, k, tb, preferred_element_type=F32) * d, 0.0)
          for q, k, d in zip(qn, kn, decay)]
    qg = [q * e for q, e in zip(qn, e_g)]
    kg = [k * jnp.exp(gl_c - g_c) for k, gl_c, g_c in zip(kn, glcol, gcol)]
    g_end = [jnp.exp(gl_c[0:1, :]) for gl_c in glcol]

    states = [s_scr[h] for h in heads]
    for c in range(tl // chunk):
        j0 = c * GDN_HEADS
        v_new = [u[j0 + h] - dot(w[j0 + h], states[h]) for h in heads]
        o = [dot(qg[j0 + h], states[h]) + dot(qk[j0 + h], v_new[h]) for h in heads]
        states = [states[h] * g_end[j0 + h] + lax.dot_general(kg[j0 + h], v_new[h], (((0,), (0,)), ((), ())),
                                                              preferred_element_type=F32) for h in heads]
        for h in heads:
            hc = slice(h * HEAD_DIM, (h + 1) * HEAD_DIM)
            y = o[h] * lax.rsqrt(jnp.mean(o[h] * o[h], axis=-1, keepdims=True) + NORM_EPS) * nw_ref[...]
            o_ref[rows_of(c), hc] = y * _silu(z_ref[rows_of(c), hc])
    for h in heads:
        s_scr[h] = states[h]

    @pl.when(t == n_t - 1)
    def _():
        sout_ref[...] = s_scr[...]
        last = l_valid - (nt - 1) * tl
        cout_ref[...] = xbuf[pl.ds(CONV_PAD - tail + last, tail), :]

    xbuf[CONV_PAD - tail:CONV_PAD, :] = xbuf[CONV_PAD - tail + tl:CONV_PAD + tl, :]


def _gdn(proj, b, l_pad, l_valid, conv0, s0, conv_w, a_log, dt_bias, norm_w, tl, chunk):
    nt = l_pad // tl
    lane_pad = lambda v: jnp.zeros((1, LANES), F32).at[0, SM_DECAY:SM_DECAY + GDN_HEADS].set(v)
    return pl.pallas_call(
        functools.partial(_gdn_kernel, tl=tl, chunk=chunk, l_valid=l_valid, nt=nt),
        grid=(b, nt),
        in_specs=[pl.BlockSpec((tl, GDN_CONV_DIM), lambda i, t: (i * nt + t, P_QKV // GDN_CONV_DIM)),
                  pl.BlockSpec((tl, GDN_QK), lambda i, t: (i * nt + t, P_Z // GDN_QK)),
                  pl.BlockSpec((tl, LANES), lambda i, t: (i * nt + t, P_SMALL // LANES)),
                  pl.BlockSpec((CONV_WIDTH, GDN_CONV_DIM), lambda i, t: (0, 0)),
                  pl.BlockSpec((1, LANES), lambda i, t: (0, 0)),
                  pl.BlockSpec((1, LANES), lambda i, t: (0, 0)),
                  pl.BlockSpec((1, HEAD_DIM), lambda i, t: (0, 0)),
                  pl.BlockSpec((None, CONV_WIDTH - 1, GDN_CONV_DIM), lambda i, t: (i, 0, 0)),
                  pl.BlockSpec((None, GDN_HEADS, HEAD_DIM, HEAD_DIM), lambda i, t: (i, 0, 0, 0))],
        out_specs=[pl.BlockSpec((tl, GDN_QK), lambda i, t: (i * nt + t, 0)),
                   pl.BlockSpec((None, GDN_HEADS, HEAD_DIM, HEAD_DIM), lambda i, t: (i, 0, 0, 0)),
                   pl.BlockSpec((None, CONV_WIDTH - 1, GDN_CONV_DIM), lambda i, t: (i, 0, 0))],
        out_shape=[jax.ShapeDtypeStruct((b * l_pad, GDN_QK), F32),
                   jax.ShapeDtypeStruct((b, GDN_HEADS, HEAD_DIM, HEAD_DIM), F32),
                   jax.ShapeDtypeStruct((b, CONV_WIDTH - 1, GDN_CONV_DIM), F32)],
        scratch_shapes=[pltpu.VMEM((CONV_PAD + tl, GDN_CONV_DIM), F32),
                        pltpu.VMEM((tl, GDN_CONV_DIM), F32),
                        pltpu.VMEM((GDN_HEADS, HEAD_DIM, HEAD_DIM), F32)],
        compiler_params=_cparams("parallel", "arbitrary"),
        name="gdn",
    )(proj, proj, proj, conv_w, lane_pad(a_log), lane_pad(dt_bias), norm_w[None, :], conv0, s0)


SEL_BLOCK = CMP_BLOCK
WIN_QBLOCK = 128
SEL_QBLOCK = 32
PAGE_SIZE = 128


def l2_normalize(x):
    return x * lax.rsqrt(jnp.sum(x * x, axis=-1, keepdims=True) + NORM_EPS)


def t5_bucket(dist):
    n = jnp.maximum(dist, 0)
    max_exact = REL_BUCKETS // 2
    nf = jnp.maximum(n, 1).astype(jnp.float32)
    large = max_exact + (jnp.log(nf / max_exact) / math.log(REL_MAX_DIST / max_exact)
                         * (REL_BUCKETS - max_exact)).astype(jnp.int32)
    return jnp.where(n < max_exact, n, jnp.minimum(large, REL_BUCKETS - 1))


def masked_probs(s, mask):
    s = jnp.where(mask, s.astype(jnp.float32), NEG_INF)
    return jax.nn.softmax(s, axis=-1) * mask


def short_conv(x, buf, w):
    L = x.shape[1]
    xp = jnp.concatenate([buf.astype(x.dtype), x], axis=1)
    y = sum(xp[:, j:j + L] * w[j] for j in range(CONV_WIDTH))
    return jax.nn.silu(y), xp[:, L:]


def gated_delta_chunked(q, k, v, g, beta, s0):
    B, H, L, dk = q.shape
    dv = v.shape[-1]
    C = math.gcd(L, GDN_CHUNK)
    n = L // C

    def chunks(t):
        return t.reshape(B, H, n, C, *t.shape[3:])

    q, k, v, g, beta = (chunks(t) for t in (q, k, v, g, beta))
    gc = jnp.cumsum(g, axis=-1)
    lower = jnp.tril(jnp.ones((C, C), bool))
    strict = jnp.tril(jnp.ones((C, C), bool), -1)
    decay = jnp.exp(jnp.where(lower, gc[..., :, None] - gc[..., None, :], NEG_INF))
    kb = k * beta[..., None]
    lmat = jnp.where(strict, jnp.einsum('bhncd,bhnjd->bhncj', kb, k) * decay, 0.0)
    rhs = jnp.concatenate([v * beta[..., None], kb * jnp.exp(gc)[..., None]], axis=-1)
    sol = lax.linalg.triangular_solve(lmat + jnp.eye(C, dtype=lmat.dtype), rhs,
                                      left_side=True, lower=True, unit_diagonal=True)
    u, w = sol[..., :dv], sol[..., dv:]
    qk = jnp.where(lower, jnp.einsum('bhncd,bhnjd->bhncj', q, k) * decay, 0.0)
    qg = q * jnp.exp(gc)[..., None]
    kg = k * jnp.exp(gc[..., -1:] - gc)[..., None]
    g_last = jnp.exp(gc[..., -1])

    def step(S, xs):
        u_i, w_i, qk_i, qg_i, kg_i, gl_i = xs
        v_new = u_i - jnp.einsum('bhcd,bhde->bhce', w_i, S)
        o = jnp.einsum('bhcd,bhde->bhce', qg_i, S) + jnp.einsum('bhcj,bhje->bhce', qk_i, v_new)
        S = S * gl_i[..., None, None] + jnp.einsum('bhcd,bhce->bhde', kg_i, v_new)
        return S, o

    xs = tuple(jnp.moveaxis(t, 2, 0) for t in (u, w, qk, qg, kg, g_last))
    S, o = lax.scan(step, s0, xs)
    return jnp.moveaxis(o, 0, 2).reshape(B, H, L, dv), S


def gdn_mixer(qkv, z, b_raw, a_raw, conv_buf, s0, conv_w, a_log, dt_bias, norm_w):
    B, L, _ = qkv.shape
    qkv_c, new_buf = short_conv(qkv, conv_buf, conv_w)
    qc, kc, vc = jnp.split(qkv_c, [GDN_QK, 2 * GDN_QK], axis=-1)

    def heads(t, d):
        return t.reshape(B, L, GDN_HEADS, d).transpose(0, 2, 1, 3).astype(jnp.float32)

    q = l2_normalize(heads(qc, HEAD_DIM)) * (HEAD_DIM ** -0.5)
    k = l2_normalize(heads(kc, HEAD_DIM))
    v = heads(vc, HEAD_DIM)
    beta = jax.nn.sigmoid(b_raw.astype(jnp.float32)).transpose(0, 2, 1)
    g = (-jnp.exp(a_log.astype(jnp.float32))
         * jax.nn.softplus(a_raw.astype(jnp.float32) + dt_bias.astype(jnp.float32))).transpose(0, 2, 1)
    o, s_new = gated_delta_chunked(q, k, v, g, beta, s0.astype(jnp.float32))
    o = o.transpose(0, 2, 1, 3)
    o = (o * lax.rsqrt(jnp.mean(o * o, axis=-1, keepdims=True) + NORM_EPS) * norm_w.astype(jnp.float32)
         * jax.nn.silu(z.reshape(B, L, GDN_HEADS, HEAD_DIM).astype(jnp.float32)))
    return o.reshape(B, L, GDN_HEADS * HEAD_DIM).astype(qkv.dtype), new_buf, s_new.astype(s0.dtype)


def compress_blocks(kv, pe, w1, b1, w2):
    B, Lk = kv.shape[:2]
    nb = Lk // CMP_BLOCK
    blk = kv[:, :nb * CMP_BLOCK].reshape(B, nb, CMP_BLOCK, 2, NSA_KV_HEADS, HEAD_DIM)
    blk = blk + pe[:, :, None, :]
    flat = blk.transpose(0, 1, 3, 4, 2, 5).reshape(B, nb, 2, NSA_KV_HEADS, CMP_BLOCK * HEAD_DIM)
    hid = jax.nn.silu(jnp.einsum('bnshf,sfe->bnshe', flat, w1) + b1[:, None, :])
    return jnp.einsum('bnshe,sed->bnshd', hid, w2)


def cmp_attend(q, qpos, kvc, rel_g):
    nb = kvc.shape[1]
    bend = jnp.arange(nb, dtype=jnp.int32) * CMP_BLOCK + (CMP_BLOCK - 1)
    dist = qpos[:, None] - bend[None, :]
    bias = rel_g[t5_bucket(dist)].transpose(2, 3, 0, 1)
    s = jnp.einsum('bhgqd,bnhd->bhgqn', q, kvc[:, :, 0]).astype(jnp.float32) * ATTN_SCALE + bias
    p = masked_probs(s, dist >= 0)
    o = jnp.einsum('bhgqn,bnhd->bhgqd', p, kvc[:, :, 1].astype(jnp.float32))
    return o, p


def select_blocks(p, qpos):
    score = jnp.sum(p, axis=2)
    B, Hkv, Q, nb = score.shape
    cur = qpos // SEL_BLOCK
    score = jnp.where(jnp.arange(nb)[None, :] < cur[:, None], score, -1.0)
    width = max(nb, SEL_TOPK - 1)
    score = jnp.pad(score, ((0, 0), (0, 0), (0, 0), (0, width - nb)), constant_values=-1.0)
    top_s, top_i = lax.top_k(score, SEL_TOPK - 1)
    cur_b = jnp.broadcast_to(cur[None, None, :, None], (B, Hkv, Q, 1)).astype(jnp.int32)
    idx = jnp.concatenate([cur_b, top_i.astype(jnp.int32)], axis=-1)
    valid = jnp.concatenate([jnp.ones((B, Hkv, Q, 1), bool), top_s >= 0], axis=-1)
    return idx, valid


def sel_attend(q, qpos, idx, valid, fetch, rel_g):
    B, Hkv, G, Q, dh = q.shape
    qc = math.gcd(Q, SEL_QBLOCK)
    nc = Q // qc
    qs = q.reshape(B, Hkv, G, nc, qc, dh).transpose(3, 0, 1, 2, 4, 5)
    ids = idx.reshape(B, Hkv, nc, qc, SEL_TOPK).transpose(2, 0, 1, 3, 4)
    vals = valid.reshape(B, Hkv, nc, qc, SEL_TOPK).transpose(2, 0, 1, 3, 4)
    ps = qpos.reshape(nc, qc)
    hidx = jnp.arange(Hkv)[None, :, None, None]
    offs = jnp.arange(SEL_BLOCK, dtype=jnp.int32)
    nkeys = SEL_TOPK * SEL_BLOCK

    def one(args):
        qb, ib, vb, pb = args
        kv = fetch(ib)
        kk = kv[..., 0, :].reshape(B, Hkv, qc, nkeys, dh)
        vv = kv[..., 1, :].reshape(B, Hkv, qc, nkeys, dh)
        kpos = (ib[..., None] * SEL_BLOCK + offs).reshape(B, Hkv, qc, nkeys)
        dist = pb[None, None, :, None] - kpos
        mask = jnp.broadcast_to(vb[..., None], (B, Hkv, qc, SEL_TOPK, SEL_BLOCK)).reshape(B, Hkv, qc, nkeys) & (dist >= 0)
        bias = rel_g[t5_bucket(dist), hidx].transpose(0, 1, 4, 2, 3)
        s = jnp.einsum('bhgqd,bhqkd->bhgqk', qb, kk).astype(jnp.float32) * ATTN_SCALE + bias
        p = masked_probs(s, mask[:, :, None])
        return jnp.einsum('bhgqk,bhqkd->bhgqd', p, vv.astype(jnp.float32))

    o = lax.map(one, (qs, ids, vals, ps))
    return o.transpose(1, 2, 3, 0, 4, 5).reshape(B, Hkv, G, Q, dh)


def win_attend(q, qpos, kv, kpos, rel_g):
    dist = qpos[:, None] - kpos[None, :]
    mask = (dist >= 0) & (dist < WINDOW) & (kpos[None, :] >= 0)
    bias = rel_g[t5_bucket(dist)].transpose(2, 3, 0, 1)
    s = jnp.einsum('bhgqd,bkhd->bhgqk', q, kv[:, :, 0]).astype(jnp.float32) * ATTN_SCALE + bias
    p = masked_probs(s, mask)
    return jnp.einsum('bhgqk,bkhd->bhgqd', p, kv[:, :, 1].astype(jnp.float32))


def win_attend_prompt(q, kv, rel_g):
    B, Hkv, G, L, dh = q.shape
    wq = math.gcd(L, WIN_QBLOCK)
    nq = L // wq
    kvp = jnp.pad(kv, ((0, 0), (WINDOW, 0), (0, 0), (0, 0), (0, 0)))
    qs = q.reshape(B, Hkv, G, nq, wq, dh).transpose(3, 0, 1, 2, 4, 5)

    def one(args):
        qb, i = args
        start = i * wq
        band = lax.dynamic_slice_in_dim(kvp, start, WINDOW + wq, axis=1)
        qpos = start + jnp.arange(wq, dtype=jnp.int32)
        kpos = start - WINDOW + jnp.arange(WINDOW + wq, dtype=jnp.int32)
        return win_attend(qb, qpos, band, kpos, rel_g)

    o = lax.map(one, (qs, jnp.arange(nq, dtype=jnp.int32)))
    return o.transpose(1, 2, 3, 0, 4, 5).reshape(B, Hkv, G, L, dh)


def make_nsa_prompt(rel_g, cmp_params):
    def attend(q, kv_cmp, kv_sel, kv_win):
        B, L = kv_cmp.shape[:2]
        qpos = jnp.arange(L, dtype=jnp.int32)
        o_cmp, p = cmp_attend(q, qpos, compress_blocks(kv_cmp, *cmp_params), rel_g)
        idx, valid = select_blocks(p, qpos)
        nblk = -(-L // SEL_BLOCK)
        store = jnp.pad(kv_sel, ((0, 0), (0, nblk * SEL_BLOCK - L), (0, 0), (0, 0), (0, 0)))
        bidx = jnp.arange(B)[:, None, None, None, None]
        hidx = jnp.arange(NSA_KV_HEADS)[None, :, None, None, None]
        offs = jnp.arange(SEL_BLOCK, dtype=jnp.int32)

        def fetch(ib):
            rows = jnp.clip(ib, 0, nblk - 1)[..., None] * SEL_BLOCK + offs
            return store[bidx, rows, :, hidx]

        o_sel = sel_attend(q, qpos, idx, valid, fetch, rel_g)
        o_win = win_attend_prompt(q, kv_win, rel_g)
        return o_cmp, o_sel, o_win, (kv_cmp, kv_sel, kv_win[:, L - min(WINDOW, L):])
    return attend


def make_nsa_sample(rel_g, cmp_params, pool_cmp, pool_sel, win_buf, page_table):
    def attend(q, kv_cmp, kv_sel, kv_win):
        Bd, L = kv_cmp.shape[:2]
        n_pages = page_table.shape[1]
        past = n_pages * PAGE_SIZE
        qpos = past + jnp.arange(L, dtype=jnp.int32)
        past_cmp = pool_cmp[page_table].reshape(Bd, past, 2, NSA_KV_HEADS, HEAD_DIM).astype(kv_cmp.dtype)
        kvc = jnp.concatenate([compress_blocks(past_cmp, *cmp_params),
                               compress_blocks(kv_cmp, *cmp_params)], axis=1)
        o_cmp, p = cmp_attend(q, qpos, kvc, rel_g)
        idx, valid = select_blocks(p, qpos)
        bpp = PAGE_SIZE // SEL_BLOCK
        n_past_blk = n_pages * bpp
        n_new_blk = -(-L // SEL_BLOCK)
        new_rows = jnp.pad(kv_sel, ((0, 0), (0, n_new_blk * SEL_BLOCK - L), (0, 0), (0, 0), (0, 0)))
        bidx = jnp.arange(Bd)[:, None, None, None, None]
        hidx = jnp.arange(NSA_KV_HEADS)[None, :, None, None, None]
        offs = jnp.arange(SEL_BLOCK, dtype=jnp.int32)

        def fetch(ib):
            ip = jnp.clip(ib, 0, n_past_blk - 1)
            phys = page_table[bidx[..., 0], ip // bpp][..., None]
            from_past = pool_sel[phys, (ip % bpp)[..., None] * SEL_BLOCK + offs, :, hidx]
            rows_new = jnp.clip(ib - n_past_blk, 0, n_new_blk - 1)[..., None] * SEL_BLOCK + offs
            from_new = new_rows[bidx, rows_new, :, hidx]
            return jnp.where((ib >= n_past_blk)[..., None, None, None], from_new, from_past.astype(from_new.dtype))

        o_sel = sel_attend(q, qpos, idx, valid, fetch, rel_g)
        wb = win_buf.shape[1]
        kw = jnp.concatenate([win_buf.astype(kv_win.dtype), kv_win], axis=1)
        kpos = past - wb + jnp.arange(wb + L, dtype=jnp.int32)
        o_win = win_attend(q, qpos, kw, kpos, rel_g)
        return o_cmp, o_sel, o_win, (kv_cmp, kv_sel, kw[:, L:])
    return attend


def _heads_to_rows(o):
    b, hkv, g, l, dh = o.shape
    return o.transpose(0, 3, 1, 2, 4).reshape(b * l, hkv * g * dh)


def _jax_mixers(proj, b, l, nsa_attend, conv_buf, s0, conv_w, a_log, dt_bias, norm_w):
    p3 = proj.reshape(b, l, P_DIM)
    qkv = p3[..., P_QKV:P_Z]
    z = p3[..., P_Z:P_Q]
    nsa_q = p3[..., P_Q:P_KV]
    nsa_kv = p3[..., P_KV:P_SMALL]
    small = p3[..., P_SMALL:P_SMALL + LANES]
    b_raw = small[..., SM_BETA:SM_BETA + GDN_HEADS]
    a_raw = small[..., SM_DECAY:SM_DECAY + GDN_HEADS]
    o_gdn, new_conv, new_s = gdn_mixer(qkv, z, b_raw, a_raw, conv_buf, s0, conv_w, a_log, dt_bias, norm_w)
    q = nsa_q.reshape(b, l, NSA_KV_HEADS, NSA_GROUP, HEAD_DIM).transpose(0, 2, 3, 1, 4)
    kv = nsa_kv.reshape(b, l, N_BRANCH, 2, NSA_KV_HEADS, HEAD_DIM)
    o_cmp, o_sel, o_win, nsa_state = nsa_attend(q, kv[:, :, 0], kv[:, :, 1], kv[:, :, 2])
    return (o_gdn.reshape(b * l, GDN_QK), _heads_to_rows(o_cmp), _heads_to_rows(o_sel), _heads_to_rows(o_win),
            nsa_state, new_s, new_conv)


def kernel(x_prompt, x_sample, cache_cmp_kv, cache_sel_kv, cache_win_kv, state_gdn, state_conv, page_table,
           c_prompt, c_sample, rel_bias, w_ada, b_ada, ln_mix_pre, ln_mix_post, ln_ffn_pre, ln_ffn_post,
           w_in, w_out, conv_w, gdn_a_log, gdn_dt_bias, gdn_norm, cmp_pe, cmp_w1, cmp_b1, cmp_w2,
           w_router, router_bias, w_exp_gu, w_exp_down, w_sh_gu, w_sh_down):
    bp, lp, _ = x_prompt.shape
    bs, ls, _ = x_sample.shape
    xp = x_prompt.reshape(bp * lp, D_MODEL)
    xs = x_sample.reshape(bs * ls, D_MODEL)
    mod = _ada(jnp.concatenate([c_prompt, c_sample], axis=0), w_ada.reshape(w_ada.shape[1:]), b_ada[0])
    sh1, sc1, gt1, sh2, sc2, gt2 = jnp.split(mod, 6, axis=1)
    w_in_p = _pack_w_in(w_in[0])
    proj_p = _inproj(xp, ln_mix_pre[0], sc1[:bp], sh1[:bp], w_in_p, lp, 512)
    proj_s = _inproj(xs, ln_mix_pre[0], sc1[bp:], sh1[bp:], w_in_p, ls, bs * ls)
    cmp_params = (cmp_pe[0], cmp_w1[0], cmp_b1[0], cmp_w2[0])
    gdn_params = (conv_w[0], gdn_a_log[0], gdn_dt_bias[0], gdn_norm[0])
    rel_bias = rel_bias.astype(F32)
    proj3_p = proj_p.reshape(bp, lp, P_DIM)
    proj3_s = proj_s.reshape(bs, ls, P_DIM)
    assert ls < CMP_BLOCK and ls <= QPAD, "the sample step adds less than one compressed block"

    conv0 = jnp.zeros((bp, CONV_WIDTH - 1, GDN_CONV_DIM), state_conv.dtype)
    s00 = jnp.zeros((bp, GDN_HEADS, HEAD_DIM, HEAD_DIM), state_gdn.dtype)
    o_gdn_p, gdn_p, conv_p = _gdn(proj_p, bp, lp, lp, conv0, s00, *gdn_params, 2 * GDN_CHUNK, GDN_CHUNK)
    nb_p = lp // CMP_BLOCK
    kvc_p = _compress(proj_p, P_KV, jnp.arange(bp * nb_p, dtype=I32), *cmp_params).reshape(bp, nb_p, KV_W)
    o_cmp_p, neg_p, _ = _cmp_select(proj3_p, P_Q // (NSA_GROUP * HEAD_DIM), kvc_p, rel_bias, ATT_T, 0)
    nsub = ATT_T // ATT_SUB
    d_min = 1 - nsub
    t_sel = _bias_tiles(rel_bias, d_min, (lp // ATT_T + 1) * nsub - 1, 0, 1 << 30)
    t_win = _bias_tiles(rel_bias, d_min, 3 * nsub - 1, 0, WINDOW)
    o_sel_p = _flash(proj_p, neg_p.reshape(NSA_KV_HEADS, bp * lp, nb_p), t_sel, bp, lp, 1, d_min)
    o_win_p = _flash(proj_p, None, t_win, bp, lp, 2, d_min)

    n_pages = page_table.shape[1]
    past = n_pages * PAGE_ROWS
    halves = PAGE_ROWS // CMP_BLOCK
    blk_s = (page_table[..., None] * halves + jnp.arange(halves, dtype=I32)).reshape(-1)
    assert cache_cmp_kv.shape[0] == 1, "one decoder layer"
    kvc_s = _compress(cache_cmp_kv.reshape(-1, HEAD_DIM), None, blk_s, *cmp_params)
    kvc_s = kvc_s.reshape(bs, n_pages * halves, KV_W)
    q_pad = jnp.pad(proj3_s[..., P_Q:P_KV], ((0, 0), (0, QPAD - ls), (0, 0)))
    o_cmp_s, _, picks = _cmp_select(q_pad, 0, kvc_s, rel_bias, QPAD, past)
    o_sel_s = _sel_sample(picks, page_table, rel_bias, q_pad, proj3_s, cache_sel_kv.reshape(-1, HEAD_DIM),
                          past, ls)
    wb = cache_win_kv.shape[2]
    o_win_s, win_roll = _win_sample(rel_bias, q_pad, proj3_s, cache_win_kv.reshape(bs, wb, KV_W), past, ls)
    proj_s_pad = jnp.pad(proj3_s, ((0, 0), (0, QPAD - ls), (0, 0))).reshape(bs * QPAD, P_DIM)
    o_gdn_s, gdn_s, conv_s = _gdn(proj_s_pad, bs, QPAD, ls, state_conv.reshape(bs, CONV_WIDTH - 1, GDN_CONV_DIM),
                                  state_gdn.reshape(bs, GDN_HEADS, HEAD_DIM, HEAD_DIM), *gdn_params, QPAD, QPAD)
    cut = lambda o: o.reshape(bs, QPAD, -1)[:, :ls].reshape(bs * ls, -1)

    w_out_b = w_out[0].astype(BF16)
    x1_p = _outproj(o_gdn_p, o_cmp_p.reshape(bp * lp, -1), o_sel_p, o_win_p, proj_p, xp, w_out_b, ln_mix_post[0],
                    gt1[:bp], lp, 256)
    x1_s = _outproj(cut(o_gdn_s), cut(o_cmp_s), cut(o_sel_s), cut(o_win_s), proj_s, xs, w_out_b, ln_mix_post[0],
                    gt1[bp:], ls, bs * ls)
    y_p, y_s = _moe(x1_p, x1_s, (sc2[:bp], sh2[:bp], gt2[:bp]), (sc2[bp:], sh2[bp:], gt2[bp:]), lp, ls,
                    ln_ffn_pre[0], ln_ffn_post[0], w_router[0], router_bias[0],
                    w_exp_gu.reshape(w_exp_gu.shape[1:]), w_exp_down.reshape(w_exp_down.shape[1:]),
                    w_sh_gu[0].astype(BF16), w_sh_down[0].astype(BF16))

    kv_shape = (2, NSA_KV_HEADS, HEAD_DIM)
    branch = lambda p3, br: p3[..., P_KV + br * KV_W:P_KV + (br + 1) * KV_W]
    win_p = branch(proj3_p, 2)[:, lp - min(WINDOW, lp):]
    return (y_p.reshape(x_prompt.shape), y_s.reshape(x_sample.shape),
            branch(proj3_p, 0).reshape(1, bp, lp, *kv_shape), branch(proj3_s, 0).reshape(1, bs, ls, *kv_shape),
            branch(proj3_p, 1).reshape(1, bp, lp, *kv_shape), branch(proj3_s, 1).reshape(1, bs, ls, *kv_shape),
            win_p.reshape(1, bp, win_p.shape[1], *kv_shape), win_roll.reshape(1, bs, wb, *kv_shape),
            gdn_p[None].astype(state_gdn.dtype), gdn_s[None].astype(state_gdn.dtype),
            conv_p[None].astype(state_conv.dtype), conv_s[None].astype(state_conv.dtype))
```

```python
import functools
import math

import jax
import jax.numpy as jnp
from jax import lax
from jax.experimental import pallas as pl
from jax.experimental.pallas import tpu as pltpu

F32, BF16, I32 = jnp.float32, jnp.bfloat16, jnp.int32

D_MODEL = 2048
HEAD_DIM = 128
LANES = 128
SUBLANES = 8
ROW_SLABS = D_MODEL // LANES
GDN_HEADS = 8
GDN_QK = GDN_HEADS * HEAD_DIM
GDN_CONV_DIM = 3 * GDN_QK
CONV_WIDTH = 4
GDN_CHUNK = 64
NSA_HEADS = 8
NSA_KV_HEADS = 2
NSA_GROUP = NSA_HEADS // NSA_KV_HEADS
N_BRANCH = 3
CMP_BLOCK = 64
CMP_SHIFT = CMP_BLOCK.bit_length() - 1
SEL_TOPK = 16
WINDOW = 512
REL_BUCKETS = 32
REL_MAX_DIST = 8192
MOE_TOPK = 8
N_GROUPS = 8
TOPK_GROUPS = 4
ROUTED_SCALE = 2.5
NORM_EPS = 1e-6
NEG_INF = -1e30
ATTN_SCALE = HEAD_DIM ** -0.5
KV_W = 2 * NSA_KV_HEADS * HEAD_DIM

P_QKV = 0
P_Z = P_QKV + GDN_CONV_DIM
P_Q = P_Z + GDN_QK
P_KV = P_Q + NSA_HEADS * HEAD_DIM
P_SMALL = P_KV + N_BRANCH * KV_W
P_DIM = 7168
SM_BETA, SM_DECAY, SM_GATE = 0, GDN_HEADS, 2 * GDN_HEADS
IN_SIZES = (GDN_CONV_DIM, GDN_QK, GDN_HEADS, GDN_HEADS, NSA_HEADS * HEAD_DIM, N_BRANCH * KV_W, N_BRANCH * NSA_HEADS)

VMEM_LIMIT = 56 * 1024 * 1024
ROW_CHUNK = 32
MOE_MB = 256
ROW_DMA_PRIORITY = 1


def _cparams(*sem):
    return pltpu.CompilerParams(dimension_semantics=sem, vmem_limit_bytes=VMEM_LIMIT)


def _silu(x):
    return x * jax.nn.sigmoid(x)


def _row_chunks(n_rows, body):
    def step(i, carry):
        body(pl.multiple_of(i * ROW_CHUNK, ROW_CHUNK))
        return carry
    lax.fori_loop(0, n_rows // ROW_CHUNK, step, 0)


def _rms(x):
    return x * lax.rsqrt(jnp.mean(x * x, axis=-1, keepdims=True) + NORM_EPS)


def _mod_rows(ref, r0, per_row):
    return ref[pl.ds(r0, ROW_CHUNK), :] if per_row else ref[...]


def _mod_operand(mod, tm, rows_per_batch):
    if rows_per_batch % tm == 0:
        per = rows_per_batch // tm
        return (mod[:, None, :], pl.BlockSpec((None, 1, D_MODEL), lambda i, *_: (i // per, 0, 0)), False)
    return (jnp.repeat(mod, rows_per_batch, axis=0), pl.BlockSpec((tm, D_MODEL), lambda i, *_: (i, 0)), True)


def _ada_kernel(c_ref, w_ref, b_ref, o_ref):
    a = _silu(c_ref[...]).astype(BF16)
    o_ref[...] = jnp.dot(a, w_ref[...].astype(BF16), preferred_element_type=F32) + b_ref[...]


def _ada(c, w_ada, b_ada):
    n = c.shape[0]
    npad = -(-n // SUBLANES) * SUBLANES
    cp = jnp.pad(c, ((0, npad - n), (0, 0)))
    tn = 512
    out = pl.pallas_call(
        _ada_kernel,
        grid=(w_ada.shape[1] // tn,),
        in_specs=[pl.BlockSpec((npad, D_MODEL), lambda j: (0, 0)),
                  pl.BlockSpec((D_MODEL, tn), lambda j: (0, j)),
                  pl.BlockSpec((1, tn), lambda j: (0, j))],
        out_specs=pl.BlockSpec((npad, tn), lambda j: (0, j)),
        out_shape=jax.ShapeDtypeStruct((npad, w_ada.shape[1]), F32),
        compiler_params=_cparams("parallel"),
        name="ada",
    )(cp, w_ada, b_ada[None, :])
    return out[:n]


def _inproj_kernel(x_ref, g_ref, sc_ref, sh_ref, w_ref, o_ref, h_scr, *, tm, per_row):
    @pl.when(pl.program_id(1) == 0)
    def _():
        def body(r0):
            y = _rms(x_ref[pl.ds(r0, ROW_CHUNK), :]) * g_ref[...]
            h = y * (1.0 + _mod_rows(sc_ref, r0, per_row)) + _mod_rows(sh_ref, r0, per_row)
            h_scr[pl.ds(r0, ROW_CHUNK), :] = h.astype(BF16)
        _row_chunks(tm, body)
    o_ref[...] = jnp.dot(h_scr[...], w_ref[...], preferred_element_type=F32)


def _inproj(x, ln_g, sc, sh, w_in_p, rows_per_batch, tm):
    rows = x.shape[0]
    tn = 1024
    sc_a, sc_spec, per_row = _mod_operand(sc, tm, rows_per_batch)
    sh_a, sh_spec, _ = _mod_operand(sh, tm, rows_per_batch)
    return pl.pallas_call(
        functools.partial(_inproj_kernel, tm=tm, per_row=per_row),
        grid=(rows // tm, P_DIM // tn),
        in_specs=[pl.BlockSpec((tm, D_MODEL), lambda i, j: (i, 0)),
                  pl.BlockSpec((1, D_MODEL), lambda i, j: (0, 0)),
                  sc_spec, sh_spec,
                  pl.BlockSpec((D_MODEL, tn), lambda i, j: (0, j))],
        out_specs=pl.BlockSpec((tm, tn), lambda i, j: (i, j)),
        out_shape=jax.ShapeDtypeStruct((rows, P_DIM), F32),
        scratch_shapes=[pltpu.VMEM((tm, D_MODEL), BF16)],
        compiler_params=_cparams("parallel", "arbitrary"),
        name="inproj",
    )(x, ln_g[None, :], sc_a, sh_a, w_in_p)


def _pack_w_in(w_in):
    parts = jnp.split(w_in, list(np_cumsum(IN_SIZES)[:-1]), axis=1)
    qkv, z, b_raw, a_raw, nsa_q, nsa_kv, nsa_g = parts
    small = jnp.concatenate([b_raw, a_raw, nsa_g], axis=1)
    w = jnp.concatenate([qkv, z, nsa_q, nsa_kv, small], axis=1)
    return jnp.pad(w, ((0, 0), (0, P_DIM - w.shape[1]))).astype(BF16)


def np_cumsum(sizes):
    out, acc = [], 0
    for s in sizes:
        acc += s
        out.append(acc)
    return out


def _outproj_kernel(og_ref, oc_ref, os_ref, ow_ref, sm_ref, x_ref, w_ref, g_ref, gt_ref, o_ref,
                    mix_in, mix_out, *, tm, per_row):
    def build(r0):
        rows = pl.ds(r0, ROW_CHUNK)
        mix_in[rows, :GDN_QK] = og_ref[rows, :].astype(BF16)
        gates = jax.nn.sigmoid(sm_ref[rows, :])
        for hd in range(NSA_HEADS):
            cols = slice(hd * HEAD_DIM, (hd + 1) * HEAD_DIM)
            acc = None
            for br, ref in enumerate((oc_ref, os_ref, ow_ref)):
                c = SM_GATE + br * NSA_HEADS + hd
                term = gates[:, c:c + 1] * ref[rows, cols]
                acc = term if acc is None else acc + term
            mix_in[rows, GDN_QK + hd * HEAD_DIM:GDN_QK + (hd + 1) * HEAD_DIM] = acc.astype(BF16)
    _row_chunks(tm, build)
    mix_out[...] = jnp.dot(mix_in[...], w_ref[...], preferred_element_type=F32)

    def finish(r0):
        rows = pl.ds(r0, ROW_CHUNK)
        y = _rms(mix_out[rows, :]) * g_ref[...]
        o_ref[rows, :] = x_ref[rows, :] + _mod_rows(gt_ref, r0, per_row) * y
    _row_chunks(tm, finish)


def _outproj(o_gdn, o_cmp, o_sel, o_win, proj, x, w_out_b, ln_g, gt, rows_per_batch, tm):
    rows = x.shape[0]
    gt_a, gt_spec, per_row = _mod_operand(gt, tm, rows_per_batch)
    head_spec = pl.BlockSpec((tm, GDN_QK), lambda i: (i, 0))
    row_spec = pl.BlockSpec((tm, D_MODEL), lambda i: (i, 0))
    return pl.pallas_call(
        functools.partial(_outproj_kernel, tm=tm, per_row=per_row),
        grid=(rows // tm,),
        in_specs=[head_spec, head_spec, head_spec, head_spec,
                  pl.BlockSpec((tm, LANES), lambda i: (i, P_SMALL // LANES)),
                  row_spec,
                  pl.BlockSpec((D_MODEL, D_MODEL), lambda i: (0, 0)),
                  pl.BlockSpec((1, D_MODEL), lambda i: (0, 0)),
                  gt_spec],
        out_specs=row_spec,
        out_shape=jax.ShapeDtypeStruct((rows, D_MODEL), F32),
        scratch_shapes=[pltpu.VMEM((tm, D_MODEL), BF16), pltpu.VMEM((tm, D_MODEL), F32)],
        compiler_params=_cparams("parallel"),
        name="outproj",
    )(o_gdn, o_cmp, o_sel, o_win, proj, x, w_out_b, ln_g[None, :], gt_a)


def _route_kernel(x_ref, g_ref, sc_ref, sh_ref, wr_ref, rb_ref, h_ref, ei_ref, ew_ref, cnt_ref, h_scr,
                  *, tm, per_row, n_exp):
    def body(r0):
        rows = pl.ds(r0, ROW_CHUNK)
        y = _rms(x_ref[rows, :]) * g_ref[...]
        h = y * (1.0 + _mod_rows(sc_ref, r0, per_row)) + _mod_rows(sh_ref, r0, per_row)
        h_scr[rows, :] = h
        for s in range(ROW_SLABS):
            h_ref[pl.ds(r0 * ROW_SLABS + s, ROW_CHUNK, stride=ROW_SLABS), :] = h[:, s * LANES:(s + 1) * LANES]
    _row_chunks(tm, body)

    logits = lax.dot_general(wr_ref[...], h_scr[...], (((1,), (1,)), ((), ())),
                             precision=lax.Precision.HIGHEST, preferred_element_type=F32)
    s = jax.nn.sigmoid(logits)
    sb = s + rb_ref[...]
    gsz = n_exp // N_GROUPS
    sb3 = sb.reshape(N_GROUPS, gsz, tm)
    m1 = jnp.max(sb3, axis=1)
    n_top = jnp.sum((sb3 == m1[:, None, :]).astype(F32), axis=1)
    m2 = jnp.max(jnp.where(sb3 < m1[:, None, :], sb3, -jnp.inf), axis=1)
    gscore = m1 + jnp.where(n_top >= 2.0, m1, m2)
    gid = lax.broadcasted_iota(I32, (N_GROUPS, tm), 0)
    rank = jnp.zeros((N_GROUPS, tm), F32)
    for g in range(N_GROUPS):
        row = gscore[g:g + 1, :]
        ahead = (row > gscore) | ((row == gscore) & (g < gid))
        rank = rank + ahead.astype(F32)
    gsel = rank < float(TOPK_GROUPS)
    emask = jnp.broadcast_to(gsel[:, None, :], (N_GROUPS, gsz, tm)).reshape(n_exp, tm)
    v = jnp.where(emask, sb, NEG_INF)
    eid = lax.broadcasted_iota(I32, (n_exp, tm), 0)
    idxs, wts = [], []
    taken = jnp.zeros((n_exp, tm), F32)
    for _ in range(MOE_TOPK):
        m = jnp.max(v, axis=0, keepdims=True)
        idx = jnp.min(jnp.where(v == m, eid, n_exp), axis=0, keepdims=True)
        hit = eid == idx
        wts.append(jnp.sum(jnp.where(hit, s, 0.0), axis=0, keepdims=True))
        idxs.append(idx)
        taken = taken + hit.astype(F32)
        v = jnp.where(hit, -jnp.inf, v)
    w = jnp.concatenate(wts, axis=0)
    ei_ref[...] = jnp.concatenate(idxs, axis=0)
    ew_ref[...] = w / jnp.sum(w, axis=0, keepdims=True) * ROUTED_SCALE
    cnt_ref[...] = jnp.sum(taken, axis=1, keepdims=True)


def _route(x1, ln_g, sc, sh, w_router, router_bias, rows_per_batch, tm):
    rows = x1.shape[0]
    n_exp = w_router.shape[0]
    sc_a, sc_spec, per_row = _mod_operand(sc, tm, rows_per_batch)
    sh_a, sh_spec, _ = _mod_operand(sh, tm, rows_per_batch)
    return pl.pallas_call(
        functools.partial(_route_kernel, tm=tm, per_row=per_row, n_exp=n_exp),
        grid=(rows // tm,),
        in_specs=[pl.BlockSpec((tm, D_MODEL), lambda i: (i, 0)),
                  pl.BlockSpec((1, D_MODEL), lambda i: (0, 0)),
                  sc_spec, sh_spec,
                  pl.BlockSpec((n_exp, D_MODEL), lambda i: (0, 0)),
                  pl.BlockSpec((n_exp, 1), lambda i: (0, 0))],
        out_specs=[pl.BlockSpec((tm * ROW_SLABS, LANES), lambda i: (i, 0)),
                   pl.BlockSpec((MOE_TOPK, tm), lambda i: (0, i)),
                   pl.BlockSpec((MOE_TOPK, tm), lambda i: (0, i)),
                   pl.BlockSpec((None, n_exp, 1), lambda i: (i, 0, 0))],
        out_shape=[jax.ShapeDtypeStruct((rows * ROW_SLABS, LANES), F32),
                   jax.ShapeDtypeStruct((MOE_TOPK, rows), I32),
                   jax.ShapeDtypeStruct((MOE_TOPK, rows), F32),
                   jax.ShapeDtypeStruct((rows // tm, n_exp, 1), F32)],
        scratch_shapes=[pltpu.VMEM((tm, D_MODEL), F32)],
        compiler_params=_cparams("parallel"),
        name="route",
    )(x1, ln_g[None, :], sc_a, sh_a, w_router, router_bias[:, None])


ROW_GROUP = 32


def _expert_kernel(blk_e_ref, nused_ref, nreal_ref, src_ref, nxt_ref, dst_ref, rw_ref, wgu_ref, wd_ref, h_hbm,
                   y_hbm, xbuf, xmat, ybuf, gsem, ssem, *, mb, ff, dump0, n_blk):
    i = pl.program_id(0)
    nused = nused_ref[0]
    slot = lax.rem(i, 2)
    slab_rows = mb * ROW_SLABS
    n_real = nreal_ref[i]
    n_real_next = nreal_ref[jnp.minimum(i + 1, n_blk - 1)]
    n_real_prev = nreal_ref[jnp.maximum(i - 1, 0)]

    def live_groups(n_live, fn):
        for g0 in range(0, mb, ROW_GROUP):
            @pl.when(g0 < n_live)
            def _():
                for r in range(g0, g0 + ROW_GROUP):
                    fn(r)

    def gather(idx_ref, to_slot, r):
        tok = idx_ref[0, 0, r]
        return pltpu.make_async_copy(
            h_hbm.at[pl.ds(pl.multiple_of(tok * ROW_SLABS, ROW_SLABS), ROW_SLABS), :],
            xbuf.at[pl.ds(pl.multiple_of(to_slot * slab_rows + r * ROW_SLABS, ROW_SLABS), ROW_SLABS), :],
            gsem.at[to_slot])

    def scatter(r):
        row = dst_ref[0, 0, r]
        return pltpu.make_async_copy(
            ybuf.at[pl.ds(r * ROW_SLABS, ROW_SLABS), :],
            y_hbm.at[pl.ds(pl.multiple_of(row * ROW_SLABS, ROW_SLABS), ROW_SLABS), :],
            ssem.at[0])

    @pl.when(i == 0)
    def _():
        xbuf[...] = jnp.zeros_like(xbuf)
        live_groups(n_real, lambda r: gather(src_ref, 0, r).start(priority=ROW_DMA_PRIORITY))
        ybuf[...] = jnp.zeros_like(ybuf)
        fill = pltpu.make_async_copy(ybuf, y_hbm.at[pl.ds(dump0 * ROW_SLABS, slab_rows), :], ssem.at[0])
        fill.start()
        fill.wait()

    @pl.when(i < nused)
    def _():
        @pl.when(i + 1 < nused)
        def _():
            live_groups(n_real_next, lambda r: gather(nxt_ref, 1 - slot, r).start(priority=ROW_DMA_PRIORITY))

        live_groups(n_real, lambda r: gather(src_ref, slot, r).wait())
        base = pl.multiple_of(slot * slab_rows, slab_rows)
        for s in range(ROW_SLABS):
            xmat[:, s * LANES:(s + 1) * LANES] = xbuf[pl.ds(base + s, mb, stride=ROW_SLABS), :].astype(BF16)
        gu = jnp.dot(xmat[...], wgu_ref[...].astype(BF16), preferred_element_type=F32)
        hid = (_silu(gu[:, :ff]) * gu[:, ff:]).astype(BF16)
        y = jnp.dot(hid, wd_ref[...].astype(BF16), preferred_element_type=F32) * rw_ref[...]

        @pl.when(i > 0)
        def _():
            live_groups(n_real_prev, lambda r: scatter(r).wait())
        for s in range(ROW_SLABS):
            ybuf[pl.ds(s, mb, stride=ROW_SLABS), :] = y[:, s * LANES:(s + 1) * LANES]
        live_groups(n_real, lambda r: scatter(r).start(priority=ROW_DMA_PRIORITY))

        @pl.when(i == nused - 1)
        def _():
            live_groups(n_real, lambda r: scatter(r).wait())


def _dispatch_plan(eidx, ew, counts, n_exp, mb):
    t_all = eidx.shape[0]
    n_asg = t_all * MOE_TOPK
    n_blk = -(-(n_asg + n_exp * (mb - 1)) // mb)
    n_slot = n_blk * mb
    plane_rows = t_all
    pad = (-counts) % mb
    asg = jnp.arange(n_asg, dtype=I32)
    last_key = 2 * n_exp
    pad_key = jnp.where(jnp.arange(mb - 1, dtype=I32)[None, :] < pad[:, None],
                        2 * jnp.arange(n_exp, dtype=I32)[:, None] + 1, last_key).reshape(-1)
    n_fill = n_slot - n_asg
    keys = jnp.concatenate([2 * eidx.reshape(-1), pad_key, jnp.full((n_fill - pad_key.shape[0],), last_key, I32)])
    src = jnp.concatenate([asg // MOE_TOPK, jnp.zeros((n_fill,), I32)])
    dst = jnp.concatenate([(asg % MOE_TOPK) * plane_rows + asg // MOE_TOPK, jnp.full((n_fill,), -1, I32)])
    wts = jnp.concatenate([ew.reshape(-1), jnp.zeros((n_fill,), F32)])
    keys, rows_src, rows_dst, rows_w = lax.sort((keys, src, dst, wts), num_keys=1, is_stable=True)
    n_real = jnp.sum(((keys % 2 == 0) & (keys < last_key)).reshape(n_blk, mb), axis=1).astype(I32)
    slot_id = jnp.arange(n_slot, dtype=I32)
    rows_dst = jnp.where(rows_dst < 0, MOE_TOPK * plane_rows + slot_id % mb, rows_dst)
    pends = jnp.cumsum(counts + pad)
    blk_e = jnp.minimum(jnp.searchsorted(pends, jnp.arange(n_blk, dtype=I32) * mb, side='right'),
                        n_exp - 1).astype(I32)
    nused = (pends[-1] // mb).astype(I32).reshape(1)
    return n_blk, plane_rows, rows_src, rows_dst, rows_w, blk_e, nused, n_real


def _experts(h_slabs, eidx, ew, counts, w_gu, w_down):
    n_exp, _, ff2 = w_gu.shape
    ff = ff2 // 2
    mb = MOE_MB
    n_blk, plane_rows, rows_src, rows_dst, rows_w, blk_e, nused, n_real = _dispatch_plan(eidx, ew, counts, n_exp, mb)
    idx_spec = lambda f: pl.BlockSpec((1, 1, mb), f, memory_space=pltpu.SMEM)
    y = pl.pallas_call(
        functools.partial(_expert_kernel, mb=mb, ff=ff, dump0=MOE_TOPK * plane_rows, n_blk=n_blk),
        grid_spec=pltpu.PrefetchScalarGridSpec(
            num_scalar_prefetch=3,
            grid=(n_blk,),
            in_specs=[idx_spec(lambda i, be, nu, nr: (i, 0, 0)),
                      idx_spec(lambda i, be, nu, nr: (jnp.minimum(i + 1, n_blk - 1), 0, 0)),
                      idx_spec(lambda i, be, nu, nr: (i, 0, 0)),
                      pl.BlockSpec((mb, 1), lambda i, be, nu, nr: (i, 0)),
                      pl.BlockSpec((None, D_MODEL, ff2), lambda i, be, nu, nr: (be[i], 0, 0)),
                      pl.BlockSpec((None, ff, D_MODEL), lambda i, be, nu, nr: (be[i], 0, 0)),
                      pl.BlockSpec(memory_space=pl.ANY)],
            out_specs=pl.BlockSpec(memory_space=pl.ANY),
            scratch_shapes=[pltpu.VMEM((2 * mb * ROW_SLABS, LANES), F32),
                            pltpu.VMEM((mb, D_MODEL), BF16),
                            pltpu.VMEM((mb * ROW_SLABS, LANES), F32),
                            pltpu.SemaphoreType.DMA((2,)),
                            pltpu.SemaphoreType.DMA((1,))]),
        out_shape=jax.ShapeDtypeStruct(((MOE_TOPK * plane_rows + mb) * ROW_SLABS, LANES), F32),
        compiler_params=_cparams("arbitrary"),
        name="experts",
    )(blk_e, nused, n_real, rows_src.reshape(n_blk, 1, mb), rows_src.reshape(n_blk, 1, mb),
      rows_dst.reshape(n_blk, 1, mb), rows_w[:, None], w_gu, w_down, h_slabs)
    return y


def _combine_kernel(*refs, tm, per_row, ff):
    y_refs = refs[:MOE_TOPK]
    h_ref, wgu_ref, wd_ref, x_ref, g_ref, gt_ref, o_ref, fsum, hmat, f_scr = refs[MOE_TOPK:]

    def add_planes(r0):
        rows = pl.ds(r0 * ROW_SLABS, ROW_CHUNK * ROW_SLABS)
        acc = y_refs[0][rows, :]
        for y_ref in y_refs[1:]:
            acc = acc + y_ref[rows, :]
        fsum[rows, :] = acc
    _row_chunks(tm, add_planes)
    for s in range(ROW_SLABS):
        hmat[:, s * LANES:(s + 1) * LANES] = h_ref[pl.ds(s, tm, stride=ROW_SLABS), :].astype(BF16)
    gu = jnp.dot(hmat[...], wgu_ref[...], preferred_element_type=F32)
    hid = (_silu(gu[:, :ff]) * gu[:, ff:]).astype(BF16)
    f_scr[...] = jnp.dot(hid, wd_ref[...], preferred_element_type=F32)
    for s in range(ROW_SLABS):
        f_scr[:, s * LANES:(s + 1) * LANES] += fsum[pl.ds(s, tm, stride=ROW_SLABS), :]

    def finish(r0):
        rows = pl.ds(r0, ROW_CHUNK)
        y = _rms(f_scr[rows, :]) * g_ref[...]
        o_ref[rows, :] = x_ref[rows, :] + _mod_rows(gt_ref, r0, per_row) * y
    _row_chunks(tm, finish)


def _combine(y_planes, h_slabs, row0, w_sh_gu_b, w_sh_down_b, x1, ln_g, gt, rows_per_batch, tm):
    rows = x1.shape[0]
    ff = w_sh_down_b.shape[0]
    t0 = row0 // tm
    plane_tiles = h_slabs.shape[0] // (tm * ROW_SLABS)
    gt_a, gt_spec, per_row = _mod_operand(gt, tm, rows_per_batch)
    row_spec = pl.BlockSpec((tm, D_MODEL), lambda i: (i, 0))
    plane_specs = [pl.BlockSpec((tm * ROW_SLABS, LANES), lambda i, k=k: (k * plane_tiles + t0 + i, 0))
                   for k in range(MOE_TOPK)]
    return pl.pallas_call(
        functools.partial(_combine_kernel, tm=tm, per_row=per_row, ff=ff),
        grid=(rows // tm,),
        in_specs=plane_specs + [
                  pl.BlockSpec((tm * ROW_SLABS, LANES), lambda i: (t0 + i, 0)),
                  pl.BlockSpec((D_MODEL, 2 * ff), lambda i: (0, 0)),
                  pl.BlockSpec((ff, D_MODEL), lambda i: (0, 0)),
                  row_spec,
                  pl.BlockSpec((1, D_MODEL), lambda i: (0, 0)),
                  gt_spec],
        out_specs=row_spec,
        out_shape=jax.ShapeDtypeStruct((rows, D_MODEL), F32),
        scratch_shapes=[pltpu.VMEM((tm * ROW_SLABS, LANES), F32),
                        pltpu.VMEM((tm, D_MODEL), BF16),
                        pltpu.VMEM((tm, D_MODEL), F32)],
        compiler_params=_cparams("parallel"),
        name="combine",
    )(*([y_planes] * MOE_TOPK), h_slabs, w_sh_gu_b, w_sh_down_b, x1, ln_g[None, :], gt_a)


def _moe(x1_p, x1_s, mod_p, mod_s, rpb_p, rpb_s, ln_pre, ln_post, w_router, router_bias, w_gu, w_down,
         w_sh_gu_b, w_sh_down_b):
    tm_s = x1_s.shape[0]
    h_p, ei_p, ew_p, cnt_p = _route(x1_p, ln_pre, mod_p[0], mod_p[1], w_router, router_bias, rpb_p, 256)
    h_s, ei_s, ew_s, cnt_s = _route(x1_s, ln_pre, mod_s[0], mod_s[1], w_router, router_bias, rpb_s, tm_s)
    h_all = jnp.concatenate([h_p, h_s], axis=0)
    eidx = jnp.concatenate([ei_p, ei_s], axis=1).T
    ew = jnp.concatenate([ew_p, ew_s], axis=1).T
    counts = (jnp.sum(cnt_p, axis=(0, 2)) + jnp.sum(cnt_s, axis=(0, 2))).astype(I32)
    y4 = _experts(h_all, eidx, ew, counts, w_gu, w_down)
    out_p = _combine(y4, h_all, 0, w_sh_gu_b, w_sh_down_b, x1_p, ln_post, mod_p[2], rpb_p, 128)
    out_s = _combine(y4, h_all, x1_p.shape[0], w_sh_gu_b, w_sh_down_b, x1_s, ln_post, mod_s[2], rpb_s, tm_s)
    return out_p, out_s


def _t5_bucket(dist):
    n = jnp.maximum(dist, 0)
    max_exact = REL_BUCKETS // 2
    nf = jnp.maximum(n, 1).astype(F32)
    large = max_exact + (jnp.log(nf / max_exact) / math.log(REL_MAX_DIST / max_exact)
                         * (REL_BUCKETS - max_exact)).astype(I32)
    return jnp.where(n < max_exact, n, jnp.minimum(large, REL_BUCKETS - 1))


def _bucket_bias(bucket, rel_ref, head):
    out = jnp.zeros(bucket.shape, F32)
    for b in range(REL_BUCKETS):
        out = jnp.where(bucket == b, rel_ref[b, head], out)
    return out


def _bias_tile_kernel(rel_ref, o_ref, *, d_min, lo, hi):
    d = pl.program_id(0) + d_min
    row = lax.broadcasted_iota(I32, (LANES, LANES), 0)
    col = lax.broadcasted_iota(I32, (LANES, LANES), 1)
    dist = d * LANES + row - col
    bucket = _t5_bucket(dist)
    visible = (dist >= lo) & (dist < hi)
    for h in range(NSA_HEADS):
        o_ref[h, 0] = jnp.where(visible, _bucket_bias(bucket, rel_ref, h), NEG_INF)


def _bias_tiles(rel_bias, d_min, n_d, lo, hi):
    return pl.pallas_call(
        functools.partial(_bias_tile_kernel, d_min=d_min, lo=lo, hi=hi),
        grid=(n_d,),
        in_specs=[pl.BlockSpec(memory_space=pltpu.SMEM)],
        out_specs=pl.BlockSpec((NSA_HEADS, 1, LANES, LANES), lambda d: (0, d, 0, 0)),
        out_shape=jax.ShapeDtypeStruct((NSA_HEADS, n_d, LANES, LANES), F32),
        compiler_params=_cparams("parallel"),
        name="bias_tiles",
    )(rel_bias)


CMP_ROWS = 128


def _cmp_select_kernel(rel_ref, q_ref, k_ref, v_ref, o_ref, neg_ref, idx_ref, *, tq, nb, pos0):
    hkv = pl.program_id(1)
    rq = min(tq, CMP_ROWS)
    k = k_ref[...].astype(BF16)
    v = v_ref[...].astype(BF16)
    blk = lax.broadcasted_iota(I32, (rq, nb), 1)
    blkf = blk.astype(F32)
    tile0 = pl.program_id(2) * tq

    def step(c):
        r0 = c * rq
        rows = slice(r0, r0 + rq)
        qpos = pos0 + tile0 + r0 + lax.broadcasted_iota(I32, (rq, nb), 0)
        dist = qpos - (blk * CMP_BLOCK + (CMP_BLOCK - 1))
        bucket = _t5_bucket(dist)
        seen = dist >= 0
        psum = jnp.zeros((rq, nb), F32)
        for g in range(NSA_GROUP):
            cols = slice(g * HEAD_DIM, (g + 1) * HEAD_DIM)
            s = lax.dot_general(q_ref[rows, cols].astype(BF16), k, (((1,), (1,)), ((), ())),
                                preferred_element_type=F32) * ATTN_SCALE
            bias = jnp.zeros((rq, nb), F32)
            for b in range(REL_BUCKETS):
                bias = jnp.where(bucket == b, rel_ref[b, hkv * NSA_GROUP + g], bias)
            s = jnp.where(seen, s + bias, NEG_INF)
            e = jnp.exp(s - jnp.max(s, axis=1, keepdims=True))
            p = e / jnp.sum(e, axis=1, keepdims=True) * seen.astype(F32)
            o_ref[rows, cols] = jnp.dot(p.astype(BF16), v, preferred_element_type=F32)
            psum = psum + p
        cur = lax.shift_right_logical(qpos, CMP_SHIFT)
        score = jnp.where(blk < cur, psum, -1.0)
        chosen = blk == cur
        lane = lax.broadcasted_iota(I32, (rq, SEL_TOPK), 1)
        picks = jnp.where(lane == 0, cur[:, :SEL_TOPK].astype(F32), -1.0)
        for r in range(1, SEL_TOPK):
            m = jnp.max(score, axis=1, keepdims=True)
            first = jnp.min(jnp.where(score == m, blkf, float(nb)), axis=1, keepdims=True)
            hit = blkf == first
            ok = m >= 0.0
            chosen = chosen | (hit & ok)
            picks = jnp.where(lane == r, jnp.where(ok, first, -1.0), picks)
            score = jnp.where(hit, -2.0, score)
        neg_ref[rows, :] = jnp.where(chosen, 0.0, NEG_INF).astype(BF16)
        idx_ref[rows, :] = picks.astype(I32)

    for c in range(tq // rq):
        step(c)


def _cmp_select(q3, col_blk0, kvc, rel_bias, tq, pos0):
    b, lq, _ = q3.shape
    nb = kvc.shape[1]
    gw = NSA_GROUP * HEAD_DIM
    return pl.pallas_call(
        functools.partial(_cmp_select_kernel, tq=tq, nb=nb, pos0=pos0),
        grid=(b, NSA_KV_HEADS, lq // tq),
        in_specs=[pl.BlockSpec(memory_space=pltpu.SMEM),
                  pl.BlockSpec((None, tq, gw), lambda i, h, t: (i, t, col_blk0 + h)),
                  pl.BlockSpec((None, nb, HEAD_DIM), lambda i, h, t: (i, 0, h)),
                  pl.BlockSpec((None, nb, HEAD_DIM), lambda i, h, t: (i, 0, NSA_KV_HEADS + h))],
        out_specs=[pl.BlockSpec((None, tq, gw), lambda i, h, t: (i, t, h)),
                   pl.BlockSpec((None, None, tq, nb), lambda i, h, t: (h, i, t, 0)),
                   pl.BlockSpec((None, None, tq, SEL_TOPK), lambda i, h, t: (h, i, t, 0))],
        out_shape=[jax.ShapeDtypeStruct((b, lq, NSA_HEADS * HEAD_DIM), F32),
                   jax.ShapeDtypeStruct((NSA_KV_HEADS, b, lq, nb), BF16),
                   jax.ShapeDtypeStruct((NSA_KV_HEADS, b, lq, SEL_TOPK), I32)],
        compiler_params=_cparams("parallel", "parallel", "parallel"),
        name="cmp_select",
    )(rel_bias, q3, kvc, kvc)


ATT_T = 512
ATT_SUB = 128


def _flash_kernel(*refs, selected, d_min):
    if selected:
        q_ref, neg_ref, k_ref, v_ref, t_ref, o_ref, m_scr, l_scr, acc_scr = refs
    else:
        q_ref, k_ref, v_ref, t_ref, o_ref, m_scr, l_scr, acc_scr = refs
    qi, kk = pl.program_id(2), pl.program_id(3)
    kj = kk if selected else qi - 1 + kk
    nsub = ATT_T // ATT_SUB

    @pl.when(kk == 0)
    def _():
        m_scr[...] = jnp.full(m_scr.shape, NEG_INF, F32)
        l_scr[...] = jnp.zeros(l_scr.shape, F32)
        acc_scr[...] = jnp.zeros(acc_scr.shape, F32)

    @pl.when((kj >= 0) & (kj <= qi))
    def _():
        kb = k_ref[...].astype(BF16)
        if selected:
            nb = neg_ref.shape[1]
            key_blk = lax.shift_right_logical(kj * ATT_T + lax.broadcasted_iota(I32, (ATT_T, nb), 0), CMP_SHIFT)
            onehot = jnp.where(key_blk == lax.broadcasted_iota(I32, (ATT_T, nb), 1), 1.0, 0.0)
            kb = jnp.concatenate([kb, onehot.astype(BF16)], axis=1)
        vb = v_ref[...].astype(BF16)
        d0 = (qi - kj) * nsub - d_min
        subs = range(nsub)
        rows = [slice(a * ATT_SUB, (a + 1) * ATT_SUB) for a in subs]
        qa = [jnp.concatenate([q_ref[rows[a], g * HEAD_DIM:(g + 1) * HEAD_DIM] for g in range(NSA_GROUP)],
                              axis=0).astype(BF16) for a in subs]
        if selected:
            qa = [jnp.concatenate([qa[a], jnp.concatenate([neg_ref[rows[a], :]] * NSA_GROUP, axis=0)], axis=1)
                  for a in subs]
        s = [lax.dot_general(qa[a], kb, (((1,), (1,)), ((), ())), preferred_element_type=F32) for a in subs]
        bias = [jnp.concatenate([jnp.concatenate([t_ref[g, d0 + a - c] for c in subs], axis=1)
                                 for g in range(NSA_GROUP)], axis=0) for a in subs]
        s = [s[a] * ATTN_SCALE + bias[a] for a in subs]
        m_prev = [m_scr[a] for a in subs]
        m_new = [jnp.maximum(m_prev[a], jnp.max(s[a], axis=1, keepdims=True)) for a in subs]
        alpha = [jnp.exp(m_prev[a] - m_new[a]) for a in subs]
        p = [jnp.exp(s[a] - m_new[a]) for a in subs]
        pv = [jnp.dot(p[a].astype(BF16), vb, preferred_element_type=F32) for a in subs]
        for a in subs:
            l_scr[a] = alpha[a] * l_scr[a] + jnp.sum(p[a], axis=1, keepdims=True)
            acc_scr[a] = alpha[a] * acc_scr[a] + pv[a]
            m_scr[a] = m_new[a]

    @pl.when(kk == pl.num_programs(3) - 1)
    def _():
        for a in range(nsub):
            out = acc_scr[a] / l_scr[a]
            for g in range(NSA_GROUP):
                o_ref[a * ATT_SUB:(a + 1) * ATT_SUB, g * HEAD_DIM:(g + 1) * HEAD_DIM] = out[g * ATT_SUB:(g + 1) * ATT_SUB]


def _flash(proj, neg, tiles, b, l, branch, d_min):
    selected = neg is not None
    nq = l // ATT_T
    nk = nq if selected else 2
    gw = NSA_GROUP * HEAD_DIM
    n_d = tiles.shape[1]
    k_col = (P_KV + branch * KV_W) // HEAD_DIM

    def kj_of(qi, kk):
        return jnp.clip(kk if selected else qi - 1 + kk, 0, qi)

    in_specs = [pl.BlockSpec((ATT_T, gw), lambda i, h, qi, kk: (i * nq + qi, P_Q // gw + h))]
    args = [proj]
    if selected:
        in_specs.append(pl.BlockSpec((None, ATT_T, neg.shape[2]), lambda i, h, qi, kk: (h, i * nq + qi, 0)))
        args.append(neg)
    in_specs += [pl.BlockSpec((ATT_T, HEAD_DIM), lambda i, h, qi, kk: (i * nq + kj_of(qi, kk), k_col + h)),
                 pl.BlockSpec((ATT_T, HEAD_DIM),
                              lambda i, h, qi, kk: (i * nq + kj_of(qi, kk), k_col + NSA_KV_HEADS + h)),
                 pl.BlockSpec((NSA_GROUP, n_d, LANES, LANES), lambda i, h, qi, kk: (h, 0, 0, 0))]
    args += [proj, proj, tiles]
    return pl.pallas_call(
        functools.partial(_flash_kernel, selected=selected, d_min=d_min),
        grid=(b, NSA_KV_HEADS, nq, nk),
        in_specs=in_specs,
        out_specs=pl.BlockSpec((ATT_T, gw), lambda i, h, qi, kk: (i * nq + qi, h)),
        out_shape=jax.ShapeDtypeStruct((b * l, NSA_HEADS * HEAD_DIM), F32),
        scratch_shapes=[pltpu.VMEM((ATT_T // ATT_SUB, NSA_GROUP * ATT_SUB, 1), F32),
                        pltpu.VMEM((ATT_T // ATT_SUB, NSA_GROUP * ATT_SUB, 1), F32),
                        pltpu.VMEM((ATT_T // ATT_SUB, NSA_GROUP * ATT_SUB, HEAD_DIM), F32)],
        compiler_params=_cparams("parallel", "parallel", "parallel", "arbitrary"),
        name="flash_sel" if selected else "flash_win",
    )(*args)


QPAD = SUBLANES
NEW_PAD = LANES


def _masked_attend(s, mask, parts):
    s = [jnp.where(m, x, NEG_INF) for x, m in zip(s, mask)]
    top = s[0].max(axis=1, keepdims=True)
    for x in s[1:]:
        top = jnp.maximum(top, x.max(axis=1, keepdims=True))
    e = [jnp.exp(x - top) for x in s]
    den = sum(x.sum(axis=1, keepdims=True) for x in e)
    out = None
    for x, m, v in zip(e, mask, parts):
        term = jnp.dot((x / den * m.astype(F32)).astype(BF16), v, preferred_element_type=F32)
        out = term if out is None else out + term
    return out


def _group_rows(q_ref, hkv):
    return jnp.concatenate([q_ref[:, (hkv * NSA_GROUP + g) * HEAD_DIM:(hkv * NSA_GROUP + g + 1) * HEAD_DIM]
                            for g in range(NSA_GROUP)], axis=0).astype(BF16)


def _rows_bias(bucket, rel_ref, hkv):
    return jnp.concatenate([_bucket_bias(bucket[g * QPAD:(g + 1) * QPAD], rel_ref, hkv * NSA_GROUP + g)
                            for g in range(NSA_GROUP)], axis=0)


def _sel_sample_kernel(pick_ref, page_ref, rel_ref, q_ref, new_ref, kpos_ref, own_ref, pool_hbm, o_ref,
                       blkbuf, kbuf, vbuf, sem, *, n_b, n_tok, n_pick, past, pages_per_seq):
    i = pl.program_id(0)
    n_keys = n_tok * n_pick * CMP_BLOCK
    half_pages = PAGE_ROWS // CMP_BLOCK
    blk_rows = CMP_BLOCK * N_KV_SLABS

    def fetch(b, hkv, tok, j):
        blk = jnp.maximum(pick_ref[((hkv * n_b + b) * QPAD + tok) * SEL_TOPK + 1 + j], 0)
        page = page_ref[b * pages_per_seq + blk // half_pages]
        row0 = pl.multiple_of((page * half_pages + lax.rem(blk, half_pages)) * blk_rows, blk_rows)
        p = (hkv * n_tok + tok) * n_pick + j
        return pltpu.make_async_copy(pool_hbm.at[pl.ds(row0, blk_rows), :],
                                     blkbuf.at[pl.ds(p * blk_rows, blk_rows), :], sem.at[0])

    def fetch_all(b, wait):
        for hkv in range(NSA_KV_HEADS):
            for tok in range(n_tok):
                for j in range(n_pick):
                    cp = fetch(b, hkv, tok, j)
                    cp.wait() if wait else cp.start()

    @pl.when(i == 0)
    def _():
        fetch_all(0, False)
        zeros = jnp.zeros((NEW_PAD, HEAD_DIM), F32)
        for hkv in range(NSA_KV_HEADS):
            kbuf[hkv, n_keys:n_keys + NEW_PAD, :] = zeros
            vbuf[hkv, n_keys:n_keys + NEW_PAD, :] = zeros

    fetch_all(i, True)
    for hkv in range(NSA_KV_HEADS):
        for p in range(n_tok * n_pick):
            base = (hkv * n_tok * n_pick + p) * blk_rows
            keys = pl.ds(p * CMP_BLOCK, CMP_BLOCK)
            kbuf[hkv, keys, :] = blkbuf[pl.ds(base + hkv, CMP_BLOCK, stride=N_KV_SLABS), :]
            vbuf[hkv, keys, :] = blkbuf[pl.ds(base + NSA_KV_HEADS + hkv, CMP_BLOCK, stride=N_KV_SLABS), :]

    @pl.when(i + 1 < n_b)
    def _():
        fetch_all(i + 1, False)

    kpos = kpos_ref[...]
    n_all = n_keys + NEW_PAD
    tok_of_row = lax.rem(lax.broadcasted_iota(I32, (NSA_GROUP * QPAD, n_all), 0), QPAD)
    for hkv in range(NSA_KV_HEADS):
        kbuf[hkv, n_keys:n_keys + n_tok, :] = new_ref[:, hkv * HEAD_DIM:(hkv + 1) * HEAD_DIM]
        vbuf[hkv, n_keys:n_keys + n_tok, :] = new_ref[:, (NSA_KV_HEADS + hkv) * HEAD_DIM:
                                                      (NSA_KV_HEADS + hkv + 1) * HEAD_DIM]
        q = _group_rows(q_ref, hkv)
        s = lax.dot_general(q, kbuf[hkv].astype(BF16), (((1,), (1,)), ((), ())),
                            preferred_element_type=F32) * ATTN_SCALE
        kp = kpos[hkv:hkv + 1, :]
        own = own_ref[hkv:hkv + 1, :]
        dist = past + tok_of_row - kp
        mask = (kp >= 0) & (dist >= 0) & ((own < 0) | (own == tok_of_row))
        s = s + _rows_bias(_t5_bucket(dist), rel_ref, hkv)
        out = _masked_attend([s], [mask], [vbuf[hkv].astype(BF16)])
        for g in range(NSA_GROUP):
            c = (hkv * NSA_GROUP + g) * HEAD_DIM
            o_ref[:, c:c + HEAD_DIM] = out[g * QPAD:(g + 1) * QPAD, :]


PAGE_ROWS = 128


def _sel_sample(picks, page_table, rel_bias, q_pad, proj3, pool2d, past, n_tok):
    n_b = q_pad.shape[0]
    n_pick = SEL_TOPK - 1
    n_keys = n_tok * n_pick * CMP_BLOCK
    blk = picks[:, :, :n_tok, 1:]
    kpos = jnp.where(blk[..., None] >= 0, blk[..., None] * CMP_BLOCK + jnp.arange(CMP_BLOCK, dtype=I32), -1)
    kpos = kpos.transpose(1, 0, 2, 3, 4).reshape(n_b, NSA_KV_HEADS, n_keys)
    new_pos = jnp.where(jnp.arange(NEW_PAD) < n_tok, past + jnp.arange(NEW_PAD), -1).astype(I32)
    kpos = jnp.concatenate([kpos, jnp.broadcast_to(new_pos, (n_b, NSA_KV_HEADS, NEW_PAD))], axis=2)
    own = jnp.concatenate([jnp.repeat(jnp.arange(n_tok, dtype=I32), n_pick * CMP_BLOCK),
                           jnp.full((NEW_PAD,), -1, I32)])
    own = jnp.broadcast_to(own, (NSA_KV_HEADS, n_keys + NEW_PAD))
    sel_col = (P_KV + KV_W) // KV_W
    return pl.pallas_call(
        functools.partial(_sel_sample_kernel, n_b=n_b, n_tok=n_tok, n_pick=n_pick, past=past,
                          pages_per_seq=page_table.shape[1]),
        grid_spec=pltpu.PrefetchScalarGridSpec(
            num_scalar_prefetch=2,
            grid=(n_b,),
            in_specs=[pl.BlockSpec(memory_space=pltpu.SMEM),
                      pl.BlockSpec((None, QPAD, NSA_HEADS * HEAD_DIM), lambda i, pk, pg: (i, 0, 0)),
                      pl.BlockSpec((None, n_tok, KV_W), lambda i, pk, pg: (i, 0, sel_col)),
                      pl.BlockSpec((None, NSA_KV_HEADS, n_keys + NEW_PAD), lambda i, pk, pg: (i, 0, 0)),
                      pl.BlockSpec((NSA_KV_HEADS, n_keys + NEW_PAD), lambda i, pk, pg: (0, 0)),
                      pl.BlockSpec(memory_space=pl.ANY)],
            out_specs=pl.BlockSpec((None, QPAD, NSA_HEADS * HEAD_DIM), lambda i, pk, pg: (i, 0, 0)),
            scratch_shapes=[pltpu.VMEM((NSA_KV_HEADS * n_tok * n_pick * CMP_BLOCK * N_KV_SLABS, HEAD_DIM), F32),
                            pltpu.VMEM((NSA_KV_HEADS, n_keys + NEW_PAD, HEAD_DIM), F32),
                            pltpu.VMEM((NSA_KV_HEADS, n_keys + NEW_PAD, HEAD_DIM), F32),
                            pltpu.SemaphoreType.DMA((1,))]),
        out_shape=jax.ShapeDtypeStruct((n_b, QPAD, NSA_HEADS * HEAD_DIM), F32),
        compiler_params=_cparams("arbitrary"),
        name="sel_sample",
    )(picks.reshape(-1), page_table.reshape(-1), rel_bias, q_pad, proj3, kpos, own, pool2d)


def _win_sample_kernel(rel_ref, q_ref, new_ref, buf_ref, o_ref, roll_ref, new_pad, *, n_tok, past):
    wb = buf_ref.shape[0]
    new_pad[...] = jnp.zeros(new_pad.shape, F32)
    new_pad[0:n_tok, :] = new_ref[...]
    rows = NSA_GROUP * QPAD
    tok_old = lax.rem(lax.broadcasted_iota(I32, (rows, wb), 0), QPAD)
    tok_new = lax.rem(lax.broadcasted_iota(I32, (rows, QPAD), 0), QPAD)
    dist_old = tok_old + wb - lax.broadcasted_iota(I32, (rows, wb), 1)
    new_col = lax.broadcasted_iota(I32, (rows, QPAD), 1)
    dist_new = tok_new - new_col
    kpos_old = past - wb + lax.broadcasted_iota(I32, (rows, wb), 1)
    mask_old = (dist_old >= 0) & (dist_old < WINDOW) & (kpos_old >= 0)
    mask_new = (dist_new >= 0) & (dist_new < WINDOW) & (new_col < n_tok)
    tb = (((1,), (1,)), ((), ()))
    for hkv in range(NSA_KV_HEADS):
        kc = slice(hkv * HEAD_DIM, (hkv + 1) * HEAD_DIM)
        vc = slice((NSA_KV_HEADS + hkv) * HEAD_DIM, (NSA_KV_HEADS + hkv + 1) * HEAD_DIM)
        q = _group_rows(q_ref, hkv)
        s_old = lax.dot_general(q, buf_ref[:, kc].astype(BF16), tb, preferred_element_type=F32) * ATTN_SCALE
        s_new = lax.dot_general(q, new_pad[:, kc].astype(BF16), tb, preferred_element_type=F32) * ATTN_SCALE
        s_old = s_old + _rows_bias(_t5_bucket(dist_old), rel_ref, hkv)
        s_new = s_new + _rows_bias(_t5_bucket(dist_new), rel_ref, hkv)
        out = _masked_attend([s_old, s_new], [mask_old, mask_new],
                             [buf_ref[:, vc].astype(BF16), new_pad[:, vc].astype(BF16)])
        for g in range(NSA_GROUP):
            c = (hkv * NSA_GROUP + g) * HEAD_DIM
            o_ref[:, c:c + HEAD_DIM] = out[g * QPAD:(g + 1) * QPAD, :]
    roll_ref[0:wb - n_tok, :] = buf_ref[n_tok:wb, :]
    roll_ref[wb - n_tok:wb, :] = new_ref[...]


def _win_sample(rel_bias, q_pad, proj3, win_buf2d, past, n_tok):
    n_b, wb, _ = win_buf2d.shape
    win_col = (P_KV + 2 * KV_W) // KV_W
    return pl.pallas_call(
        functools.partial(_win_sample_kernel, n_tok=n_tok, past=past),
        grid=(n_b,),
        in_specs=[pl.BlockSpec(memory_space=pltpu.SMEM),
                  pl.BlockSpec((None, QPAD, NSA_HEADS * HEAD_DIM), lambda i: (i, 0, 0)),
                  pl.BlockSpec((None, n_tok, KV_W), lambda i: (i, 0, win_col)),
                  pl.BlockSpec((None, wb, KV_W), lambda i: (i, 0, 0))],
        out_specs=[pl.BlockSpec((None, QPAD, NSA_HEADS * HEAD_DIM), lambda i: (i, 0, 0)),
                   pl.BlockSpec((None, wb, KV_W), lambda i: (i, 0, 0))],
        out_shape=[jax.ShapeDtypeStruct((n_b, QPAD, NSA_HEADS * HEAD_DIM), F32),
                   jax.ShapeDtypeStruct((n_b, wb, KV_W), F32)],
        scratch_shapes=[pltpu.VMEM((QPAD, KV_W), F32)],
        compiler_params=_cparams("parallel"),
        name="win_sample",
    )(rel_bias, q_pad, proj3, win_buf2d)


CMP_GROUP = 64
CMP_PITCH = CMP_BLOCK + 8
N_KV_SLABS = KV_W // HEAD_DIM
CMP_ROW_PITCH = CMP_BLOCK * N_KV_SLABS + 8


def _compress_kernel(blk_ref, src_hbm, w1_ref, pe_ref, b1_ref, w2_ref, o_ref, buf, flat, c1_scr, sem,
                     *, col0, n_grp):
    i = pl.program_id(0)
    slot = lax.rem(i, 2)
    g = CMP_GROUP
    row_view = col0 is None
    blk_rows = CMP_BLOCK * N_KV_SLABS

    def fetch(grp, to_slot, k, slab):
        blk = blk_ref[grp * g + k]
        if row_view:
            return pltpu.make_async_copy(
                src_hbm.at[pl.ds(pl.multiple_of(blk * blk_rows, blk_rows), blk_rows), :],
                buf.at[to_slot, pl.ds(k * CMP_ROW_PITCH, blk_rows), :], sem.at[to_slot])
        return pltpu.make_async_copy(
            src_hbm.at[pl.ds(pl.multiple_of(blk * CMP_BLOCK, CMP_BLOCK), CMP_BLOCK),
                       pl.ds(col0 + slab * HEAD_DIM, HEAD_DIM)],
            buf.at[to_slot, pl.ds((slab * g + k) * CMP_PITCH, CMP_BLOCK), :], sem.at[to_slot])

    def fetch_group(grp, to_slot, wait):
        for k in range(g):
            for slab in range(1 if row_view else N_KV_SLABS):
                cp = fetch(grp, to_slot, k, slab)
                cp.wait() if wait else cp.start()

    def block_rows(r, slab):
        if row_view:
            return buf[slot, pl.ds(r * N_KV_SLABS + slab, g, stride=CMP_ROW_PITCH), :]
        return buf[slot, pl.ds(slab * g * CMP_PITCH + r, g, stride=CMP_PITCH), :]

    @pl.when(i == 0)
    def _():
        fetch_group(0, 0, False)
        for s in range(2):
            pe_rows = jnp.broadcast_to(pe_ref[s:s + 1, :], (SUBLANES, CMP_BLOCK * HEAD_DIM)).astype(BF16)
            c1_scr[s:s + 1, :] = jnp.dot(pe_rows, w1_ref[s], preferred_element_type=F32)[0:1, :] + b1_ref[s:s + 1, :]

    @pl.when(i + 1 < n_grp)
    def _():
        fetch_group(i + 1, 1 - slot, False)

    fetch_group(i, slot, True)
    for s in range(2):
        for h in range(NSA_KV_HEADS):
            for r in range(CMP_BLOCK):
                flat[h * g:(h + 1) * g, r * HEAD_DIM:(r + 1) * HEAD_DIM] = (
                    block_rows(r, s * NSA_KV_HEADS + h).astype(BF16))
        acc = jnp.dot(flat[...], w1_ref[s], preferred_element_type=F32)
        hid = _silu(acc + c1_scr[s:s + 1, :]).astype(BF16)
        out = jnp.dot(hid, w2_ref[s], preferred_element_type=F32)
        for h in range(NSA_KV_HEADS):
            c = (s * NSA_KV_HEADS + h) * HEAD_DIM
            o_ref[:, c:c + HEAD_DIM] = out[h * g:(h + 1) * g, :]


def _compress(src2d, col0, blk_rows, cmp_pe, cmp_w1, cmp_b1, cmp_w2):
    n_blocks = blk_rows.shape[0]
    n_grp = n_blocks // CMP_GROUP
    stage_rows = CMP_GROUP * (CMP_ROW_PITCH if col0 is None else N_KV_SLABS * CMP_PITCH)
    pe_flat = cmp_pe.transpose(1, 0, 2).reshape(2, CMP_BLOCK * HEAD_DIM)
    return pl.pallas_call(
        functools.partial(_compress_kernel, col0=col0, n_grp=n_grp),
        grid_spec=pltpu.PrefetchScalarGridSpec(
            num_scalar_prefetch=1,
            grid=(n_grp,),
            in_specs=[pl.BlockSpec(memory_space=pl.ANY),
                      pl.BlockSpec((2, CMP_BLOCK * HEAD_DIM, HEAD_DIM), lambda i, br: (0, 0, 0)),
                      pl.BlockSpec((2, CMP_BLOCK * HEAD_DIM), lambda i, br: (0, 0)),
                      pl.BlockSpec((2, HEAD_DIM), lambda i, br: (0, 0)),
                      pl.BlockSpec((2, HEAD_DIM, HEAD_DIM), lambda i, br: (0, 0, 0))],
            out_specs=pl.BlockSpec((CMP_GROUP, KV_W), lambda i, br: (i, 0)),
            scratch_shapes=[pltpu.VMEM((2, stage_rows, HEAD_DIM), F32),
                            pltpu.VMEM((NSA_KV_HEADS * CMP_GROUP, CMP_BLOCK * HEAD_DIM), BF16),
                            pltpu.VMEM((2, HEAD_DIM), F32),
                            pltpu.SemaphoreType.DMA((2,))]),
        out_shape=jax.ShapeDtypeStruct((n_blocks, KV_W), F32),
        compiler_params=_cparams("arbitrary"),
        name="compress",
    )(blk_rows, src2d, cmp_w1.astype(BF16), pe_flat, cmp_b1, cmp_w2.astype(BF16))


CONV_PAD = SUBLANES
def _split_bf16(x, parts):
    out = []
    for _ in range(parts):
        piece = x.astype(BF16)
        out.append(piece)
        x = x - piece.astype(F32)
    return out


def _dot_hi(a, b):
    (ah, al), (bh, bl) = _split_bf16(a, 2), _split_bf16(b, 2)
    dot = functools.partial(jnp.dot, preferred_element_type=F32)
    return dot(ah, bh) + (dot(ah, bl) + dot(al, bh))


def _dot_mask(mask, x):
    m = mask.astype(BF16)
    return sum(jnp.dot(m, piece, preferred_element_type=F32) for piece in _split_bf16(x, 3))


def _unit_lower_inverses(lmats, c):
    eye = (lax.broadcasted_iota(I32, (c, c), 0) == lax.broadcasted_iota(I32, (c, c), 1)).astype(F32)
    x = [eye - m for m in lmats]
    p = list(lmats)
    span = 2
    while span < c:
        p = [_dot_hi(m, m) for m in p]
        x = [a + _dot_hi(a, m) for a, m in zip(x, p)]
        span *= 2
    return x


def _gdn_kernel(qkv_ref, z_ref, sm_ref, cw_ref, alog_ref, dtb_ref, nw_ref, conv0_ref, s0_ref,
                o_ref, sout_ref, cout_ref, xbuf, qkvc, s_scr, *, tl, chunk, l_valid, nt):
    t = pl.program_id(1)
    n_t = nt
    tail = CONV_WIDTH - 1

    @pl.when(t == 0)
    def _():
        xbuf[CONV_PAD - tail:CONV_PAD, :] = conv0_ref[...]
        s_scr[...] = s0_ref[...]

    xbuf[CONV_PAD:CONV_PAD + tl, :] = qkv_ref[...]
    for cb in range(GDN_CONV_DIM // LANES):
        cols = slice(cb * LANES, (cb + 1) * LANES)
        y = xbuf[CONV_PAD - tail:CONV_PAD - tail + tl, cols] * cw_ref[0:1, cols]
        for j in range(1, CONV_WIDTH):
            y = y + xbuf[CONV_PAD - tail + j:CONV_PAD - tail + j + tl, cols] * cw_ref[j:j + 1, cols]
        qkvc[:, cols] = _silu(y)

    lv = l_valid - t * tl
    small = sm_ref[...]
    live = lax.broadcasted_iota(I32, (tl, LANES), 0) < lv
    beta = jnp.where(live, jax.nn.sigmoid(small), 0.0)
    g = jnp.where(live, -jnp.exp(alog_ref[...]) * jax.nn.softplus(small + dtb_ref[...]), 0.0)
    ri = lax.broadcasted_iota(I32, (tl, tl), 0)
    ci = lax.broadcasted_iota(I32, (tl, tl), 1)
    cshift = chunk.bit_length() - 1
    same = lax.shift_right_logical(ri, cshift) == lax.shift_right_logical(ci, cshift)
    gc = _dot_mask(jnp.where(same & (ci <= ri), 1.0, 0.0), g)
    gl = _dot_mask(jnp.where(same, 1.0, 0.0), g)
    gc_t = gc.T
    low = lax.broadcasted_iota(I32, (chunk, chunk), 0) >= lax.broadcasted_iota(I32, (chunk, chunk), 1)
    strict = lax.broadcasted_iota(I32, (chunk, chunk), 0) > lax.broadcasted_iota(I32, (chunk, chunk), 1)
    tb = (((1,), (1,)), ((), ()))

    heads = range(GDN_HEADS)
    dot = functools.partial(jnp.dot, preferred_element_type=F32)
    jobs = [(c, h) for c in range(tl // chunk) for h in heads]
    rows_of = lambda c: slice(c * chunk, (c + 1) * chunk)
    col = lambda arr, c, lane: arr[rows_of(c), lane:lane + 1]

    def unit_rows(c, base, h):
        x = qkvc[rows_of(c), base + h * HEAD_DIM:base + (h + 1) * HEAD_DIM]
        return x * lax.rsqrt(jnp.sum(x * x, axis=-1, keepdims=True) + NORM_EPS)

    qn = [unit_rows(c, 0, h) * (HEAD_DIM ** -0.5) for c, h in jobs]
    kn = [unit_rows(c, GDN_QK, h) for c, h in jobs]
    vh = [qkvc[rows_of(c), 2 * GDN_QK + h * HEAD_DIM:2 * GDN_QK + (h + 1) * HEAD_DIM] for c, h in jobs]
    bcol = [col(beta, c, SM_BETA + h) for c, h in jobs]
    gcol = [col(gc, c, SM_DECAY + h) for c, h in jobs]
    glcol = [col(gl, c, SM_DECAY + h) for c, h in jobs]
    decay = [jnp.exp(jnp.where(low, g_c - gc_t[SM_DECAY + h:SM_DECAY + h + 1, rows_of(c)], NEG_INF))
             for (c, h), g_c in zip(jobs, gcol)]
    kb = [k * b for k, b in zip(kn, bcol)]
    lmat = [jnp.where(strict, lax.dot_general(a, k, tb, preferred_element_type=F32) * d, 0.0)
            for a, k, d in zip(kb, kn, decay)]
    inv = _unit_lower_inverses(lmat, chunk)
    e_g = [jnp.exp(g_c) for g_c in gcol]
    u = [_dot_hi(m, v * b) for m, v, b in zip(inv, vh, bcol)]
    w = [_dot_hi(m, a * e) for m, a, e in zip(inv, kb, e_g)]
    qk = [jnp.where(low, lax.dot_general(q, k, tb, preferred_element_type=F32) * d, 0.0)
          for q, k, d in zip(qn, kn, decay)]
    qg = [q * e for q, e in zip(qn, e_g)]
    kg = [k * jnp.exp(gl_c - g_c) for k, gl_c, g_c in zip(kn, glcol, gcol)]
    g_end = [jnp.exp(gl_c[0:1, :]) for gl_c in glcol]

    states = [s_scr[h] for h in heads]
    for c in range(tl // chunk):
        j0 = c * GDN_HEADS
        v_new = [u[j0 + h] - dot(w[j0 + h], states[h]) for h in heads]
        o = [dot(qg[j0 + h], states[h]) + dot(qk[j0 + h], v_new[h]) for h in heads]
        states = [states[h] * g_end[j0 + h] + lax.dot_general(kg[j0 + h], v_new[h], (((0,), (0,)), ((), ())),
                                                              preferred_element_type=F32) for h in heads]
        for h in heads:
            hc = slice(h * HEAD_DIM, (h + 1) * HEAD_DIM)
            y = o[h] * lax.rsqrt(jnp.mean(o[h] * o[h], axis=-1, keepdims=True) + NORM_EPS) * nw_ref[...]
            o_ref[rows_of(c), hc] = y * _silu(z_ref[rows_of(c), hc])
    for h in heads:
        s_scr[h] = states[h]

    @pl.when(t == n_t - 1)
    def _():
        sout_ref[...] = s_scr[...]
        last = l_valid - (nt - 1) * tl
        cout_ref[...] = xbuf[pl.ds(CONV_PAD - tail + last, tail), :]

    xbuf[CONV_PAD - tail:CONV_PAD, :] = xbuf[CONV_PAD - tail + tl:CONV_PAD + tl, :]


def _gdn(proj, b, l_pad, l_valid, conv0, s0, conv_w, a_log, dt_bias, norm_w, tl, chunk):
    nt = l_pad // tl
    lane_pad = lambda v: jnp.zeros((1, LANES), F32).at[0, SM_DECAY:SM_DECAY + GDN_HEADS].set(v)
    return pl.pallas_call(
        functools.partial(_gdn_kernel, tl=tl, chunk=chunk, l_valid=l_valid, nt=nt),
        grid=(b, nt),
        in_specs=[pl.BlockSpec((tl, GDN_CONV_DIM), lambda i, t: (i * nt + t, P_QKV // GDN_CONV_DIM)),
                  pl.BlockSpec((tl, GDN_QK), lambda i, t: (i * nt + t, P_Z // GDN_QK)),
                  pl.BlockSpec((tl, LANES), lambda i, t: (i * nt + t, P_SMALL // LANES)),
                  pl.BlockSpec((CONV_WIDTH, GDN_CONV_DIM), lambda i, t: (0, 0)),
                  pl.BlockSpec((1, LANES), lambda i, t: (0, 0)),
                  pl.BlockSpec((1, LANES), lambda i, t: (0, 0)),
                  pl.BlockSpec((1, HEAD_DIM), lambda i, t: (0, 0)),
                  pl.BlockSpec((None, CONV_WIDTH - 1, GDN_CONV_DIM), lambda i, t: (i, 0, 0)),
                  pl.BlockSpec((None, GDN_HEADS, HEAD_DIM, HEAD_DIM), lambda i, t: (i, 0, 0, 0))],
        out_specs=[pl.BlockSpec((tl, GDN_QK), lambda i, t: (i * nt + t, 0)),
                   pl.BlockSpec((None, GDN_HEADS, HEAD_DIM, HEAD_DIM), lambda i, t: (i, 0, 0, 0)),
                   pl.BlockSpec((None, CONV_WIDTH - 1, GDN_CONV_DIM), lambda i, t: (i, 0, 0))],
        out_shape=[jax.ShapeDtypeStruct((b * l_pad, GDN_QK), F32),
                   jax.ShapeDtypeStruct((b, GDN_HEADS, HEAD_DIM, HEAD_DIM), F32),
                   jax.ShapeDtypeStruct((b, CONV_WIDTH - 1, GDN_CONV_DIM), F32)],
        scratch_shapes=[pltpu.VMEM((CONV_PAD + tl, GDN_CONV_DIM), F32),
                        pltpu.VMEM((tl, GDN_CONV_DIM), F32),
                        pltpu.VMEM((GDN_HEADS, HEAD_DIM, HEAD_DIM), F32)],
        compiler_params=_cparams("parallel", "arbitrary"),
        name="gdn",
    )(proj, proj, proj, conv_w, lane_pad(a_log), lane_pad(dt_bias), norm_w[None, :], conv0, s0)


SEL_BLOCK = CMP_BLOCK
WIN_QBLOCK = 128
SEL_QBLOCK = 32
PAGE_SIZE = 128


def l2_normalize(x):
    return x * lax.rsqrt(jnp.sum(x * x, axis=-1, keepdims=True) + NORM_EPS)


def t5_bucket(dist):
    n = jnp.maximum(dist, 0)
    max_exact = REL_BUCKETS // 2
    nf = jnp.maximum(n, 1).astype(jnp.float32)
    large = max_exact + (jnp.log(nf / max_exact) / math.log(REL_MAX_DIST / max_exact)
                         * (REL_BUCKETS - max_exact)).astype(jnp.int32)
    return jnp.where(n < max_exact, n, jnp.minimum(large, REL_BUCKETS - 1))


def masked_probs(s, mask):
    s = jnp.where(mask, s.astype(jnp.float32), NEG_INF)
    return jax.nn.softmax(s, axis=-1) * mask


def short_conv(x, buf, w):
    L = x.shape[1]
    xp = jnp.concatenate([buf.astype(x.dtype), x], axis=1)
    y = sum(xp[:, j:j + L] * w[j] for j in range(CONV_WIDTH))
    return jax.nn.silu(y), xp[:, L:]


def gated_delta_chunked(q, k, v, g, beta, s0):
    B, H, L, dk = q.shape
    dv = v.shape[-1]
    C = math.gcd(L, GDN_CHUNK)
    n = L // C

    def chunks(t):
        return t.reshape(B, H, n, C, *t.shape[3:])

    q, k, v, g, beta = (chunks(t) for t in (q, k, v, g, beta))
    gc = jnp.cumsum(g, axis=-1)
    lower = jnp.tril(jnp.ones((C, C), bool))
    strict = jnp.tril(jnp.ones((C, C), bool), -1)
    decay = jnp.exp(jnp.where(lower, gc[..., :, None] - gc[..., None, :], NEG_INF))
    kb = k * beta[..., None]
    lmat = jnp.where(strict, jnp.einsum('bhncd,bhnjd->bhncj', kb, k) * decay, 0.0)
    rhs = jnp.concatenate([v * beta[..., None], kb * jnp.exp(gc)[..., None]], axis=-1)
    sol = lax.linalg.triangular_solve(lmat + jnp.eye(C, dtype=lmat.dtype), rhs,
                                      left_side=True, lower=True, unit_diagonal=True)
    u, w = sol[..., :dv], sol[..., dv:]
    qk = jnp.where(lower, jnp.einsum('bhncd,bhnjd->bhncj', q, k) * decay, 0.0)
    qg = q * jnp.exp(gc)[..., None]
    kg = k * jnp.exp(gc[..., -1:] - gc)[..., None]
    g_last = jnp.exp(gc[..., -1])

    def step(S, xs):
        u_i, w_i, qk_i, qg_i, kg_i, gl_i = xs
        v_new = u_i - jnp.einsum('bhcd,bhde->bhce', w_i, S)
        o = jnp.einsum('bhcd,bhde->bhce', qg_i, S) + jnp.einsum('bhcj,bhje->bhce', qk_i, v_new)
        S = S * gl_i[..., None, None] + jnp.einsum('bhcd,bhce->bhde', kg_i, v_new)
        return S, o

    xs = tuple(jnp.moveaxis(t, 2, 0) for t in (u, w, qk, qg, kg, g_last))
    S, o = lax.scan(step, s0, xs)
    return jnp.moveaxis(o, 0, 2).reshape(B, H, L, dv), S


def gdn_mixer(qkv, z, b_raw, a_raw, conv_buf, s0, conv_w, a_log, dt_bias, norm_w):
    B, L, _ = qkv.shape
    qkv_c, new_buf = short_conv(qkv, conv_buf, conv_w)
    qc, kc, vc = jnp.split(qkv_c, [GDN_QK, 2 * GDN_QK], axis=-1)

    def heads(t, d):
        return t.reshape(B, L, GDN_HEADS, d).transpose(0, 2, 1, 3).astype(jnp.float32)

    q = l2_normalize(heads(qc, HEAD_DIM)) * (HEAD_DIM ** -0.5)
    k = l2_normalize(heads(kc, HEAD_DIM))
    v = heads(vc, HEAD_DIM)
    beta = jax.nn.sigmoid(b_raw.astype(jnp.float32)).transpose(0, 2, 1)
    g = (-jnp.exp(a_log.astype(jnp.float32))
         * jax.nn.softplus(a_raw.astype(jnp.float32) + dt_bias.astype(jnp.float32))).transpose(0, 2, 1)
    o, s_new = gated_delta_chunked(q, k, v, g, beta, s0.astype(jnp.float32))
    o = o.transpose(0, 2, 1, 3)
    o = (o * lax.rsqrt(jnp.mean(o * o, axis=-1, keepdims=True) + NORM_EPS) * norm_w.astype(jnp.float32)
         * jax.nn.silu(z.reshape(B, L, GDN_HEADS, HEAD_DIM).astype(jnp.float32)))
    return o.reshape(B, L, GDN_HEADS * HEAD_DIM).astype(qkv.dtype), new_buf, s_new.astype(s0.dtype)


def compress_blocks(kv, pe, w1, b1, w2):
    B, Lk = kv.shape[:2]
    nb = Lk // CMP_BLOCK
    blk = kv[:, :nb * CMP_BLOCK].reshape(B, nb, CMP_BLOCK, 2, NSA_KV_HEADS, HEAD_DIM)
    blk = blk + pe[:, :, None, :]
    flat = blk.transpose(0, 1, 3, 4, 2, 5).reshape(B, nb, 2, NSA_KV_HEADS, CMP_BLOCK * HEAD_DIM)
    hid = jax.nn.silu(jnp.einsum('bnshf,sfe->bnshe', flat, w1) + b1[:, None, :])
    return jnp.einsum('bnshe,sed->bnshd', hid, w2)


def cmp_attend(q, qpos, kvc, rel_g):
    nb = kvc.shape[1]
    bend = jnp.arange(nb, dtype=jnp.int32) * CMP_BLOCK + (CMP_BLOCK - 1)
    dist = qpos[:, None] - bend[None, :]
    bias = rel_g[t5_bucket(dist)].transpose(2, 3, 0, 1)
    s = jnp.einsum('bhgqd,bnhd->bhgqn', q, kvc[:, :, 0]).astype(jnp.float32) * ATTN_SCALE + bias
    p = masked_probs(s, dist >= 0)
    o = jnp.einsum('bhgqn,bnhd->bhgqd', p, kvc[:, :, 1].astype(jnp.float32))
    return o, p


def select_blocks(p, qpos):
    score = jnp.sum(p, axis=2)
    B, Hkv, Q, nb = score.shape
    cur = qpos // SEL_BLOCK
    score = jnp.where(jnp.arange(nb)[None, :] < cur[:, None], score, -1.0)
    width = max(nb, SEL_TOPK - 1)
    score = jnp.pad(score, ((0, 0), (0, 0), (0, 0), (0, width - nb)), constant_values=-1.0)
    top_s, top_i = lax.top_k(score, SEL_TOPK - 1)
    cur_b = jnp.broadcast_to(cur[None, None, :, None], (B, Hkv, Q, 1)).astype(jnp.int32)
    idx = jnp.concatenate([cur_b, top_i.astype(jnp.int32)], axis=-1)
    valid = jnp.concatenate([jnp.ones((B, Hkv, Q, 1), bool), top_s >= 0], axis=-1)
    return idx, valid


def sel_attend(q, qpos, idx, valid, fetch, rel_g):
    B, Hkv, G, Q, dh = q.shape
    qc = math.gcd(Q, SEL_QBLOCK)
    nc = Q // qc
    qs = q.reshape(B, Hkv, G, nc, qc, dh).transpose(3, 0, 1, 2, 4, 5)
    ids = idx.reshape(B, Hkv, nc, qc, SEL_TOPK).transpose(2, 0, 1, 3, 4)
    vals = valid.reshape(B, Hkv, nc, qc, SEL_TOPK).transpose(2, 0, 1, 3, 4)
    ps = qpos.reshape(nc, qc)
    hidx = jnp.arange(Hkv)[None, :, None, None]
    offs = jnp.arange(SEL_BLOCK, dtype=jnp.int32)
    nkeys = SEL_TOPK * SEL_BLOCK

    def one(args):
        qb, ib, vb, pb = args
        kv = fetch(ib)
        kk = kv[..., 0, :].reshape(B, Hkv, qc, nkeys, dh)
        vv = kv[..., 1, :].reshape(B, Hkv, qc, nkeys, dh)
        kpos = (ib[..., None] * SEL_BLOCK + offs).reshape(B, Hkv, qc, nkeys)
        dist = pb[None, None, :, None] - kpos
        mask = jnp.broadcast_to(vb[..., None], (B, Hkv, qc, SEL_TOPK, SEL_BLOCK)).reshape(B, Hkv, qc, nkeys) & (dist >= 0)
        bias = rel_g[t5_bucket(dist), hidx].transpose(0, 1, 4, 2, 3)
        s = jnp.einsum('bhgqd,bhqkd->bhgqk', qb, kk).astype(jnp.float32) * ATTN_SCALE + bias
        p = masked_probs(s, mask[:, :, None])
        return jnp.einsum('bhgqk,bhqkd->bhgqd', p, vv.astype(jnp.float32))

    o = lax.map(one, (qs, ids, vals, ps))
    return o.transpose(1, 2, 3, 0, 4, 5).reshape(B, Hkv, G, Q, dh)


def win_attend(q, qpos, kv, kpos, rel_g):
    dist = qpos[:, None] - kpos[None, :]
    mask = (dist >= 0) & (dist < WINDOW) & (kpos[None, :] >= 0)
    bias = rel_g[t5_bucket(dist)].transpose(2, 3, 0, 1)
    s = jnp.einsum('bhgqd,bkhd->bhgqk', q, kv[:, :, 0]).astype(jnp.float32) * ATTN_SCALE + bias
    p = masked_probs(s, mask)
    return jnp.einsum('bhgqk,bkhd->bhgqd', p, kv[:, :, 1].astype(jnp.float32))


def win_attend_prompt(q, kv, rel_g):
    B, Hkv, G, L, dh = q.shape
    wq = math.gcd(L, WIN_QBLOCK)
    nq = L // wq
    kvp = jnp.pad(kv, ((0, 0), (WINDOW, 0), (0, 0), (0, 0), (0, 0)))
    qs = q.reshape(B, Hkv, G, nq, wq, dh).transpose(3, 0, 1, 2, 4, 5)

    def one(args):
        qb, i = args
        start = i * wq
        band = lax.dynamic_slice_in_dim(kvp, start, WINDOW + wq, axis=1)
        qpos = start + jnp.arange(wq, dtype=jnp.int32)
        kpos = start - WINDOW + jnp.arange(WINDOW + wq, dtype=jnp.int32)
        return win_attend(qb, qpos, band, kpos, rel_g)

    o = lax.map(one, (qs, jnp.arange(nq, dtype=jnp.int32)))
    return o.transpose(1, 2, 3, 0, 4, 5).reshape(B, Hkv, G, L, dh)


def make_nsa_prompt(rel_g, cmp_params):
    def attend(q, kv_cmp, kv_sel, kv_win):
        B, L = kv_cmp.shape[:2]
        qpos = jnp.arange(L, dtype=jnp.int32)
        o_cmp, p = cmp_attend(q, qpos, compress_blocks(kv_cmp, *cmp_params), rel_g)
        idx, valid = select_blocks(p, qpos)
        nblk = -(-L // SEL_BLOCK)
        store = jnp.pad(kv_sel, ((0, 0), (0, nblk * SEL_BLOCK - L), (0, 0), (0, 0), (0, 0)))
        bidx = jnp.arange(B)[:, None, None, None, None]
        hidx = jnp.arange(NSA_KV_HEADS)[None, :, None, None, None]
        offs = jnp.arange(SEL_BLOCK, dtype=jnp.int32)

        def fetch(ib):
            rows = jnp.clip(ib, 0, nblk - 1)[..., None] * SEL_BLOCK + offs
            return store[bidx, rows, :, hidx]

        o_sel = sel_attend(q, qpos, idx, valid, fetch, rel_g)
        o_win = win_attend_prompt(q, kv_win, rel_g)
        return o_cmp, o_sel, o_win, (kv_cmp, kv_sel, kv_win[:, L - min(WINDOW, L):])
    return attend


def make_nsa_sample(rel_g, cmp_params, pool_cmp, pool_sel, win_buf, page_table):
    def attend(q, kv_cmp, kv_sel, kv_win):
        Bd, L = kv_cmp.shape[:2]
        n_pages = page_table.shape[1]
        past = n_pages * PAGE_SIZE
        qpos = past + jnp.arange(L, dtype=jnp.int32)
        past_cmp = pool_cmp[page_table].reshape(Bd, past, 2, NSA_KV_HEADS, HEAD_DIM).astype(kv_cmp.dtype)
        kvc = jnp.concatenate([compress_blocks(past_cmp, *cmp_params),
                               compress_blocks(kv_cmp, *cmp_params)], axis=1)
        o_cmp, p = cmp_attend(q, qpos, kvc, rel_g)
        idx, valid = select_blocks(p, qpos)
        bpp = PAGE_SIZE // SEL_BLOCK
        n_past_blk = n_pages * bpp
        n_new_blk = -(-L // SEL_BLOCK)
        new_rows = jnp.pad(kv_sel, ((0, 0), (0, n_new_blk * SEL_BLOCK - L), (0, 0), (0, 0), (0, 0)))
        bidx = jnp.arange(Bd)[:, None, None, None, None]
        hidx = jnp.arange(NSA_KV_HEADS)[None, :, None, None, None]
        offs = jnp.arange(SEL_BLOCK, dtype=jnp.int32)

        def fetch(ib):
            ip = jnp.clip(ib, 0, n_past_blk - 1)
            phys = page_table[bidx[..., 0], ip // bpp][..., None]
            from_past = pool_sel[phys, (ip % bpp)[..., None] * SEL_BLOCK + offs, :, hidx]
            rows_new = jnp.clip(ib - n_past_blk, 0, n_new_blk - 1)[..., None] * SEL_BLOCK + offs
            from_new = new_rows[bidx, rows_new, :, hidx]
            return jnp.where((ib >= n_past_blk)[..., None, None, None], from_new, from_past.astype(from_new.dtype))

        o_sel = sel_attend(q, qpos, idx, valid, fetch, rel_g)
        wb = win_buf.shape[1]
        kw = jnp.concatenate([win_buf.astype(kv_win.dtype), kv_win], axis=1)
        kpos = past - wb + jnp.arange(wb + L, dtype=jnp.int32)
        o_win = win_attend(q, qpos, kw, kpos, rel_g)
        return o_cmp, o_sel, o_win, (kv_cmp, kv_sel, kw[:, L:])
    return attend


def _heads_to_rows(o):
    b, hkv, g, l, dh = o.shape
    return o.transpose(0, 3, 1, 2, 4).reshape(b * l, hkv * g * dh)


def _jax_mixers(proj, b, l, nsa_attend, conv_buf, s0, conv_w, a_log, dt_bias, norm_w):
    p3 = proj.reshape(b, l, P_DIM)
    qkv = p3[..., P_QKV:P_Z]
    z = p3[..., P_Z:P_Q]
    nsa_q = p3[..., P_Q:P_KV]
    nsa_kv = p3[..., P_KV:P_SMALL]
    small = p3[..., P_SMALL:P_SMALL + LANES]
    b_raw = small[..., SM_BETA:SM_BETA + GDN_HEADS]
    a_raw = small[..., SM_DECAY:SM_DECAY + GDN_HEADS]
    o_gdn, new_conv, new_s = gdn_mixer(qkv, z, b_raw, a_raw, conv_buf, s0, conv_w, a_log, dt_bias, norm_w)
    q = nsa_q.reshape(b, l, NSA_KV_HEADS, NSA_GROUP, HEAD_DIM).transpose(0, 2, 3, 1, 4)
    kv = nsa_kv.reshape(b, l, N_BRANCH, 2, NSA_KV_HEADS, HEAD_DIM)
    o_cmp, o_sel, o_win, nsa_state = nsa_attend(q, kv[:, :, 0], kv[:, :, 1], kv[:, :, 2])
    return (o_gdn.reshape(b * l, GDN_QK), _heads_to_rows(o_cmp), _heads_to_rows(o_sel), _heads_to_rows(o_win),
            nsa_state, new_s, new_conv)


def kernel(x_prompt, x_sample, cache_cmp_kv, cache_sel_kv, cache_win_kv, state_gdn, state_conv, page_table,
           c_prompt, c_sample, rel_bias, w_ada, b_ada, ln_mix_pre, ln_mix_post, ln_ffn_pre, ln_ffn_post,
           w_in, w_out, conv_w, gdn_a_log, gdn_dt_bias, gdn_norm, cmp_pe, cmp_w1, cmp_b1, cmp_w2,
           w_router, router_bias, w_exp_gu, w_exp_down, w_sh_gu, w_sh_down):
    bp, lp, _ = x_prompt.shape
    bs, ls, _ = x_sample.shape
    xp = x_prompt.reshape(bp * lp, D_MODEL)
    xs = x_sample.reshape(bs * ls, D_MODEL)
    mod = _ada(jnp.concatenate([c_prompt, c_sample], axis=0), w_ada.reshape(w_ada.shape[1:]), b_ada[0])
    sh1, sc1, gt1, sh2, sc2, gt2 = jnp.split(mod, 6, axis=1)
    w_in_p = _pack_w_in(w_in[0])
    proj_p = _inproj(xp, ln_mix_pre[0], sc1[:bp], sh1[:bp], w_in_p, lp, 512)
    proj_s = _inproj(xs, ln_mix_pre[0], sc1[bp:], sh1[bp:], w_in_p, ls, bs * ls)
    cmp_params = (cmp_pe[0], cmp_w1[0], cmp_b1[0], cmp_w2[0])
    gdn_params = (conv_w[0], gdn_a_log[0], gdn_dt_bias[0], gdn_norm[0])
    rel_bias = rel_bias.astype(F32)
    proj3_p = proj_p.reshape(bp, lp, P_DIM)
    proj3_s = proj_s.reshape(bs, ls, P_DIM)
    assert ls < CMP_BLOCK and ls <= QPAD, "the sample step adds less than one compressed block"

    conv0 = jnp.zeros((bp, CONV_WIDTH - 1, GDN_CONV_DIM), state_conv.dtype)
    s00 = jnp.zeros((bp, GDN_HEADS, HEAD_DIM, HEAD_DIM), state_gdn.dtype)
    o_gdn_p, gdn_p, conv_p = _gdn(proj_p, bp, lp, lp, conv0, s00, *gdn_params, 2 * GDN_CHUNK, GDN_CHUNK)
    nb_p = lp // CMP_BLOCK
    kvc_p = _compress(proj_p, P_KV, jnp.arange(bp * nb_p, dtype=I32), *cmp_params).reshape(bp, nb_p, KV_W)
    o_cmp_p, neg_p, _ = _cmp_select(proj3_p, P_Q // (NSA_GROUP * HEAD_DIM), kvc_p, rel_bias, ATT_T, 0)
    nsub = ATT_T // ATT_SUB
    d_min = 1 - nsub
    t_sel = _bias_tiles(rel_bias, d_min, (lp // ATT_T + 1) * nsub - 1, 0, 1 << 30)
    t_win = _bias_tiles(rel_bias, d_min, 3 * nsub - 1, 0, WINDOW)
    o_sel_p = _flash(proj_p, neg_p.reshape(NSA_KV_HEADS, bp * lp, nb_p), t_sel, bp, lp, 1, d_min)
    o_win_p = _flash(proj_p, None, t_win, bp, lp, 2, d_min)

    n_pages = page_table.shape[1]
    past = n_pages * PAGE_ROWS
    halves = PAGE_ROWS // CMP_BLOCK
    blk_s = (page_table[..., None] * halves + jnp.arange(halves, dtype=I32)).reshape(-1)
    assert cache_cmp_kv.shape[0] == 1, "one decoder layer"
    kvc_s = _compress(cache_cmp_kv.reshape(-1, HEAD_DIM), None, blk_s, *cmp_params)
    kvc_s = kvc_s.reshape(bs, n_pages * halves, KV_W)
    q_pad = jnp.pad(proj3_s[..., P_Q:P_KV], ((0, 0), (0, QPAD - ls), (0, 0)))
    o_cmp_s, _, picks = _cmp_select(q_pad, 0, kvc_s, rel_bias, QPAD, past)
    o_sel_s = _sel_sample(picks, page_table, rel_bias, q_pad, proj3_s, cache_sel_kv.reshape(-1, HEAD_DIM),
                          past, ls)
    wb = cache_win_kv.shape[2]
    o_win_s, win_roll = _win_sample(rel_bias, q_pad, proj3_s, cache_win_kv.reshape(bs, wb, KV_W), past, ls)
    proj_s_pad = jnp.pad(proj3_s, ((0, 0), (0, QPAD - ls), (0, 0))).reshape(bs * QPAD, P_DIM)
    o_gdn_s, gdn_s, conv_s = _gdn(proj_s_pad, bs, QPAD, ls, state_conv.reshape(bs, CONV_WIDTH - 1, GDN_CONV_DIM),
                                  state_gdn.reshape(bs, GDN_HEADS, HEAD_DIM, HEAD_DIM), *gdn_params, QPAD, QPAD)
    cut = lambda o: o.reshape(bs, QPAD, -1)[:, :ls].reshape(bs * ls, -1)

    w_out_b = w_out[0].astype(BF16)
    x1_p = _outproj(o_gdn_p, o_cmp_p.reshape(bp * lp, -1), o_sel_p, o_win_p, proj_p, xp, w_out_b, ln_mix_post[0],
                    gt1[:bp], lp, 256)
    x1_s = _outproj(cut(o_gdn_s), cut(o_cmp_s), cut(o_sel_s), cut(o_win_s), proj_s, xs, w_out_b, ln_mix_post[0],
                    gt1[bp:], ls, bs * ls)
    y_p, y_s = _moe(x1_p, x1_s, (sc2[:bp], sh2[:bp], gt2[:bp]), (sc2[bp:], sh2[bp:], gt2[bp:]), lp, ls,
                    ln_ffn_pre[0], ln_ffn_post[0], w_router[0], router_bias[0],
                    w_exp_gu.reshape(w_exp_gu.shape[1:]), w_exp_down.reshape(w_exp_down.shape[1:]),
                    w_sh_gu[0].astype(BF16), w_sh_down[0].astype(BF16))

    kv_shape = (2, NSA_KV_HEADS, HEAD_DIM)
    branch = lambda p3, br: p3[..., P_KV + br * KV_W:P_KV + (br + 1) * KV_W]
    win_p = branch(proj3_p, 2)[:, lp - min(WINDOW, lp):]
    return (y_p.reshape(x_prompt.shape), y_s.reshape(x_sample.shape),
            branch(proj3_p, 0).reshape(1, bp, lp, *kv_shape), branch(proj3_s, 0).reshape(1, bs, ls, *kv_shape),
            branch(proj3_p, 1).reshape(1, bp, lp, *kv_shape), branch(proj3_s, 1).reshape(1, bs, ls, *kv_shape),
            win_p.reshape(1, bp, win_p.shape[1], *kv_shape), win_roll.reshape(1, bs, wb, *kv_shape),
            gdn_p[None].astype(state_gdn.dtype), gdn_s[None].astype(state_gdn.dtype),
            conv_p[None].astype(state_conv.dtype), conv_s[None].astype(state_conv.dtype))
```
